```python
import jax
import jax.numpy as jnp
from jax import lax
import numpy as np

D_MODEL = 2048
BATCH = 2
SEQ = 8192
DEPTH = 1

F32 = jnp.float32
ATTN_HEADS = 16
ATTN_HEAD_DIM = 64
ATTN_WIDTH = ATTN_HEADS * ATTN_HEAD_DIM
HGRN_HEADS = 8
HGRN_HEAD_DIM = 128
HGRN_WIDTH = HGRN_HEADS * HGRN_HEAD_DIM
MIX_WIDTH = ATTN_WIDTH + HGRN_WIDTH
IN_PROJ_WIDTH = 3 * ATTN_WIDTH + 5 * HGRN_WIDTH
DILATED_PAIRS = ((128, 1), (512, 4), (2048, 16))
ROPE_THETA = 10000.0
HGRN_CHUNK = 64
N_EXPERTS = 256
N_EXPERT_GROUPS = 8
TOPK_GROUPS = 4
TOP_K = 8
EXPERT_DIM = 512
SHARED_DIM = 512
ROUTED_SCALE = 2.5
MOE_BLOCK = 128
ALPHA = (2 * DEPTH) ** 0.25
BETA = (8 * DEPTH) ** -0.25
LN_EPS = 1e-5
RMS_EPS = 1e-6
NEG_INF = -1e30

kernel_name = 'hybrid_dilated_attn_hgrn2_moe_encoder_layer'


def layer_norm(x, g, b):
    xf = x.astype(F32)
    mu = jnp.mean(xf, axis=-1, keepdims=True)
    var = jnp.mean(jnp.square(xf - mu), axis=-1, keepdims=True)
    return ((xf - mu) * lax.rsqrt(var + LN_EPS) * g + b).astype(x.dtype)


def head_rms_norm(t, g):
    tf = t.astype(F32)
    y = tf * lax.rsqrt(jnp.mean(jnp.square(tf), axis=-1, keepdims=True) + RMS_EPS)
    return y * g.astype(F32).reshape(t.shape[-2], t.shape[-1])


def rotary(t, positions):
    half = t.shape[-1] // 2
    inv_freq = ROPE_THETA ** (-jnp.arange(half, dtype=F32) / half)
    ang = positions.astype(F32)[:, None, :, None] * inv_freq
    cos, sin = jnp.cos(ang), jnp.sin(ang)
    t1, t2 = t[..., :half].astype(F32), t[..., half:].astype(F32)
    return jnp.concatenate([t1 * cos - t2 * sin, t2 * cos + t1 * sin], axis=-1)


def banded_local_attention(q, k, v, half):
    *lead, L, dh = q.shape
    blk = half
    nb = -(-L // blk)
    lp = nb * blk
    pad_lead = [(0, 0)] * len(lead)
    qb = jnp.pad(q, pad_lead + [(0, lp - L), (0, 0)]).reshape(*lead, nb, blk, dh)

    def band(t):
        tb = jnp.pad(t, pad_lead + [(blk, lp - L + blk), (0, 0)]).reshape(*lead, nb + 2, blk, dh)
        return jnp.concatenate([tb[..., :-2, :, :], tb[..., 1:-1, :, :], tb[..., 2:, :, :]], axis=-2)

    kb, vb = band(k), band(v)
    s = jnp.einsum('...nqd,...nkd->...nqk', qb, kb).astype(F32) * (dh ** -0.5)
    q_pos = jnp.arange(nb)[:, None, None] * blk + jnp.arange(blk)[None, :, None]
    k_pos = jnp.arange(nb)[:, None, None] * blk - blk + jnp.arange(3 * blk)[None, None, :]
    mask = (jnp.abs(k_pos - q_pos) <= half) & (k_pos >= 0) & (k_pos < L)
    s = jnp.where(mask, s, NEG_INF)
    m = jnp.max(s, axis=-1, keepdims=True)
    p = jnp.exp(s - m)
    l = jnp.sum(p, axis=-1, keepdims=True)
    o = jnp.einsum('...nqk,...nkd->...nqd', (p / l).astype(vb.dtype), vb)
    lse = (m + jnp.log(l))[..., 0]
    return o.reshape(*lead, lp, dh)[..., :L, :], lse.reshape(*lead, lp)[..., :L]


def to_strided(t, dilation):
    B, H, S, dh = t.shape
    return t.reshape(B, H, S // dilation, dilation, dh).transpose(0, 1, 3, 2, 4)


def dilated_window_attention(q, k, v):
    B, H, S, dh = q.shape
    outs, lses = [], []
    for window, dilation in DILATED_PAIRS:
        half = window // (2 * dilation)
        o, lse = banded_local_attention(to_strided(q, dilation), to_strided(k, dilation),
                                        to_strided(v, dilation), half)
        outs.append(o.transpose(0, 1, 3, 2, 4).reshape(B, H, S, dh).astype(F32))
        lses.append(lse.transpose(0, 1, 3, 2).reshape(B, H, S))
    w = jax.nn.softmax(jnp.stack(lses), axis=0)
    return jnp.einsum('rbhs,rbhsd->bhsd', w, jnp.stack(outs))


def hgrn2_bidirectional(q, f_fwd, f_bwd, v, lb_fwd, lb_bwd):
    B, S, H, dh = q.shape
    C = HGRN_CHUNK
    N = S // C

    def gates(z, lb):
        lb = lb.astype(F32).reshape(H, dh)
        zf = z.astype(F32)
        return jnp.log(lb + (1.0 - lb) * jax.nn.sigmoid(zf)), (1.0 - lb) * jax.nn.sigmoid(-zf)

    logf_fw, k_fw = gates(f_fwd, lb_fwd)
    logf_bw, k_bw = gates(jnp.flip(f_bwd, axis=1), lb_bwd)
    qf, vf = q.astype(F32), v.astype(F32)

    def chunks(fw, bw):
        t = jnp.stack([fw, bw])
        return t.reshape(2, B, N, C, H, dh).transpose(2, 0, 1, 4, 3, 5)

    xs = (chunks(qf, jnp.flip(qf, axis=1)), chunks(k_fw, k_bw),
          chunks(vf, jnp.flip(vf, axis=1)), chunks(logf_fw, logf_bw))
    tril = jnp.tril(jnp.ones((C, C), dtype=bool))

    def chunk_step(state, inp):
        qc, kc, vc, lfc = inp
        b = jnp.cumsum(lfc, axis=-2)
        diff = b[..., :, None, :] - b[..., None, :, :]
        decay = jnp.exp(jnp.where(tril[:, :, None], diff, -jnp.inf))
        scores = jnp.einsum('gbhtk,gbhsk,gbhtsk->gbhts', qc, kc, decay)
        o = (jnp.einsum('gbhts,gbhsv->gbhtv', scores, vc)
             + jnp.einsum('gbhtk,gbhkv->gbhtv', qc * jnp.exp(b), state))
        b_last = b[..., -1:, :]
        state = (jnp.exp(b_last[..., 0, :])[..., :, None] * state
                 + jnp.einsum('gbhsk,gbhsv->gbhkv', kc * jnp.exp(b_last - b), vc))
        return state, o

    state0 = jnp.zeros((2, B, H, dh, dh), F32)
    _, o = lax.scan(chunk_step, state0, xs)
    o = o.transpose(1, 2, 0, 4, 3, 5).reshape(2, B, S, H, dh)
    return o[0] + jnp.flip(o[1], axis=1)


def hybrid_mixer(u, positions, w_in, lb_fwd, lb_bwd, attn_norm_g, hgrn_norm_g, w_out):
    B, S, _ = u.shape
    proj = jnp.einsum('bsd,dp->bsp', u, w_in)
    a, h = ATTN_WIDTH, HGRN_WIDTH
    split_at = [a, 2 * a, 3 * a, 3 * a + h, 3 * a + 2 * h, 3 * a + 3 * h, 3 * a + 4 * h]
    qa, ka, va, qh, ffw, fbw, vh, gh = jnp.split(proj, split_at, axis=-1)

    def attn_heads(t):
        return t.reshape(B, S, ATTN_HEADS, ATTN_HEAD_DIM).transpose(0, 2, 1, 3)

    def hgrn_heads(t):
        return t.reshape(B, S, HGRN_HEADS, HGRN_HEAD_DIM)

    attn = dilated_window_attention(rotary(attn_heads(qa), positions),
                                    rotary(attn_heads(ka), positions), attn_heads(va))
    attn = head_rms_norm(attn.transpose(0, 2, 1, 3), attn_norm_g).reshape(B, S, ATTN_WIDTH)
    rec = hgrn2_bidirectional(hgrn_heads(qh), hgrn_heads(ffw), hgrn_heads(fbw), hgrn_heads(vh),
                              lb_fwd, lb_bwd)
    rec = (head_rms_norm(rec, hgrn_norm_g)
           * jax.nn.silu(hgrn_heads(gh).astype(F32))).reshape(B, S, HGRN_WIDTH)
    mixed = jnp.concatenate([attn, rec], axis=-1).astype(u.dtype)
    return jnp.einsum('bsm,md->bsd', mixed, w_out)


def swiglu(t, wg, wu, wd):
    return (jax.nn.silu(t @ wg) * (t @ wu)) @ wd


def route(h, w_router, router_bias):
    T = h.shape[0]
    scores = jax.nn.sigmoid(h.astype(F32) @ w_router.astype(F32))
    sel = scores + router_bias.astype(F32)
    grouped = sel.reshape(T, N_EXPERT_GROUPS, N_EXPERTS // N_EXPERT_GROUPS)
    group_score = jnp.sum(lax.top_k(grouped, 2)[0], axis=-1)
    _, gidx = lax.top_k(group_score, TOPK_GROUPS)
    gmask = jnp.sum(jax.nn.one_hot(gidx, N_EXPERT_GROUPS, dtype=F32), axis=-2) > 0
    emask = jnp.repeat(gmask, N_EXPERTS // N_EXPERT_GROUPS, axis=-1)
    _, idx = lax.top_k(jnp.where(emask, sel, -jnp.inf), TOP_K)
    w = jnp.take_along_axis(scores, idx, axis=-1)
    w = w / jnp.sum(w, axis=-1, keepdims=True) * ROUTED_SCALE
    return idx, w


def routed_experts(h, idx, gate_w, w_gate, w_up, w_down):
    T, D = h.shape
    n_assign = T * TOP_K
    n_blocks = -(-(n_assign + N_EXPERTS * (MOE_BLOCK - 1)) // MOE_BLOCK)
    n_pad = n_blocks * MOE_BLOCK
    expert_id = idx.reshape(-1)
    token_id = jnp.repeat(jnp.arange(T, dtype=jnp.int32), TOP_K)
    weight = gate_w.reshape(-1)
    order = jnp.argsort(expert_id)
    e_sorted, t_sorted, w_sorted = expert_id[order], token_id[order], weight[order]
    counts = jnp.bincount(expert_id, length=N_EXPERTS)
    start = jnp.cumsum(counts) - counts
    padded = (counts + MOE_BLOCK - 1) // MOE_BLOCK * MOE_BLOCK
    padded_end = jnp.cumsum(padded)
    dest = padded_end[e_sorted] - padded[e_sorted] + jnp.arange(n_assign) - start[e_sorted]
    tok_buf = jnp.full((n_pad,), T, jnp.int32).at[dest].set(t_sorted)
    w_buf = jnp.zeros((n_pad,), F32).at[dest].set(w_sorted)
    block_expert = jnp.minimum(
        jnp.searchsorted(padded_end, jnp.arange(n_blocks) * MOE_BLOCK, side='right'), N_EXPERTS - 1)
    h_pad = jnp.concatenate([h, jnp.zeros((1, D), h.dtype)], axis=0)

    def block_step(acc, blk):
        tok, wt, e = blk
        y = swiglu(h_pad[tok], w_gate[e], w_up[e], w_down[e]).astype(F32) * wt[:, None]
        return acc.at[tok].add(y), None

    acc, _ = lax.scan(block_step, jnp.zeros((T + 1, D), F32),
                      (tok_buf.reshape(n_blocks, MOE_BLOCK), w_buf.reshape(n_blocks, MOE_BLOCK),
                       block_expert))
    return acc[:T]


def moe_ffn(h, w_router, router_bias, w_gate, w_up, w_down, ws_gate, ws_up, ws_down):
    idx, gate_w = route(h, w_router, router_bias)
    routed = routed_experts(h, idx, gate_w, w_gate, w_up, w_down)
    shared = swiglu(h, ws_gate, ws_up, ws_down).astype(F32)
    return (shared + routed).astype(h.dtype)


def setup_inputs(seed: int = 0) -> dict:
    key = jax.random.key(seed)
    ks = jax.random.split(key, 22)
    D = D_MODEL

    def normal(k, shape, scale):
        return jax.random.normal(k, shape, F32) * scale

    value_cols = jnp.concatenate([
        jnp.ones((2 * ATTN_WIDTH,), F32), jnp.full((ATTN_WIDTH,), BETA, F32),
        jnp.ones((3 * HGRN_WIDTH,), F32), jnp.full((HGRN_WIDTH,), BETA, F32),
        jnp.ones((HGRN_WIDTH,), F32)])
    return {
        'x': normal(ks[0], (BATCH, SEQ, D), 1.0),
        'c': normal(ks[1], (BATCH, D), 1.0),
        'positions': jnp.tile(jnp.arange(SEQ, dtype=jnp.int32)[None, :], (BATCH, 1)),
        'w_ada': normal(ks[2], (DEPTH, D, 6 * D), 0.5 * D ** -0.5),
        'b_ada': normal(ks[3], (DEPTH, 6 * D), 0.02),
        'w_in': normal(ks[4], (DEPTH, D, IN_PROJ_WIDTH), D ** -0.5) * value_cols,
        'lb_logits': normal(ks[5], (2, DEPTH + 1, HGRN_WIDTH), 0.5),
        'attn_norm_g': 1.0 + normal(ks[6], (DEPTH, ATTN_WIDTH), 0.02),
        'hgrn_norm_g': 1.0 + normal(ks[7], (DEPTH, HGRN_WIDTH), 0.02),
        'w_out': normal(ks[8], (DEPTH, MIX_WIDTH, D), BETA * MIX_WIDTH ** -0.5),
        'ln1_g': 1.0 + normal(ks[9], (DEPTH, D), 0.02),
        'ln1_b': normal(ks[10], (DEPTH, D), 0.02),
        'w_router': normal(ks[11], (DEPTH, D, N_EXPERTS), D ** -0.5),
        'router_bias': normal(ks[12], (DEPTH, N_EXPERTS), 0.01),
        'expert_w_gate': normal(ks[13], (DEPTH, N_EXPERTS, D, EXPERT_DIM), D ** -0.5),
        'expert_w_up': normal(ks[14], (DEPTH, N_EXPERTS, D, EXPERT_DIM), D ** -0.5),
        'expert_w_down': normal(ks[15], (DEPTH, N_EXPERTS, EXPERT_DIM, D), BETA * EXPERT_DIM ** -0.5),
        'shared_w_gate': normal(ks[16], (DEPTH, D, SHARED_DIM), D ** -0.5),
        'shared_w_up': normal(ks[17], (DEPTH, D, SHARED_DIM), D ** -0.5),
        'shared_w_down': normal(ks[18], (DEPTH, SHARED_DIM, D), BETA * SHARED_DIM ** -0.5),
        'ln2_g': 1.0 + normal(ks[19], (DEPTH, D), 0.02),
        'ln2_b': normal(ks[20], (DEPTH, D), 0.02),
    }


def reference(x, c, positions, w_ada, b_ada, w_in, lb_logits, attn_norm_g, hgrn_norm_g, w_out,
              ln1_g, ln1_b, w_router, router_bias, expert_w_gate, expert_w_up, expert_w_down,
              shared_w_gate, shared_w_up, shared_w_down, ln2_g, ln2_b):
    B, S, D = x.shape
    lower_bounds = jnp.cumsum(jax.nn.softmax(lb_logits.astype(F32), axis=1), axis=1)
    for layer in range(DEPTH):
        mod = jax.nn.silu(c) @ w_ada[layer] + b_ada[layer]
        shift1, scale1, gate1, shift2, scale2, gate2 = jnp.split(mod[:, None, :], 6, axis=-1)
        u = x * (1.0 + scale1) + shift1
        mix = hybrid_mixer(u, positions, w_in[layer], lower_bounds[0, layer], lower_bounds[1, layer],
                           attn_norm_g[layer], hgrn_norm_g[layer], w_out[layer])
        x = layer_norm(ALPHA * x + gate1 * mix, ln1_g[layer], ln1_b[layer])
        u = x * (1.0 + scale2) + shift2
        ffn = moe_ffn(u.reshape(B * S, D), w_router[layer], router_bias[layer],
                      expert_w_gate[layer], expert_w_up[layer], expert_w_down[layer],
                      shared_w_gate[layer], shared_w_up[layer], shared_w_down[layer]).reshape(B, S, D)
        x = layer_norm(ALPHA * x + gate2 * ffn, ln2_g[layer], ln2_b[layer])
    return x
```

```python
import functools

import jax
import jax.numpy as jnp
from jax import lax
from jax.experimental import pallas as pl
from jax.experimental.pallas import tpu as pltpu

F32 = jnp.float32
BF16 = jnp.bfloat16
I32 = jnp.int32
HIGHEST = lax.Precision.HIGHEST

D_MODEL = 2048
ATTN_HEADS = 16
ATTN_HEAD_DIM = 64
ATTN_WIDTH = ATTN_HEADS * ATTN_HEAD_DIM
HGRN_HEADS = 8
HGRN_HEAD_DIM = 128
HGRN_WIDTH = HGRN_HEADS * HGRN_HEAD_DIM
IN_PROJ_WIDTH = 3 * ATTN_WIDTH + 5 * HGRN_WIDTH
DILATIONS = (1, 4, 16)
ATTN_HALF = 64
ROPE_THETA = 10000.0
N_EXPERTS = 256
N_EXPERT_GROUPS = 8
GROUP_SIZE = N_EXPERTS // N_EXPERT_GROUPS
TOPK_GROUPS = 4
TOP_K = 8
EXPERT_DIM = 512
ROUTED_SCALE = 2.5
DEPTH = 1
ALPHA = (2 * DEPTH) ** 0.25
LN_EPS = 1e-5
RMS_EPS = 1e-6
NEG_INF = -1e30

LANES = 128
VMEM_LIMIT = 56 * 1024 * 1024

ADA_TN = 1024
ROPE_TM = 2048
INPROJ_TM = 512
INPROJ_TN = 1024
ATTN_TQ = 128
ATTN_WK = 256
HGRN_CHUNK = 64
HGRN_SUB = 16
MIX_TM = 256
ROUTE_TM = 256
MOE_BLOCK = 128
FINAL_TM = 128


def _cparams(sem):
    return pltpu.CompilerParams(dimension_semantics=sem, vmem_limit_bytes=VMEM_LIMIT)


def _ada_kernel(c_ref, w_ref, b_ref, o_ref):
    c = c_ref[...]
    sc = c * jax.nn.sigmoid(c)
    o_ref[...] = jnp.dot(sc, w_ref[...], preferred_element_type=F32, precision=HIGHEST) + b_ref[...]


def _ada_mod(c, w_ada, b_ada):
    B, D = c.shape
    N = w_ada.shape[1]
    c8 = jnp.zeros((8, D), F32).at[:B].set(c)
    out = pl.pallas_call(
        _ada_kernel,
        grid=(N // ADA_TN,),
        in_specs=[pl.BlockSpec((8, D), lambda j: (0, 0)),
                  pl.BlockSpec((D, ADA_TN), lambda j: (0, j)),
                  pl.BlockSpec((1, ADA_TN), lambda j: (0, j))],
        out_specs=pl.BlockSpec((8, ADA_TN), lambda j: (0, j)),
        out_shape=jax.ShapeDtypeStruct((8, N), F32),
        compiler_params=_cparams(("parallel",)),
        name="ada_mod",
    )(c8, w_ada, b_ada.reshape(1, N))
    return out[:B]


def _rope_kernel(pos_ref, invf_ref, sign_ref, cos_ref, sin_ref):
    ang = pos_ref[...].astype(F32) * invf_ref[...]
    cos_ref[...] = jnp.cos(ang)
    sin_ref[...] = jnp.sin(ang) * sign_ref[...]


def _rope_tables(positions):
    T = positions.size
    half = ATTN_HEAD_DIM // 2
    inv_freq = ROPE_THETA ** (-jnp.arange(half, dtype=F32) / half)
    lane = jnp.arange(LANES)
    invf = inv_freq[lane % half].reshape(1, LANES)
    sign = jnp.where((lane % ATTN_HEAD_DIM) < half, -1.0, 1.0).astype(F32).reshape(1, LANES)
    return pl.pallas_call(
        _rope_kernel,
        grid=(T // ROPE_TM,),
        in_specs=[pl.BlockSpec((ROPE_TM, 1), lambda i: (i, 0)),
                  pl.BlockSpec((1, LANES), lambda i: (0, 0)),
                  pl.BlockSpec((1, LANES), lambda i: (0, 0))],
        out_specs=[pl.BlockSpec((ROPE_TM, LANES), lambda i: (i, 0)),
                   pl.BlockSpec((ROPE_TM, LANES), lambda i: (i, 0))],
        out_shape=[jax.ShapeDtypeStruct((T, LANES), F32)] * 2,
        compiler_params=_cparams(("parallel",)),
        name="rope_tables",
    )(positions.reshape(T, 1), invf, sign)


def _inproj_kernel(x_ref, sc_ref, sh_ref, w_ref, cos_ref, sin_ref, o_ref):
    j = pl.program_id(1)
    u = (x_ref[...] * (1.0 + sc_ref[...]) + sh_ref[...]).astype(BF16)
    acc = jnp.dot(u, w_ref[...], preferred_element_type=F32)

    @pl.when(j < 2)
    def _():
        qscale = jnp.where(j == 0, ATTN_HEAD_DIM ** -0.5, 1.0).astype(F32)
        cos = cos_ref[...] * qscale
        sin = sin_ref[...] * qscale
        lane = lax.broadcasted_iota(I32, cos.shape, 1)
        first = (lane % ATTN_HEAD_DIM) < (ATTN_HEAD_DIM // 2)
        for cb in range(INPROJ_TN // LANES):
            a = acc[:, cb * LANES:(cb + 1) * LANES]
            partner = jnp.where(first, pltpu.roll(a, LANES - ATTN_HEAD_DIM // 2, 1),
                                pltpu.roll(a, ATTN_HEAD_DIM // 2, 1))
            o_ref[:, cb * LANES:(cb + 1) * LANES] = (a * cos + partner * sin).astype(BF16)

    @pl.when(j >= 2)
    def _():
        o_ref[...] = acc.astype(BF16)


def _in_proj(x2, scale1, shift1, w_in_bf, cos_t, sin_t, S):
    T, D = x2.shape
    P = w_in_bf.shape[1]
    per_b = S // INPROJ_TM
    return pl.pallas_call(
        _inproj_kernel,
        grid=(T // INPROJ_TM, P // INPROJ_TN),
        in_specs=[pl.BlockSpec((INPROJ_TM, D), lambda i, j: (i, 0)),
                  pl.BlockSpec((None, 1, D), lambda i, j: (i // per_b, 0, 0)),
                  pl.BlockSpec((None, 1, D), lambda i, j: (i // per_b, 0, 0)),
                  pl.BlockSpec((D, INPROJ_TN), lambda i, j: (0, j)),
                  pl.BlockSpec((INPROJ_TM, LANES), lambda i, j: (i, 0)),
                  pl.BlockSpec((INPROJ_TM, LANES), lambda i, j: (i, 0))],
        out_specs=pl.BlockSpec((INPROJ_TM, INPROJ_TN), lambda i, j: (i, j)),
        out_shape=jax.ShapeDtypeStruct((T, P), BF16),
        compiler_params=_cparams(("parallel", "arbitrary")),
        name="in_proj",
    )(x2, scale1, shift1, w_in_bf, cos_t, sin_t)


def _attn_kernel(q_ref, k_ref, v_ref, o_ref, lse_ref, *, L):
    nq = L // ATTN_TQ
    lane = lax.broadcasted_iota(I32, (1, LANES), 1)
    head0 = lane < ATTN_HEAD_DIM
    rel = (lax.broadcasted_iota(I32, (ATTN_TQ, ATTN_WK), 1)
           - lax.broadcasted_iota(I32, (ATTN_TQ, ATTN_WK), 0))

    def body(i, carry):
        q0 = pl.multiple_of(i * ATTN_TQ, ATTN_TQ)
        ks = pl.multiple_of(jnp.clip(i * ATTN_TQ - ATTN_HALF, 0, L - ATTN_WK), ATTN_HALF)
        q = q_ref[pl.ds(q0, ATTN_TQ), :]
        k = k_ref[pl.ds(ks, ATTN_WK), :]
        v = v_ref[pl.ds(ks, ATTN_WK), :]
        mask = jnp.abs(rel + (ks - q0)) <= ATTN_HALF
        outs, lses = [], []
        for hmask in (head0, jnp.logical_not(head0)):
            qh = jnp.where(hmask, q, jnp.zeros_like(q))
            s = lax.dot_general(qh, k, (((1,), (1,)), ((), ())), preferred_element_type=F32)
            s = jnp.where(mask, s, NEG_INF)
            m = jnp.max(s, axis=-1, keepdims=True)
            p = jnp.exp(s - m)
            l = jnp.sum(p, axis=-1, keepdims=True)
            outs.append(jnp.dot(p.astype(BF16), v, preferred_element_type=F32) / l)
            lses.append(m + jnp.log(l))
        o_ref[pl.ds(q0, ATTN_TQ), :] = jnp.where(head0, outs[0], outs[1]).astype(BF16)
        lse_ref[pl.ds(q0, ATTN_TQ), :] = jnp.where(head0, lses[0], lses[1])
        return carry

    lax.fori_loop(0, nq, body, 0)


def _attn_branch(proj, dilation, B, S):
    P = proj.shape[1]
    L = S // dilation
    pv = proj.reshape(B, L, dilation * P)
    pcb = P // LANES
    acb = ATTN_WIDTH // LANES
    in_spec = lambda off: pl.BlockSpec((None, L, LANES), lambda b, r, h: (b, 0, r * pcb + off + h))
    out_spec = pl.BlockSpec((None, L, LANES), lambda b, r, h: (b, 0, r * acb + h))
    o, lse = pl.pallas_call(
        functools.partial(_attn_kernel, L=L),
        grid=(B, dilation, acb),
        in_specs=[in_spec(0), in_spec(acb), in_spec(2 * acb)],
        out_specs=[out_spec, out_spec],
        out_shape=[jax.ShapeDtypeStruct((B, L, dilation * ATTN_WIDTH), BF16),
                   jax.ShapeDtypeStruct((B, L, dilation * ATTN_WIDTH), F32)],
        compiler_params=_cparams(("parallel", "parallel", "parallel")),
        name=f"attn_d{dilation}",
    )(pv, pv, pv)
    return o.reshape(B * S, ATTN_WIDTH), lse.reshape(B * S, ATTN_WIDTH)


def _hgrn_chunk(q, kk, lf, v, state_t, tri, reverse):
    C = HGRN_CHUNK
    SUB = HGRN_SUB
    nsub = C // SUB
    b = jnp.dot(tri, lf, preferred_element_type=F32, precision=HIGHEST)
    col = lax.broadcasted_iota(I32, (SUB, C), 1)
    row = lax.broadcasted_iota(I32, (SUB, C), 0)
    kk_bf_rows = []
    score_rows = []
    for i in range(nsub):
        r0 = i * SUB
        bi = b[r0:r0 + SUB]
        qi = q[r0:r0 + SUB]
        ki = kk[r0:r0 + SUB]
        diag = jnp.zeros((SUB, C), F32)
        for s in range(SUB):
            e = jnp.exp(jnp.minimum(bi - bi[s:s + 1], 0.0))
            colv = jnp.sum(qi * e * ki[s:s + 1], axis=-1, keepdims=True)
            diag = jnp.where(col == r0 + s, colv, diag)
        if reverse:
            keep = (col - r0) >= row
        else:
            keep = (col - r0) <= row
        diag = jnp.where(jnp.logical_and(keep, jnp.logical_and(col >= r0, col < r0 + SUB)), diag, 0.0)
        if reverse:
            has_off = i < nsub - 1
            bref = b[r0 + SUB:r0 + SUB + 1] if has_off else None
            off_mask = col >= r0 + SUB
        else:
            has_off = i > 0
            bref = b[r0 - 1:r0] if has_off else None
            off_mask = col < r0
        if has_off:
            qs = (qi * jnp.exp(bi - bref)).astype(BF16)
            ks = (kk * jnp.exp(jnp.minimum(bref - b, 0.0))).astype(BF16)
            off = lax.dot_general(qs, ks, (((1,), (1,)), ((), ())), preferred_element_type=F32)
            score_rows.append(jnp.where(off_mask, off, diag))
        else:
            score_rows.append(diag)
    scores = jnp.concatenate(score_rows, axis=0)
    b_edge = b[0:1] if reverse else b[C - 1:C]
    o = jnp.dot(scores.astype(BF16), v.astype(BF16), preferred_element_type=F32)
    qd = (q * jnp.exp(b)).astype(BF16)
    o = o + lax.dot_general(qd, state_t.astype(BF16), (((1,), (1,)), ((), ())),
                            preferred_element_type=F32)
    kd = (kk * jnp.exp(b_edge - b)).astype(BF16)
    upd = lax.dot_general(v.astype(BF16), kd, (((0,), (0,)), ((), ())), preferred_element_type=F32)
    state_t = state_t * jnp.exp(b_edge) + upd
    return o, state_t


def _hgrn_kernel(q_ref, zf_ref, zb_ref, v_ref, g_ref, lbf_ref, lbb_ref, ng_ref, o_ref, acc_ref, *, S):
    C = HGRN_CHUNK
    n_chunks = S // C
    r = lax.broadcasted_iota(I32, (C, C), 0)
    c = lax.broadcasted_iota(I32, (C, C), 1)
    tri_f = (c <= r).astype(F32)
    tri_b = (c >= r).astype(F32)
    lbf = lbf_ref[...]
    lbb = lbb_ref[...]
    acc_ref[...] = jnp.zeros_like(acc_ref)

    def gates(z, lb):
        sg = jax.nn.sigmoid(z)
        return jnp.log(lb + (1.0 - lb) * sg), (1.0 - lb) * jax.nn.sigmoid(-z)

    def body(n, carry):
        st_f, st_b = carry
        rf = pl.multiple_of(n * C, C)
        rb = pl.multiple_of((n_chunks - 1 - n) * C, C)
        lf_f, kk_f = gates(zf_ref[pl.ds(rf, C), :].astype(F32), lbf)
        o_f, st_f = _hgrn_chunk(q_ref[pl.ds(rf, C), :].astype(F32), kk_f, lf_f,
                                v_ref[pl.ds(rf, C), :].astype(F32), st_f, tri_f, False)
        acc_ref[pl.ds(rf, C), :] += o_f
        lf_b, kk_b = gates(zb_ref[pl.ds(rb, C), :].astype(F32), lbb)
        o_b, st_b = _hgrn_chunk(q_ref[pl.ds(rb, C), :].astype(F32), kk_b, lf_b,
                                v_ref[pl.ds(rb, C), :].astype(F32), st_b, tri_b, True)
        acc_ref[pl.ds(rb, C), :] += o_b
        return st_f, st_b

    z0 = jnp.zeros((HGRN_HEAD_DIM, HGRN_HEAD_DIM), F32)
    lax.fori_loop(0, n_chunks, body, (z0, z0))

    ng = ng_ref[...]

    def norm_body(n, carry):
        r0 = pl.multiple_of(n * 512, 512)
        t = acc_ref[pl.ds(r0, 512), :]
        g = g_ref[pl.ds(r0, 512), :].astype(F32)
        y = t * lax.rsqrt(jnp.mean(t * t, axis=-1, keepdims=True) + RMS_EPS) * ng
        o_ref[pl.ds(r0, 512), :] = (y * (g * jax.nn.sigmoid(g))).astype(BF16)
        return carry

    lax.fori_loop(0, S // 512, norm_body, 0)


def _hgrn(proj, lb_fwd, lb_bwd, norm_g, B, S):
    P = proj.shape[1]
    pv = proj.reshape(B, S, P)
    base = 3 * ATTN_WIDTH // LANES
    nh = HGRN_HEADS
    in_spec = lambda k: pl.BlockSpec((None, S, LANES), lambda b, h: (b, 0, base + k * nh + h))
    vec_spec = pl.BlockSpec((None, 1, LANES), lambda b, h: (h, 0, 0))
    out = pl.pallas_call(
        functools.partial(_hgrn_kernel, S=S),
        grid=(B, nh),
        in_specs=[in_spec(0), in_spec(1), in_spec(2), in_spec(3), in_spec(4), vec_spec, vec_spec, vec_spec],
        out_specs=pl.BlockSpec((None, S, LANES), lambda b, h: (b, 0, h)),
        out_shape=jax.ShapeDtypeStruct((B, S, HGRN_WIDTH), BF16),
        scratch_shapes=[pltpu.VMEM((S, LANES), F32)],
        compiler_params=_cparams(("parallel", "parallel")),
        name="hgrn2",
    )(pv, pv, pv, pv, pv, lb_fwd.reshape(nh, 1, LANES), lb_bwd.reshape(nh, 1, LANES),
      norm_g.reshape(nh, 1, LANES))
    return out.reshape(B * S, HGRN_WIDTH)


def _layer_norm_rows(y, g, b):
    mu = jnp.mean(y, axis=-1, keepdims=True)
    d = y - mu
    var = jnp.mean(d * d, axis=-1, keepdims=True)
    return d * lax.rsqrt(var + LN_EPS) * g + b


def _mix_kernel(o1_ref, o2_ref, o3_ref, l1_ref, l2_ref, l3_ref, rec_ref, x_ref, grp_ref, ag_ref, w_ref,
                gate_ref, lng_ref, lnb_ref, sc_ref, sh_ref, x1_ref, u2_ref):
    l1, l2, l3 = l1_ref[...], l2_ref[...], l3_ref[...]
    m = jnp.maximum(jnp.maximum(l1, l2), l3)
    e1, e2, e3 = jnp.exp(l1 - m), jnp.exp(l2 - m), jnp.exp(l3 - m)
    attn = (e1 * o1_ref[...].astype(F32) + e2 * o2_ref[...].astype(F32)
            + e3 * o3_ref[...].astype(F32)) / (e1 + e2 + e3)
    ms = jnp.dot((attn * attn).astype(BF16), grp_ref[...], preferred_element_type=F32)
    normed = attn * lax.rsqrt(ms + RMS_EPS) * ag_ref[...]
    mixed = jnp.concatenate([normed.astype(BF16), rec_ref[...]], axis=-1)
    mix = jnp.dot(mixed, w_ref[...], preferred_element_type=F32)
    x1 = _layer_norm_rows(ALPHA * x_ref[...] + gate_ref[...] * mix, lng_ref[...], lnb_ref[...])
    x1_ref[...] = x1
    u2_ref[...] = x1 * (1.0 + sc_ref[...]) + sh_ref[...]


def _mix(o_branches, lse_branches, rec, x2, attn_norm_g, w_out_bf, gate1, ln_g, ln_b, scale2, shift2, S):
    T, D = x2.shape
    per_b = S // MIX_TM
    head = jnp.arange(ATTN_WIDTH) // ATTN_HEAD_DIM
    grp = jnp.where(head[:, None] == head[None, :], 1.0 / ATTN_HEAD_DIM, 0.0).astype(BF16)
    row = lambda w: pl.BlockSpec((MIX_TM, w), lambda i: (i, 0))
    const = lambda shape: pl.BlockSpec(shape, lambda i: (0,) * len(shape))
    per_batch = pl.BlockSpec((None, 1, D), lambda i: (i // per_b, 0, 0))
    return pl.pallas_call(
        _mix_kernel,
        grid=(T // MIX_TM,),
        in_specs=[row(ATTN_WIDTH)] * 6 + [row(HGRN_WIDTH), row(D), const((ATTN_WIDTH, ATTN_WIDTH)),
                  const((1, ATTN_WIDTH)), const((D, D)), per_batch, const((1, D)), const((1, D)),
                  per_batch, per_batch],
        out_specs=[row(D), row(D)],
        out_shape=[jax.ShapeDtypeStruct((T, D), F32)] * 2,
        compiler_params=_cparams(("parallel",)),
        name="mix_out_ln1",
    )(*o_branches, *lse_branches, rec, x2, grp, attn_norm_g.reshape(1, -1), w_out_bf, gate1,
      ln_g.reshape(1, D), ln_b.reshape(1, D), scale2, shift2)


def _route_kernel(u_ref, w_ref, bias_ref, idx_ref, gw_ref, rank_ref, cnt_ref, run_ref):
    i = pl.program_id(0)

    @pl.when(i == 0)
    def _():
        run_ref[...] = jnp.zeros_like(run_ref)

    tm = ROUTE_TM
    logits = jnp.dot(u_ref[...], w_ref[...], preferred_element_type=F32, precision=HIGHEST)
    scores = jax.nn.sigmoid(logits)
    sel = scores + bias_ref[...]
    lane = lax.broadcasted_iota(I32, (tm, N_EXPERTS), 1)
    lane_f = lane.astype(F32)
    group = lane // GROUP_SIZE
    neg = jnp.float32(-jnp.inf)

    def first_argmax(vals):
        mx = jnp.max(vals, axis=-1, keepdims=True)
        idx = jnp.min(jnp.where(vals == mx, lane_f, float(N_EXPERTS)), axis=-1, keepdims=True)
        return mx, idx

    gscore = []
    for g in range(N_EXPERT_GROUPS):
        vals = jnp.where(group == g, sel, neg)
        m1, i1 = first_argmax(vals)
        m2 = jnp.max(jnp.where(lane_f == i1, neg, vals), axis=-1, keepdims=True)
        gscore.append(m1 + m2)
    keep_f = jnp.zeros((tm, N_EXPERTS), F32)
    for g in range(N_EXPERT_GROUPS):
        beaten = jnp.zeros((tm, 1), F32)
        for h in range(N_EXPERT_GROUPS):
            if h == g:
                continue
            ahead = (gscore[h] >= gscore[g]) if h < g else (gscore[h] > gscore[g])
            beaten = beaten + jnp.where(ahead, 1.0, 0.0)
        gkeep = jnp.where(beaten < TOPK_GROUPS, 1.0, 0.0)
        keep_f = jnp.where(group == g, gkeep, keep_f)
    vals = jnp.where(keep_f > 0.5, sel, neg)

    lane_o = lax.broadcasted_iota(I32, (tm, LANES), 1)
    idx_out = jnp.zeros((tm, LANES), F32)
    gw_out = jnp.zeros((tm, LANES), F32)
    chosen = jnp.zeros((tm, N_EXPERTS), F32)
    picks = []
    wsum = jnp.zeros((tm, 1), F32)
    for k in range(TOP_K):
        _, ik = first_argmax(vals)
        hit = lane_f == ik
        sk = jnp.sum(jnp.where(hit, scores, 0.0), axis=-1, keepdims=True)
        vals = jnp.where(hit, neg, vals)
        chosen = jnp.where(hit, 1.0, chosen)
        picks.append((ik, sk))
        wsum = wsum + sk
        idx_out = jnp.where(lane_o == k, ik, idx_out)
    for k, (ik, sk) in enumerate(picks):
        gw_out = jnp.where(lane_o == k, sk / wsum * ROUTED_SCALE, gw_out)

    r = lax.broadcasted_iota(I32, (tm, tm), 0)
    c = lax.broadcasted_iota(I32, (tm, tm), 1)
    strict_lower = jnp.where(c < r, 1.0, 0.0).astype(BF16)
    before = jnp.dot(strict_lower, chosen.astype(BF16), preferred_element_type=F32) + run_ref[...]
    rank_out = jnp.zeros((tm, LANES), F32)
    for k, (ik, sk) in enumerate(picks):
        rk = jnp.sum(jnp.where(lane_f == ik, before, 0.0), axis=-1, keepdims=True)
        rank_out = jnp.where(lane_o == k, rk, rank_out)
    run_ref[...] = run_ref[...] + jnp.sum(chosen, axis=0, keepdims=True)

    idx_ref[...] = idx_out.astype(I32)
    gw_ref[...] = gw_out
    rank_ref[...] = rank_out.astype(I32)
    cnt_ref[...] = run_ref[...]


def _route(u2, w_router, router_bias):
    T, D = u2.shape
    row = lambda w: pl.BlockSpec((ROUTE_TM, w), lambda i: (i, 0))
    idx, gw, rank, cnt = pl.pallas_call(
        _route_kernel,
        grid=(T // ROUTE_TM,),
        in_specs=[row(D), pl.BlockSpec((D, N_EXPERTS), lambda i: (0, 0)),
                  pl.BlockSpec((1, N_EXPERTS), lambda i: (0, 0))],
        out_specs=[row(LANES), row(LANES), row(LANES), pl.BlockSpec((1, N_EXPERTS), lambda i: (0, 0))],
        out_shape=[jax.ShapeDtypeStruct((T, LANES), I32), jax.ShapeDtypeStruct((T, LANES), F32),
                   jax.ShapeDtypeStruct((T, LANES), I32), jax.ShapeDtypeStruct((1, N_EXPERTS), F32)],
        scratch_shapes=[pltpu.VMEM((1, N_EXPERTS), F32)],
        compiler_params=_cparams(("arbitrary",)),
        name="router",
    )(u2, w_router, router_bias.reshape(1, N_EXPERTS))
    return idx[:, :TOP_K], gw[:, :TOP_K], rank[:, :TOP_K], cnt[0].astype(I32)


def _moe_kernel(bexp_ref, nused_ref, meta_hbm, u_hbm, wg_ref, wu_ref, wd_ref, y_hbm,
                meta_smem, xbuf, ybuf, wg_bf, wu_bf, wd_bf, sem_meta, sem_g, sem_s):
    i = pl.program_id(0)
    nblk = pl.num_programs(0)
    nused = nused_ref[0]
    slot = i % 2

    def meta_copy(blk, sl):
        return pltpu.make_async_copy(meta_hbm.at[blk], meta_smem.at[sl], sem_meta.at[sl])

    @pl.when((i == 0) & (nused > 0))
    def _():
        meta_copy(0, 0).start()

    @pl.when(i >= nused)
    def _():
        ybuf[...] = jnp.zeros_like(ybuf)
        fill = pltpu.make_async_copy(ybuf, y_hbm.at[pl.ds(i * MOE_BLOCK, MOE_BLOCK)], sem_s)
        fill.start()
        fill.wait()

    @pl.when(i < nused)
    def _():
        meta_copy(i, slot).wait()

        @pl.when(i + 1 < jnp.minimum(nused, nblk))
        def _():
            meta_copy(i + 1, 1 - slot).start()

        def g_start(j, c):
            tok = meta_smem[slot, j]
            pltpu.make_async_copy(u_hbm.at[pl.ds(tok, 1)], xbuf.at[pl.ds(j, 1)], sem_g).start()
            return c

        lax.fori_loop(0, MOE_BLOCK, g_start, 0)

        prev = bexp_ref[jnp.maximum(i - 1, 0)]

        @pl.when((i == 0) | (bexp_ref[i] != prev))
        def _():
            wg_bf[...] = wg_ref[...].astype(BF16)
            wu_bf[...] = wu_ref[...].astype(BF16)
            wd_bf[...] = wd_ref[...].astype(BF16)

        pltpu.make_async_copy(u_hbm.at[pl.ds(0, MOE_BLOCK)], xbuf, sem_g).wait()

        x = xbuf[...].astype(BF16)
        hg = jnp.dot(x, wg_bf[...], preferred_element_type=F32)
        hu = jnp.dot(x, wu_bf[...], preferred_element_type=F32)
        act = (hg * jax.nn.sigmoid(hg) * hu).astype(BF16)
        ybuf[...] = jnp.dot(act, wd_bf[...], preferred_element_type=F32)

        def s_start(j, c):
            dst = meta_smem[slot, MOE_BLOCK + j]
            pltpu.make_async_copy(ybuf.at[pl.ds(j, 1)], y_hbm.at[pl.ds(dst, 1)], sem_s).start()
            return c

        lax.fori_loop(0, MOE_BLOCK, s_start, 0)
        pltpu.make_async_copy(ybuf, y_hbm.at[pl.ds(0, MOE_BLOCK)], sem_s).wait()


def _moe(u2, bexp, nused, meta, w_gate, w_up, w_down, n_pad):
    T, D = u2.shape
    n_blocks = n_pad // MOE_BLOCK
    E = EXPERT_DIM
    grid_spec = pltpu.PrefetchScalarGridSpec(
        num_scalar_prefetch=2,
        grid=(n_blocks,),
        in_specs=[pl.BlockSpec(memory_space=pl.ANY),
                  pl.BlockSpec(memory_space=pl.ANY),
                  pl.BlockSpec((None, D, E), lambda i, be, nu: (be[i], 0, 0)),
                  pl.BlockSpec((None, D, E), lambda i, be, nu: (be[i], 0, 0)),
                  pl.BlockSpec((None, E, D), lambda i, be, nu: (be[i], 0, 0))],
        out_specs=pl.BlockSpec(memory_space=pl.ANY),
        scratch_shapes=[pltpu.SMEM((2, 2 * MOE_BLOCK), I32),
                        pltpu.VMEM((MOE_BLOCK, D), F32),
                        pltpu.VMEM((MOE_BLOCK, D), F32),
                        pltpu.VMEM((D, E), BF16),
                        pltpu.VMEM((D, E), BF16),
                        pltpu.VMEM((E, D), BF16),
                        pltpu.SemaphoreType.DMA((2,)),
                        pltpu.SemaphoreType.DMA(()),
                        pltpu.SemaphoreType.DMA(())])
    return pl.pallas_call(
        _moe_kernel,
        grid_spec=grid_spec,
        out_shape=jax.ShapeDtypeStruct((n_pad, D), F32),
        compiler_params=_cparams(("arbitrary",)),
        name="moe_experts",
    )(bexp, nused, meta, u2, w_gate, w_up, w_down)


def _dispatch_plan(idx, rank, counts, T):
    n_assign = T * TOP_K
    n_blocks = -(-(n_assign + N_EXPERTS * (MOE_BLOCK - 1)) // MOE_BLOCK)
    n_pad = n_blocks * MOE_BLOCK
    padded = (counts + MOE_BLOCK - 1) // MOE_BLOCK * MOE_BLOCK
    padded_end = jnp.cumsum(padded)
    start = padded_end - padded
    dest = start[idx] + rank
    src = jnp.full((n_pad,), -1, I32).at[dest.reshape(-1)].set(jnp.arange(n_assign, dtype=I32))
    is_pad = src < 0
    pad_rank = jnp.cumsum(is_pad.astype(I32)) - 1
    tok = jnp.where(is_pad, 0, src // TOP_K)
    dst = jnp.where(is_pad, n_assign + pad_rank, src)
    meta = jnp.concatenate([tok.reshape(n_blocks, MOE_BLOCK), dst.reshape(n_blocks, MOE_BLOCK)], axis=1)
    bexp = jnp.minimum(jnp.searchsorted(padded_end, jnp.arange(n_blocks, dtype=I32) * MOE_BLOCK,
                                        side='right'), N_EXPERTS - 1).astype(I32)
    nused = (padded_end[-1] // MOE_BLOCK).astype(I32).reshape(1)
    return bexp, nused, meta.astype(I32), n_pad


def _final_kernel(u_ref, y_ref, gw_ref, x1_ref, wg_ref, wu_ref, wd_ref, gate_ref, lng_ref, lnb_ref, o_ref):
    u = u_ref[...].astype(BF16)
    hg = jnp.dot(u, wg_ref[...], preferred_element_type=F32)
    hu = jnp.dot(u, wu_ref[...], preferred_element_type=F32)
    act = (hg * jax.nn.sigmoid(hg) * hu).astype(BF16)
    ffn = jnp.dot(act, wd_ref[...], preferred_element_type=F32)
    gw = gw_ref[...]
    D = u_ref.shape[1]
    for k in range(TOP_K):
        ffn = ffn + y_ref[:, k * D:(k + 1) * D] * gw[:, k:k + 1]
    o_ref[...] = _layer_norm_rows(ALPHA * x1_ref[...] + gate_ref[...] * ffn, lng_ref[...], lnb_ref[...])


def _final(u2, y_flat, gw, x1, ws_gate_bf, ws_up_bf, ws_down_bf, gate2, ln_g, ln_b, S):
    T, D = u2.shape
    E = ws_gate_bf.shape[1]
    per_b = S // FINAL_TM
    yv = y_flat.reshape(-1, TOP_K * D)
    row = lambda w: pl.BlockSpec((FINAL_TM, w), lambda i: (i, 0))
    const = lambda shape: pl.BlockSpec(shape, lambda i: (0,) * len(shape))
    return pl.pallas_call(
        _final_kernel,
        grid=(T // FINAL_TM,),
        in_specs=[row(D), row(TOP_K * D), row(TOP_K), row(D), const((D, E)), const((D, E)), const((E, D)),
                  pl.BlockSpec((None, 1, D), lambda i: (i // per_b, 0, 0)), const((1, D)), const((1, D))],
        out_specs=row(D),
        out_shape=jax.ShapeDtypeStruct((T, D), F32),
        compiler_params=_cparams(("parallel",)),
        name="shared_combine_ln2",
    )(u2, yv, gw, x1, ws_gate_bf, ws_up_bf, ws_down_bf, gate2, ln_g.reshape(1, D), ln_b.reshape(1, D))


def kernel(x, c, positions, w_ada, b_ada, w_in, lb_logits, attn_norm_g, hgrn_norm_g, w_out, ln1_g, ln1_b,
           w_router, router_bias, expert_w_gate, expert_w_up, expert_w_down, shared_w_gate, shared_w_up,
           shared_w_down, ln2_g, ln2_b):
    B, S, D = x.shape
    T = B * S
    layer = 0
    lower_bounds = jnp.cumsum(jax.nn.softmax(lb_logits.astype(F32), axis=1), axis=1)

    mod = _ada_mod(c, w_ada[layer], b_ada[layer])
    shift1, scale1, gate1, shift2, scale2, gate2 = [m.reshape(B, 1, D) for m in jnp.split(mod, 6, axis=-1)]
    cos_t, sin_t = _rope_tables(positions)
    x2 = x.reshape(T, D)

    proj = _in_proj(x2, scale1, shift1, w_in[layer].astype(BF16), cos_t, sin_t, S)
    branches = [_attn_branch(proj, d, B, S) for d in DILATIONS]
    rec = _hgrn(proj, lower_bounds[0, layer], lower_bounds[1, layer], hgrn_norm_g[layer], B, S)
    x1, u2 = _mix([o for o, _ in branches], [l for _, l in branches], rec, x2, attn_norm_g[layer],
                  w_out[layer].astype(BF16), gate1, ln1_g[layer], ln1_b[layer], scale2, shift2, S)

    idx, gw, rank, counts = _route(u2, w_router[layer], router_bias[layer])
    bexp, nused, meta, n_pad = _dispatch_plan(idx, rank, counts, T)
    y_flat = _moe(u2, bexp, nused, meta, expert_w_gate[layer], expert_w_up[layer], expert_w_down[layer], n_pad)
    out = _final(u2, y_flat, gw, x1, shared_w_gate[layer].astype(BF16), shared_w_up[layer].astype(BF16),
                 shared_w_down[layer].astype(BF16), gate2, ln2_g[layer], ln2_b[layer], S)
    return out.reshape(B, S, D)
```

```python
import functools

import jax
import jax.numpy as jnp
from jax import lax
from jax.experimental import pallas as pl
from jax.experimental.pallas import tpu as pltpu

F32 = jnp.float32
BF16 = jnp.bfloat16
I32 = jnp.int32
HIGHEST = lax.Precision.HIGHEST

D_MODEL = 2048
ATTN_HEADS = 16
ATTN_HEAD_DIM = 64
ATTN_WIDTH = ATTN_HEADS * ATTN_HEAD_DIM
HGRN_HEADS = 8
HGRN_HEAD_DIM = 128
HGRN_WIDTH = HGRN_HEADS * HGRN_HEAD_DIM
IN_PROJ_WIDTH = 3 * ATTN_WIDTH + 5 * HGRN_WIDTH
DILATIONS = (1, 4, 16)
ATTN_HALF = 64
ROPE_THETA = 10000.0
N_EXPERTS = 256
N_EXPERT_GROUPS = 8
GROUP_SIZE = N_EXPERTS // N_EXPERT_GROUPS
TOPK_GROUPS = 4
TOP_K = 8
EXPERT_DIM = 512
ROUTED_SCALE = 2.5
DEPTH = 1
ALPHA = (2 * DEPTH) ** 0.25
LN_EPS = 1e-5
RMS_EPS = 1e-6
NEG_INF = -1e30

LANES = 128
VMEM_LIMIT = 56 * 1024 * 1024

ADA_TN = 1024
ROPE_TM = 2048
INPROJ_TM = 512
INPROJ_TN = 1024
ATTN_TQ = 128
ATTN_WK = 256
HGRN_CHUNK = 64
HGRN_SUB = 16
MIX_TM = 256
ROUTE_TM = 256
MOE_BLOCK = 128
FINAL_TM = 128


def _cparams(sem):
    return pltpu.CompilerParams(dimension_semantics=sem, vmem_limit_bytes=VMEM_LIMIT)


def _ada_kernel(c_ref, w_ref, b_ref, o_ref):
    c = c_ref[...]
    sc = c * jax.nn.sigmoid(c)
    o_ref[...] = jnp.dot(sc, w_ref[...], preferred_element_type=F32, precision=HIGHEST) + b_ref[...]


def _ada_mod(c, w_ada, b_ada):
    B, D = c.shape
    N = w_ada.shape[1]
    c8 = jnp.zeros((8, D), F32).at[:B].set(c)
    out = pl.pallas_call(
        _ada_kernel,
        grid=(N // ADA_TN,),
        in_specs=[pl.BlockSpec((8, D), lambda j: (0, 0)),
                  pl.BlockSpec((D, ADA_TN), lambda j: (0, j)),
                  pl.BlockSpec((1, ADA_TN), lambda j: (0, j))],
        out_specs=pl.BlockSpec((8, ADA_TN), lambda j: (0, j)),
        out_shape=jax.ShapeDtypeStruct((8, N), F32),
        compiler_params=_cparams(("parallel",)),
        name="ada_mod",
    )(c8, w_ada, b_ada.reshape(1, N))
    return out[:B]


def _rope_kernel(pos_ref, invf_ref, sign_ref, cos_ref, sin_ref):
    ang = pos_ref[...].astype(F32) * invf_ref[...]
    cos_ref[...] = jnp.cos(ang)
    sin_ref[...] = jnp.sin(ang) * sign_ref[...]


def _rope_tables(positions):
    T = positions.size
    half = ATTN_HEAD_DIM // 2
    inv_freq = ROPE_THETA ** (-jnp.arange(half, dtype=F32) / half)
    lane = jnp.arange(LANES)
    invf = inv_freq[lane % half].reshape(1, LANES)
    sign = jnp.where((lane % ATTN_HEAD_DIM) < half, -1.0, 1.0).astype(F32).reshape(1, LANES)
    return pl.pallas_call(
        _rope_kernel,
        grid=(T // ROPE_TM,),
        in_specs=[pl.BlockSpec((ROPE_TM, 1), lambda i: (i, 0)),
                  pl.BlockSpec((1, LANES), lambda i: (0, 0)),
                  pl.BlockSpec((1, LANES), lambda i: (0, 0))],
        out_specs=[pl.BlockSpec((ROPE_TM, LANES), lambda i: (i, 0)),
                   pl.BlockSpec((ROPE_TM, LANES), lambda i: (i, 0))],
        out_shape=[jax.ShapeDtypeStruct((T, LANES), F32)] * 2,
        compiler_params=_cparams(("parallel",)),
        name="rope_tables",
    )(positions.reshape(T, 1), invf, sign)


def _inproj_kernel(x_ref, sc_ref, sh_ref, w_ref, cos_ref, sin_ref, o_ref, o4_ref, o16_ref, stage_ref):
    j = pl.program_id(1)
    u = (x_ref[...] * (1.0 + sc_ref[...]) + sh_ref[...]).astype(BF16)
    acc = jnp.dot(u, w_ref[...], preferred_element_type=F32)

    @pl.when(j < 2)
    def _():
        qscale = jnp.where(j == 0, ATTN_HEAD_DIM ** -0.5, 1.0).astype(F32)
        cos = cos_ref[...] * qscale
        sin = sin_ref[...] * qscale
        lane = lax.broadcasted_iota(I32, cos.shape, 1)
        first = (lane % ATTN_HEAD_DIM) < (ATTN_HEAD_DIM // 2)
        for cb in range(INPROJ_TN // LANES):
            a = acc[:, cb * LANES:(cb + 1) * LANES]
            partner = jnp.where(first, pltpu.roll(a, LANES - ATTN_HEAD_DIM // 2, 1),
                                pltpu.roll(a, ATTN_HEAD_DIM // 2, 1))
            stage_ref[cb] = a * cos + partner * sin

    @pl.when(j == 2)
    def _():
        for cb in range(INPROJ_TN // LANES):
            stage_ref[cb] = acc[:, cb * LANES:(cb + 1) * LANES]

    @pl.when(j < 3)
    def _():
        for cb in range(INPROJ_TN // LANES):
            o_ref[:, cb * LANES:(cb + 1) * LANES] = stage_ref[cb].astype(BF16)
            for d, od_ref in ((DILATIONS[1], o4_ref), (DILATIONS[2], o16_ref)):
                rows = INPROJ_TM // d
                for r in range(d):
                    c0 = r * INPROJ_TN + cb * LANES
                    od_ref[:, c0:c0 + LANES] = stage_ref[cb, pl.ds(r, rows, stride=d), :].astype(BF16)

    @pl.when(j >= 3)
    def _():
        o_ref[...] = acc.astype(BF16)


def _in_proj(x2, scale1, shift1, w_in_bf, cos_t, sin_t, B, S):
    T, D = x2.shape
    P = w_in_bf.shape[1]
    per_b = S // INPROJ_TM
    assert INPROJ_TN == ATTN_WIDTH
    d4, d16 = DILATIONS[1], DILATIONS[2]
    strided_spec = lambda d: pl.BlockSpec((None, INPROJ_TM // d, d * INPROJ_TN),
                                          lambda i, j: (i // per_b, i % per_b, jnp.minimum(j, 2)))
    strided_shape = lambda d: jax.ShapeDtypeStruct((B, S // d, 3 * d * ATTN_WIDTH), BF16)
    return pl.pallas_call(
        _inproj_kernel,
        grid=(T // INPROJ_TM, P // INPROJ_TN),
        in_specs=[pl.BlockSpec((INPROJ_TM, D), lambda i, j: (i, 0)),
                  pl.BlockSpec((None, 1, D), lambda i, j: (i // per_b, 0, 0)),
                  pl.BlockSpec((None, 1, D), lambda i, j: (i // per_b, 0, 0)),
                  pl.BlockSpec((D, INPROJ_TN), lambda i, j: (0, j)),
                  pl.BlockSpec((INPROJ_TM, LANES), lambda i, j: (i, 0)),
                  pl.BlockSpec((INPROJ_TM, LANES), lambda i, j: (i, 0))],
        out_specs=[pl.BlockSpec((INPROJ_TM, INPROJ_TN), lambda i, j: (i, j)), strided_spec(d4), strided_spec(d16)],
        out_shape=[jax.ShapeDtypeStruct((T, P), BF16), strided_shape(d4), strided_shape(d16)],
        scratch_shapes=[pltpu.VMEM((INPROJ_TN // LANES, INPROJ_TM, LANES), F32)],
        compiler_params=_cparams(("parallel", "arbitrary")),
        name="in_proj",
    )(x2, scale1, shift1, w_in_bf, cos_t, sin_t)


def _attn_kernel(q_ref, k_ref, v_ref, o_ref, lse_ref, *, L):
    nq = L // ATTN_TQ
    lane = lax.broadcasted_iota(I32, (1, LANES), 1)
    head0 = lane < ATTN_HEAD_DIM
    rel = (lax.broadcasted_iota(I32, (ATTN_TQ, ATTN_WK), 1)
           - lax.broadcasted_iota(I32, (ATTN_TQ, ATTN_WK), 0))

    def body(i, carry):
        q0 = pl.multiple_of(i * ATTN_TQ, ATTN_TQ)
        ks = pl.multiple_of(jnp.clip(i * ATTN_TQ - ATTN_HALF, 0, L - ATTN_WK), ATTN_HALF)
        q = q_ref[pl.ds(q0, ATTN_TQ), :]
        k = k_ref[pl.ds(ks, ATTN_WK), :]
        v = v_ref[pl.ds(ks, ATTN_WK), :]
        mask = jnp.abs(rel + (ks - q0)) <= ATTN_HALF
        outs, lses = [], []
        for hmask in (head0, jnp.logical_not(head0)):
            qh = jnp.where(hmask, q, jnp.zeros_like(q))
            s = lax.dot_general(qh, k, (((1,), (1,)), ((), ())), preferred_element_type=F32)
            s = jnp.where(mask, s, NEG_INF)
            m = jnp.max(s, axis=-1, keepdims=True)
            p = jnp.exp(s - m)
            l = jnp.sum(p, axis=-1, keepdims=True)
            outs.append(jnp.dot(p.astype(BF16), v, preferred_element_type=F32) / l)
            lses.append(m + jnp.log(l))
        o_ref[pl.ds(q0, ATTN_TQ), :] = jnp.where(head0, outs[0], outs[1]).astype(BF16)
        lse_ref[pl.ds(q0, ATTN_TQ), :] = jnp.where(head0, lses[0], lses[1])
        return carry

    lax.fori_loop(0, nq, body, 0)


def _attn_branch(qkv, dilation, B, S):
    L = S // dilation
    pv = qkv
    acb = ATTN_WIDTH // LANES
    in_spec = lambda part: pl.BlockSpec((None, L, LANES),
                                        lambda b, r, h: (b, 0, (part * dilation + r) * acb + h))
    out_spec = pl.BlockSpec((None, L, LANES), lambda b, r, h: (b, 0, r * acb + h))
    o, lse = pl.pallas_call(
        functools.partial(_attn_kernel, L=L),
        grid=(B, dilation, acb),
        in_specs=[in_spec(0), in_spec(1), in_spec(2)],
        out_specs=[out_spec, out_spec],
        out_shape=[jax.ShapeDtypeStruct((B, L, dilation * ATTN_WIDTH), BF16),
                   jax.ShapeDtypeStruct((B, L, dilation * ATTN_WIDTH), F32)],
        compiler_params=_cparams(("parallel", "parallel", "parallel")),
        name=f"attn_d{dilation}",
    )(pv, pv, pv)
    return o.reshape(B * S, ATTN_WIDTH), lse.reshape(B * S, ATTN_WIDTH)


def _hgrn_chunk(q, kk, lf, v, state_t, tri, reverse):
    C = HGRN_CHUNK
    SUB = HGRN_SUB
    nsub = C // SUB
    b = jnp.dot(tri, lf, preferred_element_type=F32, precision=HIGHEST)
    col = lax.broadcasted_iota(I32, (SUB, C), 1)
    row = lax.broadcasted_iota(I32, (SUB, C), 0)
    kk_bf_rows = []
    score_rows = []
    for i in range(nsub):
        r0 = i * SUB
        bi = b[r0:r0 + SUB]
        qi = q[r0:r0 + SUB]
        ki = kk[r0:r0 + SUB]
        diag = jnp.zeros((SUB, C), F32)
        for s in range(SUB):
            e = jnp.exp(jnp.minimum(bi - bi[s:s + 1], 0.0))
            colv = jnp.sum(qi * e * ki[s:s + 1], axis=-1, keepdims=True)
            diag = jnp.where(col == r0 + s, colv, diag)
        if reverse:
            keep = (col - r0) >= row
        else:
            keep = (col - r0) <= row
        diag = jnp.where(jnp.logical_and(keep, jnp.logical_and(col >= r0, col < r0 + SUB)), diag, 0.0)
        if reverse:
            has_off = i < nsub - 1
            bref = b[r0 + SUB:r0 + SUB + 1] if has_off else None
            off_mask = col >= r0 + SUB
        else:
            has_off = i > 0
            bref = b[r0 - 1:r0] if has_off else None
            off_mask = col < r0
        if has_off:
            qs = (qi * jnp.exp(bi - bref)).astype(BF16)
            ks = (kk * jnp.exp(jnp.minimum(bref - b, 0.0))).astype(BF16)
            off = lax.dot_general(qs, ks, (((1,), (1,)), ((), ())), preferred_element_type=F32)
            score_rows.append(jnp.where(off_mask, off, diag))
        else:
            score_rows.append(diag)
    scores = jnp.concatenate(score_rows, axis=0)
    b_edge = b[0:1] if reverse else b[C - 1:C]
    o = jnp.dot(scores.astype(BF16), v.astype(BF16), preferred_element_type=F32)
    qd = (q * jnp.exp(b)).astype(BF16)
    o = o + lax.dot_general(qd, state_t.astype(BF16), (((1,), (1,)), ((), ())),
                            preferred_element_type=F32)
    kd = (kk * jnp.exp(b_edge - b)).astype(BF16)
    upd = lax.dot_general(v.astype(BF16), kd, (((0,), (0,)), ((), ())), preferred_element_type=F32)
    state_t = state_t * jnp.exp(b_edge) + upd
    return o, state_t


def _hgrn_kernel(q_ref, zf_ref, zb_ref, v_ref, g_ref, lbf_ref, lbb_ref, ng_ref, o_ref, acc_ref, *, S):
    C = HGRN_CHUNK
    n_chunks = S // C
    r = lax.broadcasted_iota(I32, (C, C), 0)
    c = lax.broadcasted_iota(I32, (C, C), 1)
    tri_f = (c <= r).astype(F32)
    tri_b = (c >= r).astype(F32)
    lbf = lbf_ref[...]
    lbb = lbb_ref[...]
    acc_ref[...] = jnp.zeros_like(acc_ref)

    def gates(z, lb):
        sg = jax.nn.sigmoid(z)
        return jnp.log(lb + (1.0 - lb) * sg), (1.0 - lb) * jax.nn.sigmoid(-z)

    def body(n, carry):
        st_f, st_b = carry
        rf = pl.multiple_of(n * C, C)
        rb = pl.multiple_of((n_chunks - 1 - n) * C, C)
        lf_f, kk_f = gates(zf_ref[pl.ds(rf, C), :].astype(F32), lbf)
        o_f, st_f = _hgrn_chunk(q_ref[pl.ds(rf, C), :].astype(F32), kk_f, lf_f,
                                v_ref[pl.ds(rf, C), :].astype(F32), st_f, tri_f, False)
        acc_ref[pl.ds(rf, C), :] += o_f
        lf_b, kk_b = gates(zb_ref[pl.ds(rb, C), :].astype(F32), lbb)
        o_b, st_b = _hgrn_chunk(q_ref[pl.ds(rb, C), :].astype(F32), kk_b, lf_b,
                                v_ref[pl.ds(rb, C), :].astype(F32), st_b, tri_b, True)
        acc_ref[pl.ds(rb, C), :] += o_b
        return st_f, st_b

    z0 = jnp.zeros((HGRN_HEAD_DIM, HGRN_HEAD_DIM), F32)
    lax.fori_loop(0, n_chunks, body, (z0, z0))

    ng = ng_ref[...]

    def norm_body(n, carry):
        r0 = pl.multiple_of(n * 512, 512)
        t = acc_ref[pl.ds(r0, 512), :]
        g = g_ref[pl.ds(r0, 512), :].astype(F32)
        y = t * lax.rsqrt(jnp.mean(t * t, axis=-1, keepdims=True) + RMS_EPS) * ng
        o_ref[pl.ds(r0, 512), :] = (y * (g * jax.nn.sigmoid(g))).astype(BF16)
        return carry

    lax.fori_loop(0, S // 512, norm_body, 0)


def _hgrn(proj, lb_fwd, lb_bwd, norm_g, B, S):
    P = proj.shape[1]
    pv = proj.reshape(B, S, P)
    base = 3 * ATTN_WIDTH // LANES
    nh = HGRN_HEADS
    in_spec = lambda k: pl.BlockSpec((None, S, LANES), lambda b, h: (b, 0, base + k * nh + h))
    vec_spec = pl.BlockSpec((None, 1, LANES), lambda b, h: (h, 0, 0))
    out = pl.pallas_call(
        functools.partial(_hgrn_kernel, S=S),
        grid=(B, nh),
        in_specs=[in_spec(0), in_spec(1), in_spec(2), in_spec(3), in_spec(4), vec_spec, vec_spec, vec_spec],
        out_specs=pl.BlockSpec((None, S, LANES), lambda b, h: (b, 0, h)),
        out_shape=jax.ShapeDtypeStruct((B, S, HGRN_WIDTH), BF16),
        scratch_shapes=[pltpu.VMEM((S, LANES), F32)],
        compiler_params=_cparams(("parallel", "parallel")),
        name="hgrn2",
    )(pv, pv, pv, pv, pv, lb_fwd.reshape(nh, 1, LANES), lb_bwd.reshape(nh, 1, LANES),
      norm_g.reshape(nh, 1, LANES))
    return out.reshape(B * S, HGRN_WIDTH)


def _layer_norm_rows(y, g, b):
    mu = jnp.mean(y, axis=-1, keepdims=True)
    d = y - mu
    var = jnp.mean(d * d, axis=-1, keepdims=True)
    return d * lax.rsqrt(var + LN_EPS) * g + b


def _mix_kernel(o1_ref, o2_ref, o3_ref, l1_ref, l2_ref, l3_ref, rec_ref, x_ref, grp_ref, ag_ref, w_ref,
                gate_ref, lng_ref, lnb_ref, sc_ref, sh_ref, x1_ref, u2_ref):
    l1, l2, l3 = l1_ref[...], l2_ref[...], l3_ref[...]
    m = jnp.maximum(jnp.maximum(l1, l2), l3)
    e1, e2, e3 = jnp.exp(l1 - m), jnp.exp(l2 - m), jnp.exp(l3 - m)
    attn = (e1 * o1_ref[...].astype(F32) + e2 * o2_ref[...].astype(F32)
            + e3 * o3_ref[...].astype(F32)) / (e1 + e2 + e3)
    ms = jnp.dot((attn * attn).astype(BF16), grp_ref[...], preferred_element_type=F32)
    normed = attn * lax.rsqrt(ms + RMS_EPS) * ag_ref[...]
    mixed = jnp.concatenate([normed.astype(BF16), rec_ref[...]], axis=-1)
    mix = jnp.dot(mixed, w_ref[...], preferred_element_type=F32)
    x1 = _layer_norm_rows(ALPHA * x_ref[...] + gate_ref[...] * mix, lng_ref[...], lnb_ref[...])
    x1_ref[...] = x1
    u2_ref[...] = x1 * (1.0 + sc_ref[...]) + sh_ref[...]


def _mix(o_branches, lse_branches, rec, x2, attn_norm_g, w_out_bf, gate1, ln_g, ln_b, scale2, shift2, S):
    T, D = x2.shape
    per_b = S // MIX_TM
    head = jnp.arange(ATTN_WIDTH) // ATTN_HEAD_DIM
    grp = jnp.where(head[:, None] == head[None, :], 1.0 / ATTN_HEAD_DIM, 0.0).astype(BF16)
    row = lambda w: pl.BlockSpec((MIX_TM, w), lambda i: (i, 0))
    const = lambda shape: pl.BlockSpec(shape, lambda i: (0,) * len(shape))
    per_batch = pl.BlockSpec((None, 1, D), lambda i: (i // per_b, 0, 0))
    return pl.pallas_call(
        _mix_kernel,
        grid=(T // MIX_TM,),
        in_specs=[row(ATTN_WIDTH)] * 6 + [row(HGRN_WIDTH), row(D), const((ATTN_WIDTH, ATTN_WIDTH)),
                  const((1, ATTN_WIDTH)), const((D, D)), per_batch, const((1, D)), const((1, D)),
                  per_batch, per_batch],
        out_specs=[row(D), row(D)],
        out_shape=[jax.ShapeDtypeStruct((T, D), F32)] * 2,
        compiler_params=_cparams(("parallel",)),
        name="mix_out_ln1",
    )(*o_branches, *lse_branches, rec, x2, grp, attn_norm_g.reshape(1, -1), w_out_bf, gate1,
      ln_g.reshape(1, D), ln_b.reshape(1, D), scale2, shift2)


def _route_kernel(u_ref, w_ref, bias_ref, idx_ref, gw_ref, rank_ref, cnt_ref, run_ref):
    i = pl.program_id(0)

    @pl.when(i == 0)
    def _():
        run_ref[...] = jnp.zeros_like(run_ref)

    tm = ROUTE_TM
    logits = jnp.dot(u_ref[...], w_ref[...], preferred_element_type=F32, precision=HIGHEST)
    scores = jax.nn.sigmoid(logits)
    sel = scores + bias_ref[...]
    lane = lax.broadcasted_iota(I32, (tm, N_EXPERTS), 1)
    lane_f = lane.astype(F32)
    group = lane // GROUP_SIZE
    neg = jnp.float32(-jnp.inf)

    def first_argmax(vals):
        mx = jnp.max(vals, axis=-1, keepdims=True)
        idx = jnp.min(jnp.where(vals == mx, lane_f, float(N_EXPERTS)), axis=-1, keepdims=True)
        return mx, idx

    gscore = []
    for g in range(N_EXPERT_GROUPS):
        vals = jnp.where(group == g, sel, neg)
        m1, i1 = first_argmax(vals)
        m2 = jnp.max(jnp.where(lane_f == i1, neg, vals), axis=-1, keepdims=True)
        gscore.append(m1 + m2)
    keep_f = jnp.zeros((tm, N_EXPERTS), F32)
    for g in range(N_EXPERT_GROUPS):
        beaten = jnp.zeros((tm, 1), F32)
        for h in range(N_EXPERT_GROUPS):
            if h == g:
                continue
            ahead = (gscore[h] >= gscore[g]) if h < g else (gscore[h] > gscore[g])
            beaten = beaten + jnp.where(ahead, 1.0, 0.0)
        gkeep = jnp.where(beaten < TOPK_GROUPS, 1.0, 0.0)
        keep_f = jnp.where(group == g, gkeep, keep_f)
    vals = jnp.where(keep_f > 0.5, sel, neg)

    lane_o = lax.broadcasted_iota(I32, (tm, LANES), 1)
    idx_out = jnp.zeros((tm, LANES), F32)
    gw_out = jnp.zeros((tm, LANES), F32)
    chosen = jnp.zeros((tm, N_EXPERTS), F32)
    picks = []
    wsum = jnp.zeros((tm, 1), F32)
    for k in range(TOP_K):
        _, ik = first_argmax(vals)
        hit = lane_f == ik
        sk = jnp.sum(jnp.where(hit, scores, 0.0), axis=-1, keepdims=True)
        vals = jnp.where(hit, neg, vals)
        chosen = jnp.where(hit, 1.0, chosen)
        picks.append((ik, sk))
        wsum = wsum + sk
        idx_out = jnp.where(lane_o == k, ik, idx_out)
    for k, (ik, sk) in enumerate(picks):
        gw_out = jnp.where(lane_o == k, sk / wsum * ROUTED_SCALE, gw_out)

    r = lax.broadcasted_iota(I32, (tm, tm), 0)
    c = lax.broadcasted_iota(I32, (tm, tm), 1)
    strict_lower = jnp.where(c < r, 1.0, 0.0).astype(BF16)
    before = jnp.dot(strict_lower, chosen.astype(BF16), preferred_element_type=F32) + run_ref[...]
    rank_out = jnp.zeros((tm, LANES), F32)
    for k, (ik, sk) in enumerate(picks):
        rk = jnp.sum(jnp.where(lane_f == ik, before, 0.0), axis=-1, keepdims=True)
        rank_out = jnp.where(lane_o == k, rk, rank_out)
    run_ref[...] = run_ref[...] + jnp.sum(chosen, axis=0, keepdims=True)

    idx_ref[...] = idx_out.astype(I32)
    gw_ref[...] = gw_out
    rank_ref[...] = rank_out.astype(I32)
    cnt_ref[...] = run_ref[...]


def _route(u2, w_router, router_bias):
    T, D = u2.shape
    row = lambda w: pl.BlockSpec((ROUTE_TM, w), lambda i: (i, 0))
    idx, gw, rank, cnt = pl.pallas_call(
        _route_kernel,
        grid=(T // ROUTE_TM,),
        in_specs=[row(D), pl.BlockSpec((D, N_EXPERTS), lambda i: (0, 0)),
                  pl.BlockSpec((1, N_EXPERTS), lambda i: (0, 0))],
        out_specs=[row(LANES), row(LANES), row(LANES), pl.BlockSpec((1, N_EXPERTS), lambda i: (0, 0))],
        out_shape=[jax.ShapeDtypeStruct((T, LANES), I32), jax.ShapeDtypeStruct((T, LANES), F32),
                   jax.ShapeDtypeStruct((T, LANES), I32), jax.ShapeDtypeStruct((1, N_EXPERTS), F32)],
        scratch_shapes=[pltpu.VMEM((1, N_EXPERTS), F32)],
        compiler_params=_cparams(("arbitrary",)),
        name="router",
    )(u2, w_router, router_bias.reshape(1, N_EXPERTS))
    return idx, gw[:, :TOP_K], rank, cnt[0].astype(I32)


def _slot_kernel(idx_ref, rank_ref, start_ref, dest_ref):
    tm = ROUTE_TM
    lane_e = lax.broadcasted_iota(I32, (tm, N_EXPERTS), 1)
    lane_o = lax.broadcasted_iota(I32, (tm, LANES), 1)
    idx = idx_ref[...]
    start = start_ref[...]
    base = jnp.zeros((tm, LANES), F32)
    for k in range(TOP_K):
        sk = jnp.sum(jnp.where(lane_e == idx[:, k:k + 1], start, 0.0), axis=-1, keepdims=True)
        base = jnp.where(lane_o == k, sk, base)
    dest_ref[...] = base.astype(I32) + rank_ref[...]


def _slots(idx, rank, start):
    T = idx.shape[0]
    row = pl.BlockSpec((ROUTE_TM, LANES), lambda i: (i, 0))
    dest = pl.pallas_call(
        _slot_kernel,
        grid=(T // ROUTE_TM,),
        in_specs=[row, row, pl.BlockSpec((1, N_EXPERTS), lambda i: (0, 0))],
        out_specs=row,
        out_shape=jax.ShapeDtypeStruct((T, LANES), I32),
        compiler_params=_cparams(("parallel",)),
        name="slots",
    )(idx, rank, start.astype(F32).reshape(1, N_EXPERTS))
    return dest[:, :TOP_K]


MOE_META_SLOTS = 4
Y_TILE = 8


def _moe_kernel(bexp_ref, nused_ref, meta_hbm, u_hbm, wg_ref, wu_ref, wd_ref, y_hbm,
                meta_smem, xb0, xb1, yb0, yb1, wg_bf, wu_bf, wd_bf, sem_meta, sem_g, sem_s):
    i = pl.program_id(0)
    n_blocks = pl.num_programs(0)
    nused = nused_ref[0]
    last = nused - 1
    D = xb0.shape[1]

    def meta_copy(blk, sl):
        row = pl.ds(pl.multiple_of(blk * (2 * MOE_BLOCK), 2 * MOE_BLOCK), 2 * MOE_BLOCK)
        return pltpu.make_async_copy(meta_hbm.at[row], meta_smem.at[sl], sem_meta.at[sl])

    def gather_rows(sl, xdst, sem):
        for j in range(MOE_BLOCK):
            tok = meta_smem[sl, j]
            pltpu.make_async_copy(u_hbm.at[pl.ds(tok, 1)], xdst.at[pl.ds(j, 1)], sem).start()

    def scatter_rows(sl, ysrc, sem):
        for j in range(MOE_BLOCK):
            dst = meta_smem[sl, MOE_BLOCK + j]
            pltpu.make_async_copy(ysrc.at[pl.ds(j, 1)],
                                  y_hbm.at[dst >> 3, pl.ds(dst & (Y_TILE - 1), 1)], sem).start()

    def wait_rows(buf, sem):
        pltpu.make_async_copy(u_hbm.at[pl.ds(0, MOE_BLOCK)], buf, sem).wait()

    @pl.when(i >= nused)
    def _():
        yb0[...] = jnp.zeros_like(yb0)
        tiles = MOE_BLOCK // Y_TILE
        for a in range(tiles):
            pltpu.make_async_copy(yb0.at[pl.ds(a * Y_TILE, Y_TILE)], y_hbm.at[i * tiles + a], sem_s.at[0]).start()
        wait_rows(yb0, sem_s.at[0])

    def step(p):
        xcur, xnxt = (xb0, xb1) if p == 0 else (xb1, xb0)
        ycur, yprv = (yb0, yb1) if p == 0 else (yb1, yb0)
        s_cur = i % MOE_META_SLOTS
        s_nxt = (i + 1) % MOE_META_SLOTS
        s_nn = (i + 2) % MOE_META_SLOTS
        s_prv = (i + 3) % MOE_META_SLOTS
        nxt = jnp.minimum(i + 1, last)
        nn = jnp.minimum(i + 2, last)

        @pl.when(i == 0)
        def _():
            first = meta_copy(0, 0)
            first.start()
            first.wait()
            dump = meta_copy(n_blocks, MOE_META_SLOTS - 1)
            dump.start()
            dump.wait()
            meta_copy(nxt, 1).start()
            yprv[...] = jnp.zeros_like(yprv)
            gather_rows(0, xcur, sem_g.at[p])

        meta_copy(nxt, s_nxt).wait()
        meta_copy(nn, s_nn).start()

        @pl.when((i == 0) | (bexp_ref[i] != bexp_ref[jnp.maximum(i - 1, 0)]))
        def _():
            wg_bf[...] = wg_ref[...].astype(BF16)
            wu_bf[...] = wu_ref[...].astype(BF16)
            wd_bf[...] = wd_ref[...].astype(BF16)

        wait_rows(xcur, sem_g.at[p])
        gather_rows(s_nxt, xnxt, sem_g.at[1 - p])
        scatter_rows(s_prv, yprv, sem_s.at[1 - p])
        x = xcur[...].astype(BF16)
        hg = jnp.dot(x, wg_bf[...], preferred_element_type=F32)
        hu = jnp.dot(x, wu_bf[...], preferred_element_type=F32)
        act = (hg * jax.nn.sigmoid(hg) * hu).astype(BF16)

        @pl.when(i >= 1)
        def _():
            wait_rows(ycur, sem_s.at[p])

        ycur[...] = jnp.dot(act, wd_bf[...], preferred_element_type=F32)

        @pl.when(i == last)
        def _():
            scatter_rows(s_cur, ycur, sem_s.at[p])
            wait_rows(ycur, sem_s.at[p])
            wait_rows(yprv, sem_s.at[1 - p])
            wait_rows(xnxt, sem_g.at[1 - p])
            meta_copy(nn, s_nn).wait()

    @pl.when((i < nused) & (i % 2 == 0))
    def _():
        step(0)

    @pl.when((i < nused) & (i % 2 == 1))
    def _():
        step(1)


def _moe(u2, bexp, nused, meta, w_gate, w_up, w_down, n_pad):
    T, D = u2.shape
    n_blocks = n_pad // MOE_BLOCK
    E = EXPERT_DIM
    grid_spec = pltpu.PrefetchScalarGridSpec(
        num_scalar_prefetch=2,
        grid=(n_blocks,),
        in_specs=[pl.BlockSpec(memory_space=pl.ANY),
                  pl.BlockSpec(memory_space=pl.ANY),
                  pl.BlockSpec((None, D, E), lambda i, be, nu: (be[i], 0, 0)),
                  pl.BlockSpec((None, D, E), lambda i, be, nu: (be[i], 0, 0)),
                  pl.BlockSpec((None, E, D), lambda i, be, nu: (be[i], 0, 0))],
        out_specs=pl.BlockSpec(memory_space=pl.ANY),
        scratch_shapes=[pltpu.SMEM((MOE_META_SLOTS, 2 * MOE_BLOCK), I32),
                        pltpu.VMEM((MOE_BLOCK, D), F32),
                        pltpu.VMEM((MOE_BLOCK, D), F32),
                        pltpu.VMEM((MOE_BLOCK, D), F32),
                        pltpu.VMEM((MOE_BLOCK, D), F32),
                        pltpu.VMEM((D, E), BF16),
                        pltpu.VMEM((D, E), BF16),
                        pltpu.VMEM((E, D), BF16),
                        pltpu.SemaphoreType.DMA((MOE_META_SLOTS,)),
                        pltpu.SemaphoreType.DMA((2,)),
                        pltpu.SemaphoreType.DMA((2,))])
    return pl.pallas_call(
        _moe_kernel,
        grid_spec=grid_spec,
        out_shape=jax.ShapeDtypeStruct(((n_pad + MOE_BLOCK) // Y_TILE, Y_TILE, D), F32),
        compiler_params=_cparams(("arbitrary",)),
        name="moe_experts",
    )(bexp, nused, meta, u2, w_gate, w_up, w_down)


def _dispatch_plan(idx, rank, counts, T):
    n_assign = T * TOP_K
    n_blocks = -(-(n_assign + N_EXPERTS * (MOE_BLOCK - 1)) // MOE_BLOCK)
    n_pad = n_blocks * MOE_BLOCK
    padded = (counts + MOE_BLOCK - 1) // MOE_BLOCK * MOE_BLOCK
    padded_end = jnp.cumsum(padded)
    start = padded_end - padded
    dest = _slots(idx, rank, start)
    src = jnp.full((n_pad,), -1, I32).at[dest.reshape(-1)].set(jnp.arange(n_assign, dtype=I32))
    is_pad = src < 0
    pad_rank = jnp.cumsum(is_pad.astype(I32)) - 1
    tok = jnp.where(is_pad, 0, src // TOP_K)
    dst = jnp.where(is_pad, n_assign + pad_rank, src)
    tok = jnp.concatenate([tok, jnp.zeros((MOE_BLOCK,), I32)]).reshape(n_blocks + 1, MOE_BLOCK)
    dst = jnp.concatenate([dst, n_pad + jnp.arange(MOE_BLOCK, dtype=I32)]).reshape(n_blocks + 1, MOE_BLOCK)
    meta = jnp.concatenate([tok, dst], axis=1).reshape(-1)
    block_row0 = jnp.arange(n_blocks, dtype=I32) * MOE_BLOCK
    bexp = jnp.minimum(jnp.sum((padded_end[None, :] <= block_row0[:, None]).astype(I32), axis=1),
                       N_EXPERTS - 1).astype(I32)
    nused = (padded_end[-1] // MOE_BLOCK).astype(I32).reshape(1)
    return bexp, nused, meta.astype(I32), n_pad


def _final_kernel(u_ref, y_ref, gw_ref, x1_ref, wg_ref, wu_ref, wd_ref, gate_ref, lng_ref, lnb_ref, o_ref):
    u = u_ref[...].astype(BF16)
    hg = jnp.dot(u, wg_ref[...], preferred_element_type=F32)
    hu = jnp.dot(u, wu_ref[...], preferred_element_type=F32)
    act = (hg * jax.nn.sigmoid(hg) * hu).astype(BF16)
    ffn = jnp.dot(act, wd_ref[...], preferred_element_type=F32)
    gw = gw_ref[...]
    for k in range(TOP_K):
        ffn = ffn + y_ref[:, k, :] * gw[:, k:k + 1]
    o_ref[...] = _layer_norm_rows(ALPHA * x1_ref[...] + gate_ref[...] * ffn, lng_ref[...], lnb_ref[...])


def _final(u2, y_flat, gw, x1, ws_gate_bf, ws_up_bf, ws_down_bf, gate2, ln_g, ln_b, S):
    T, D = u2.shape
    E = ws_gate_bf.shape[1]
    per_b = S // FINAL_TM
    assert Y_TILE == TOP_K
    row = lambda w: pl.BlockSpec((FINAL_TM, w), lambda i: (i, 0))
    const = lambda shape: pl.BlockSpec(shape, lambda i: (0,) * len(shape))
    return pl.pallas_call(
        _final_kernel,
        grid=(T // FINAL_TM,),
        in_specs=[row(D), pl.BlockSpec((FINAL_TM, TOP_K, D), lambda i: (i, 0, 0)), row(TOP_K), row(D),
                  const((D, E)), const((D, E)), const((E, D)),
                  pl.BlockSpec((None, 1, D), lambda i: (i // per_b, 0, 0)), const((1, D)), const((1, D))],
        out_specs=row(D),
        out_shape=jax.ShapeDtypeStruct((T, D), F32),
        compiler_params=_cparams(("parallel",)),
        name="shared_combine_ln2",
    )(u2, y_flat, gw, x1, ws_gate_bf, ws_up_bf, ws_down_bf, gate2, ln_g.reshape(1, D), ln_b.reshape(1, D))


def kernel(x, c, positions, w_ada, b_ada, w_in, lb_logits, attn_norm_g, hgrn_norm_g, w_out, ln1_g, ln1_b,
           w_router, router_bias, expert_w_gate, expert_w_up, expert_w_down, shared_w_gate, shared_w_up,
           shared_w_down, ln2_g, ln2_b):
    B, S, D = x.shape
    T = B * S
    layer = 0
    lower_bounds = jnp.cumsum(jax.nn.softmax(lb_logits.astype(F32), axis=1), axis=1)

    mod = _ada_mod(c, w_ada[layer], b_ada[layer])
    shift1, scale1, gate1, shift2, scale2, gate2 = [m.reshape(B, 1, D) for m in jnp.split(mod, 6, axis=-1)]
    cos_t, sin_t = _rope_tables(positions)
    x2 = x.reshape(T, D)

    proj, qkv4, qkv16 = _in_proj(x2, scale1, shift1, w_in[layer].astype(BF16), cos_t, sin_t, B, S)
    branches = [_attn_branch(qkv, d, B, S)
                for qkv, d in zip((proj.reshape(B, S, -1), qkv4, qkv16), DILATIONS)]
    rec = _hgrn(proj, lower_bounds[0, layer], lower_bounds[1, layer], hgrn_norm_g[layer], B, S)
    x1, u2 = _mix([o for o, _ in branches], [l for _, l in branches], rec, x2, attn_norm_g[layer],
                  w_out[layer].astype(BF16), gate1, ln1_g[layer], ln1_b[layer], scale2, shift2, S)

    idx, gw, rank, counts = _route(u2, w_router[layer], router_bias[layer])
    bexp, nused, meta, n_pad = _dispatch_plan(idx, rank, counts, T)
    y_flat = _moe(u2, bexp, nused, meta, expert_w_gate[layer], expert_w_up[layer], expert_w_down[layer], n_pad)
    out = _final(u2, y_flat, gw, x1, shared_w_gate[layer].astype(BF16), shared_w_up[layer].astype(BF16),
                 shared_w_down[layer].astype(BF16), gate2, ln2_g[layer], ln2_b[layer], S)
    return out.reshape(B, S, D)
```

```python
import functools

import jax
import jax.numpy as jnp
from jax import lax
from jax.experimental import pallas as pl
from jax.experimental.pallas import tpu as pltpu

F32 = jnp.float32
BF16 = jnp.bfloat16
I32 = jnp.int32
HIGHEST = lax.Precision.HIGHEST

D_MODEL = 2048
ATTN_HEADS = 16
ATTN_HEAD_DIM = 64
ATTN_WIDTH = ATTN_HEADS * ATTN_HEAD_DIM
HGRN_HEADS = 8
HGRN_HEAD_DIM = 128
HGRN_WIDTH = HGRN_HEADS * HGRN_HEAD_DIM
IN_PROJ_WIDTH = 3 * ATTN_WIDTH + 5 * HGRN_WIDTH
DILATIONS = (1, 4, 16)
ATTN_HALF = 64
ROPE_THETA = 10000.0
N_EXPERTS = 256
N_EXPERT_GROUPS = 8
GROUP_SIZE = N_EXPERTS // N_EXPERT_GROUPS
TOPK_GROUPS = 4
TOP_K = 8
EXPERT_DIM = 512
ROUTED_SCALE = 2.5
DEPTH = 1
ALPHA = (2 * DEPTH) ** 0.25
LN_EPS = 1e-5
RMS_EPS = 1e-6
NEG_INF = -1e30
LOG2E = 1.4426950408889634

LANES = 128
VMEM_LIMIT = 56 * 1024 * 1024

ADA_TN = 1024
ROPE_TM = 2048
INPROJ_TM = 512
INPROJ_TN = 1024
ATTN_TQ = 128
ATTN_WK = 256
ATTN_UNROLL = 4
HGRN_CHUNK = 64
HGRN_SUB = 16
MIX_TM = 256
ROUTE_TM = 256
MOE_BLOCK = 128
ROW_SUB = D_MODEL // LANES
ROW_PITCH = 24
FINAL_TM = 128


def _cparams(sem):
    return pltpu.CompilerParams(dimension_semantics=sem, vmem_limit_bytes=VMEM_LIMIT)


def _ada_kernel(c_ref, w_ref, b_ref, o_ref):
    c = c_ref[...]
    sc = c * jax.nn.sigmoid(c)
    o_ref[...] = jnp.dot(sc, w_ref[...], preferred_element_type=F32, precision=HIGHEST) + b_ref[...]


def _ada_mod(c, w_ada, b_ada):
    B, D = c.shape
    N = w_ada.shape[1]
    c8 = jnp.zeros((8, D), F32).at[:B].set(c)
    out = pl.pallas_call(
        _ada_kernel,
        grid=(N // ADA_TN,),
        in_specs=[pl.BlockSpec((8, D), lambda j: (0, 0)),
                  pl.BlockSpec((D, ADA_TN), lambda j: (0, j)),
                  pl.BlockSpec((1, ADA_TN), lambda j: (0, j))],
        out_specs=pl.BlockSpec((8, ADA_TN), lambda j: (0, j)),
        out_shape=jax.ShapeDtypeStruct((8, N), F32),
        compiler_params=_cparams(("parallel",)),
        name="ada_mod",
    )(c8, w_ada, b_ada.reshape(1, N))
    return out[:B]


def _rope_kernel(pos_ref, invf_ref, sign_ref, cos_ref, sin_ref):
    ang = pos_ref[...].astype(F32) * invf_ref[...]
    cos_ref[...] = jnp.cos(ang)
    sin_ref[...] = jnp.sin(ang) * sign_ref[...]


def _rope_tables(positions):
    T = positions.size
    half = ATTN_HEAD_DIM // 2
    inv_freq = ROPE_THETA ** (-jnp.arange(half, dtype=F32) / half)
    lane = jnp.arange(LANES)
    invf = inv_freq[lane % half].reshape(1, LANES)
    sign = jnp.where((lane % ATTN_HEAD_DIM) < half, -1.0, 1.0).astype(F32).reshape(1, LANES)
    return pl.pallas_call(
        _rope_kernel,
        grid=(T // ROPE_TM,),
        in_specs=[pl.BlockSpec((ROPE_TM, 1), lambda i: (i, 0)),
                  pl.BlockSpec((1, LANES), lambda i: (0, 0)),
                  pl.BlockSpec((1, LANES), lambda i: (0, 0))],
        out_specs=[pl.BlockSpec((ROPE_TM, LANES), lambda i: (i, 0)),
                   pl.BlockSpec((ROPE_TM, LANES), lambda i: (i, 0))],
        out_shape=[jax.ShapeDtypeStruct((T, LANES), F32)] * 2,
        compiler_params=_cparams(("parallel",)),
        name="rope_tables",
    )(positions.reshape(T, 1), invf, sign)


def _inproj_kernel(x_ref, sc_ref, sh_ref, w_ref, cos_ref, sin_ref, o_ref, o4_ref, o16_ref, stage_ref):
    j = pl.program_id(1)
    u = (x_ref[...] * (1.0 + sc_ref[...]) + sh_ref[...]).astype(BF16)
    acc = jnp.dot(u, w_ref[...], preferred_element_type=F32)

    @pl.when(j < 2)
    def _():
        qscale = jnp.where(j == 0, ATTN_HEAD_DIM ** -0.5, 1.0).astype(F32)
        cos = cos_ref[...] * qscale
        sin = sin_ref[...] * qscale
        lane = lax.broadcasted_iota(I32, cos.shape, 1)
        first = (lane % ATTN_HEAD_DIM) < (ATTN_HEAD_DIM // 2)
        for cb in range(INPROJ_TN // LANES):
            a = acc[:, cb * LANES:(cb + 1) * LANES]
            partner = jnp.where(first, pltpu.roll(a, LANES - ATTN_HEAD_DIM // 2, 1),
                                pltpu.roll(a, ATTN_HEAD_DIM // 2, 1))
            stage_ref[cb] = a * cos + partner * sin

    @pl.when(j == 2)
    def _():
        for cb in range(INPROJ_TN // LANES):
            stage_ref[cb] = acc[:, cb * LANES:(cb + 1) * LANES]

    @pl.when(j < 3)
    def _():
        for cb in range(INPROJ_TN // LANES):
            o_ref[:, cb * LANES:(cb + 1) * LANES] = stage_ref[cb].astype(BF16)
            for d, od_ref in ((DILATIONS[1], o4_ref), (DILATIONS[2], o16_ref)):
                rows = INPROJ_TM // d
                for r in range(d):
                    c0 = r * INPROJ_TN + cb * LANES
                    od_ref[:, c0:c0 + LANES] = stage_ref[cb, pl.ds(r, rows, stride=d), :].astype(BF16)

    @pl.when(j >= 3)
    def _():
        o_ref[...] = acc.astype(BF16)


def _in_proj(x2, scale1, shift1, w_in_bf, cos_t, sin_t, B, S):
    T, D = x2.shape
    P = w_in_bf.shape[1]
    per_b = S // INPROJ_TM
    assert INPROJ_TN == ATTN_WIDTH
    d4, d16 = DILATIONS[1], DILATIONS[2]
    strided_spec = lambda d: pl.BlockSpec((None, INPROJ_TM // d, d * INPROJ_TN),
                                          lambda i, j: (i // per_b, i % per_b, jnp.minimum(j, 2)))
    strided_shape = lambda d: jax.ShapeDtypeStruct((B, S // d, 3 * d * ATTN_WIDTH), BF16)
    return pl.pallas_call(
        _inproj_kernel,
        grid=(T // INPROJ_TM, P // INPROJ_TN),
        in_specs=[pl.BlockSpec((INPROJ_TM, D), lambda i, j: (i, 0)),
                  pl.BlockSpec((None, 1, D), lambda i, j: (i // per_b, 0, 0)),
                  pl.BlockSpec((None, 1, D), lambda i, j: (i // per_b, 0, 0)),
                  pl.BlockSpec((D, INPROJ_TN), lambda i, j: (0, j)),
                  pl.BlockSpec((INPROJ_TM, LANES), lambda i, j: (i, 0)),
                  pl.BlockSpec((INPROJ_TM, LANES), lambda i, j: (i, 0))],
        out_specs=[pl.BlockSpec((INPROJ_TM, INPROJ_TN), lambda i, j: (i, j)), strided_spec(d4), strided_spec(d16)],
        out_shape=[jax.ShapeDtypeStruct((T, P), BF16), strided_shape(d4), strided_shape(d16)],
        scratch_shapes=[pltpu.VMEM((INPROJ_TN // LANES, INPROJ_TM, LANES), F32)],
        compiler_params=_cparams(("parallel", "arbitrary")),
        name="in_proj",
    )(x2, scale1, shift1, w_in_bf, cos_t, sin_t)


def _attn_kernel(q_ref, k_ref, v_ref, o_ref, lse_ref, *, L):
    nq = L // ATTN_TQ
    lane = lax.broadcasted_iota(I32, (1, LANES), 1)
    head0 = lane < ATTN_HEAD_DIM
    rel = (lax.broadcasted_iota(I32, (ATTN_TQ, ATTN_WK), 1)
           - lax.broadcasted_iota(I32, (ATTN_TQ, ATTN_WK), 0))

    def body(i, carry):
        q0 = pl.multiple_of(i * ATTN_TQ, ATTN_TQ)
        ks = pl.multiple_of(jnp.clip(i * ATTN_TQ - ATTN_HALF, 0, L - ATTN_WK), ATTN_HALF)
        q = q_ref[pl.ds(q0, ATTN_TQ), :]
        k = k_ref[pl.ds(ks, ATTN_WK), :]
        v = v_ref[pl.ds(ks, ATTN_WK), :]
        mask = jnp.abs(rel + (ks - q0)) <= ATTN_HALF
        outs, lses = [], []
        for hmask in (head0, jnp.logical_not(head0)):
            qh = jnp.where(hmask, q, jnp.zeros_like(q))
            s = lax.dot_general(qh, k, (((1,), (1,)), ((), ())), preferred_element_type=F32)
            s = jnp.where(mask, s, NEG_INF)
            m = jnp.max(s, axis=-1, keepdims=True)
            p = jnp.exp(s - m)
            l = jnp.sum(p, axis=-1, keepdims=True)
            outs.append(jnp.dot(p.astype(BF16), v, preferred_element_type=F32) / l)
            lses.append(m + jnp.log(l))
        o_ref[pl.ds(q0, ATTN_TQ), :] = jnp.where(head0, outs[0], outs[1]).astype(BF16)
        lse_ref[pl.ds(q0, ATTN_TQ), :] = jnp.where(head0, lses[0], lses[1])
        return carry

    lax.fori_loop(0, nq, body, 0, unroll=ATTN_UNROLL)


def _attn_branch(qkv, dilation, B, S):
    L = S // dilation
    pv = qkv
    acb = ATTN_WIDTH // LANES
    in_spec = lambda part: pl.BlockSpec((None, L, LANES),
                                        lambda b, r, h: (b, 0, (part * dilation + r) * acb + h))
    out_spec = pl.BlockSpec((None, L, LANES), lambda b, r, h: (b, 0, r * acb + h))
    o, lse = pl.pallas_call(
        functools.partial(_attn_kernel, L=L),
        grid=(B, dilation, acb),
        in_specs=[in_spec(0), in_spec(1), in_spec(2)],
        out_specs=[out_spec, out_spec],
        out_shape=[jax.ShapeDtypeStruct((B, L, dilation * ATTN_WIDTH), BF16),
                   jax.ShapeDtypeStruct((B, L, dilation * ATTN_WIDTH), F32)],
        compiler_params=_cparams(("parallel", "parallel", "parallel")),
        name=f"attn_d{dilation}",
    )(pv, pv, pv)
    return o.reshape(B * S, ATTN_WIDTH), lse.reshape(B * S, ATTN_WIDTH)


def _hgrn_chunk(q, kk, lf, v, state_t, tri, reverse):
    C = HGRN_CHUNK
    SUB = HGRN_SUB
    nsub = C // SUB
    b = jnp.dot(tri, lf, preferred_element_type=F32, precision=HIGHEST) * LOG2E
    col = lax.broadcasted_iota(I32, (SUB, C), 1)
    row = lax.broadcasted_iota(I32, (SUB, C), 0)
    score_rows = []
    for i in range(nsub):
        r0 = i * SUB
        bi = b[r0:r0 + SUB]
        qi = q[r0:r0 + SUB]
        ki = kk[r0:r0 + SUB]
        diag = jnp.zeros((SUB, C), F32)
        for s in range(SUB):
            e = jnp.exp2(bi - bi[s:s + 1])
            colv = jnp.sum(qi * e * ki[s:s + 1], axis=-1, keepdims=True)
            diag = jnp.where(col == r0 + s, colv, diag)
        if reverse:
            keep = (col - r0) >= row
        else:
            keep = (col - r0) <= row
        diag = jnp.where(jnp.logical_and(keep, jnp.logical_and(col >= r0, col < r0 + SUB)), diag, 0.0)
        if reverse:
            has_off = i < nsub - 1
            bref = b[r0 + SUB:r0 + SUB + 1] if has_off else None
            off_mask = col >= r0 + SUB
        else:
            has_off = i > 0
            bref = b[r0 - 1:r0] if has_off else None
            off_mask = col < r0
        if has_off:
            qs = (qi * jnp.exp2(bi - bref)).astype(BF16)
            ks = (kk * jnp.exp2(bref - b)).astype(BF16)
            off = lax.dot_general(qs, ks, (((1,), (1,)), ((), ())), preferred_element_type=F32)
            score_rows.append(jnp.where(off_mask, off, diag))
        else:
            score_rows.append(diag)
    scores = jnp.concatenate(score_rows, axis=0)
    b_edge = b[0:1] if reverse else b[C - 1:C]
    o = jnp.dot(scores.astype(BF16), v.astype(BF16), preferred_element_type=F32)
    qd = (q * jnp.exp2(b)).astype(BF16)
    o = o + lax.dot_general(qd, state_t.astype(BF16), (((1,), (1,)), ((), ())),
                            preferred_element_type=F32)
    kd = (kk * jnp.exp2(b_edge - b)).astype(BF16)
    upd = lax.dot_general(v.astype(BF16), kd, (((0,), (0,)), ((), ())), preferred_element_type=F32)
    state_t = state_t * jnp.exp2(b_edge) + upd
    return o, state_t


def _hgrn_kernel(q_ref, zf_ref, zb_ref, v_ref, g_ref, lbf_ref, lbb_ref, ng_ref, o_ref, acc_ref, *, S):
    C = HGRN_CHUNK
    n_chunks = S // C
    r = lax.broadcasted_iota(I32, (C, C), 0)
    c = lax.broadcasted_iota(I32, (C, C), 1)
    tri_f = (c <= r).astype(F32)
    tri_b = (c >= r).astype(F32)
    lbf = lbf_ref[...]
    lbb = lbb_ref[...]
    acc_ref[...] = jnp.zeros_like(acc_ref)

    def gates(z, lb):
        sg = jax.nn.sigmoid(z)
        return jnp.log(lb + (1.0 - lb) * sg), (1.0 - lb) * (1.0 - sg)

    def body(n, carry):
        st_f, st_b = carry
        rf = pl.multiple_of(n * C, C)
        rb = pl.multiple_of((n_chunks - 1 - n) * C, C)
        lf_f, kk_f = gates(zf_ref[pl.ds(rf, C), :].astype(F32), lbf)
        o_f, st_f = _hgrn_chunk(q_ref[pl.ds(rf, C), :].astype(F32), kk_f, lf_f,
                                v_ref[pl.ds(rf, C), :].astype(F32), st_f, tri_f, False)
        acc_ref[pl.ds(rf, C), :] += o_f
        lf_b, kk_b = gates(zb_ref[pl.ds(rb, C), :].astype(F32), lbb)
        o_b, st_b = _hgrn_chunk(q_ref[pl.ds(rb, C), :].astype(F32), kk_b, lf_b,
                                v_ref[pl.ds(rb, C), :].astype(F32), st_b, tri_b, True)
        acc_ref[pl.ds(rb, C), :] += o_b
        return st_f, st_b

    z0 = jnp.zeros((HGRN_HEAD_DIM, HGRN_HEAD_DIM), F32)
    lax.fori_loop(0, n_chunks, body, (z0, z0))

    ng = ng_ref[...]

    def norm_body(n, carry):
        r0 = pl.multiple_of(n * 512, 512)
        t = acc_ref[pl.ds(r0, 512), :]
        g = g_ref[pl.ds(r0, 512), :].astype(F32)
        y = t * lax.rsqrt(jnp.mean(t * t, axis=-1, keepdims=True) + RMS_EPS) * ng
        o_ref[pl.ds(r0, 512), :] = (y * (g * jax.nn.sigmoid(g))).astype(BF16)
        return carry

    lax.fori_loop(0, S // 512, norm_body, 0)


def _hgrn(proj, lb_fwd, lb_bwd, norm_g, B, S):
    P = proj.shape[1]
    pv = proj.reshape(B, S, P)
    base = 3 * ATTN_WIDTH // LANES
    nh = HGRN_HEADS
    in_spec = lambda k: pl.BlockSpec((None, S, LANES), lambda b, h: (b, 0, base + k * nh + h))
    vec_spec = pl.BlockSpec((None, 1, LANES), lambda b, h: (h, 0, 0))
    out = pl.pallas_call(
        functools.partial(_hgrn_kernel, S=S),
        grid=(B, nh),
        in_specs=[in_spec(0), in_spec(1), in_spec(2), in_spec(3), in_spec(4), vec_spec, vec_spec, vec_spec],
        out_specs=pl.BlockSpec((None, S, LANES), lambda b, h: (b, 0, h)),
        out_shape=jax.ShapeDtypeStruct((B, S, HGRN_WIDTH), BF16),
        scratch_shapes=[pltpu.VMEM((S, LANES), F32)],
        compiler_params=_cparams(("parallel", "parallel")),
        name="hgrn2",
    )(pv, pv, pv, pv, pv, lb_fwd.reshape(nh, 1, LANES), lb_bwd.reshape(nh, 1, LANES),
      norm_g.reshape(nh, 1, LANES))
    return out.reshape(B * S, HGRN_WIDTH)


def _layer_norm_rows(y, g, b):
    mu = jnp.mean(y, axis=-1, keepdims=True)
    d = y - mu
    var = jnp.mean(d * d, axis=-1, keepdims=True)
    return d * lax.rsqrt(var + LN_EPS) * g + b


def _mix_kernel(o1_ref, o2_ref, o3_ref, l1_ref, l2_ref, l3_ref, rec_ref, x_ref, grp_ref, ag_ref, w_ref,
                gate_ref, lng_ref, lnb_ref, sc_ref, sh_ref, x1_ref, u2_ref, urows_ref):
    l1, l2, l3 = l1_ref[...], l2_ref[...], l3_ref[...]
    m = jnp.maximum(jnp.maximum(l1, l2), l3)
    e1, e2, e3 = jnp.exp(l1 - m), jnp.exp(l2 - m), jnp.exp(l3 - m)
    attn = (e1 * o1_ref[...].astype(F32) + e2 * o2_ref[...].astype(F32)
            + e3 * o3_ref[...].astype(F32)) / (e1 + e2 + e3)
    ms = jnp.dot((attn * attn).astype(BF16), grp_ref[...], preferred_element_type=F32)
    normed = attn * lax.rsqrt(ms + RMS_EPS) * ag_ref[...]
    mixed = jnp.concatenate([normed.astype(BF16), rec_ref[...]], axis=-1)
    mix = jnp.dot(mixed, w_ref[...], preferred_element_type=F32)
    x1 = _layer_norm_rows(ALPHA * x_ref[...] + gate_ref[...] * mix, lng_ref[...], lnb_ref[...])
    x1_ref[...] = x1
    u2 = x1 * (1.0 + sc_ref[...]) + sh_ref[...]
    u2_ref[...] = u2
    for cb in range(ROW_SUB):
        urows_ref[pl.ds(cb, MIX_TM, stride=ROW_SUB), :] = u2[:, cb * LANES:(cb + 1) * LANES]


def _mix(o_branches, lse_branches, rec, x2, attn_norm_g, w_out_bf, gate1, ln_g, ln_b, scale2, shift2, S):
    T, D = x2.shape
    per_b = S // MIX_TM
    head = jnp.arange(ATTN_WIDTH) // ATTN_HEAD_DIM
    grp = jnp.where(head[:, None] == head[None, :], 1.0 / ATTN_HEAD_DIM, 0.0).astype(BF16)
    row = lambda w: pl.BlockSpec((MIX_TM, w), lambda i: (i, 0))
    const = lambda shape: pl.BlockSpec(shape, lambda i: (0,) * len(shape))
    per_batch = pl.BlockSpec((None, 1, D), lambda i: (i // per_b, 0, 0))
    return pl.pallas_call(
        _mix_kernel,
        grid=(T // MIX_TM,),
        in_specs=[row(ATTN_WIDTH)] * 6 + [row(HGRN_WIDTH), row(D), const((ATTN_WIDTH, ATTN_WIDTH)),
                  const((1, ATTN_WIDTH)), const((D, D)), per_batch, const((1, D)), const((1, D)),
                  per_batch, per_batch],
        out_specs=[row(D), row(D), pl.BlockSpec((MIX_TM * ROW_SUB, LANES), lambda i: (i, 0))],
        out_shape=[jax.ShapeDtypeStruct((T, D), F32)] * 2 + [jax.ShapeDtypeStruct((T * ROW_SUB, LANES), F32)],
        compiler_params=_cparams(("parallel",)),
        name="mix_out_ln1",
    )(*o_branches, *lse_branches, rec, x2, grp, attn_norm_g.reshape(1, -1), w_out_bf, gate1,
      ln_g.reshape(1, D), ln_b.reshape(1, D), scale2, shift2)


def _route_kernel(u_ref, w_ref, bias_ref, idx_ref, gw_ref, rank_ref, cnt_ref, run_ref):
    i = pl.program_id(0)

    @pl.when(i == 0)
    def _():
        run_ref[...] = jnp.zeros_like(run_ref)

    tm = ROUTE_TM
    logits = jnp.dot(u_ref[...], w_ref[...], preferred_element_type=F32, precision=HIGHEST)
    scores = jax.nn.sigmoid(logits)
    sel = scores + bias_ref[...]
    lane = lax.broadcasted_iota(I32, (tm, N_EXPERTS), 1)
    lane_f = lane.astype(F32)
    group = lane // GROUP_SIZE
    neg = jnp.float32(-jnp.inf)

    def first_argmax(vals):
        mx = jnp.max(vals, axis=-1, keepdims=True)
        idx = jnp.min(jnp.where(vals == mx, lane_f, float(N_EXPERTS)), axis=-1, keepdims=True)
        return mx, idx

    gscore = []
    for g in range(N_EXPERT_GROUPS):
        vals = jnp.where(group == g, sel, neg)
        m1, i1 = first_argmax(vals)
        m2 = jnp.max(jnp.where(lane_f == i1, neg, vals), axis=-1, keepdims=True)
        gscore.append(m1 + m2)
    keep_f = jnp.zeros((tm, N_EXPERTS), F32)
    for g in range(N_EXPERT_GROUPS):
        beaten = jnp.zeros((tm, 1), F32)
        for h in range(N_EXPERT_GROUPS):
            if h == g:
                continue
            ahead = (gscore[h] >= gscore[g]) if h < g else (gscore[h] > gscore[g])
            beaten = beaten + jnp.where(ahead, 1.0, 0.0)
        gkeep = jnp.where(beaten < TOPK_GROUPS, 1.0, 0.0)
        keep_f = jnp.where(group == g, gkeep, keep_f)
    vals = jnp.where(keep_f > 0.5, sel, neg)

    lane_o = lax.broadcasted_iota(I32, (tm, LANES), 1)
    idx_out = jnp.zeros((tm, LANES), F32)
    gw_out = jnp.zeros((tm, LANES), F32)
    chosen = jnp.zeros((tm, N_EXPERTS), F32)
    picks = []
    wsum = jnp.zeros((tm, 1), F32)
    for k in range(TOP_K):
        _, ik = first_argmax(vals)
        hit = lane_f == ik
        sk = jnp.sum(jnp.where(hit, scores, 0.0), axis=-1, keepdims=True)
        vals = jnp.where(hit, neg, vals)
        chosen = jnp.where(hit, 1.0, chosen)
        picks.append((ik, sk))
        wsum = wsum + sk
        idx_out = jnp.where(lane_o == k, ik, idx_out)
    for k, (ik, sk) in enumerate(picks):
        gw_out = jnp.where(lane_o == k, sk / wsum * ROUTED_SCALE, gw_out)

    r = lax.broadcasted_iota(I32, (tm, tm), 0)
    c = lax.broadcasted_iota(I32, (tm, tm), 1)
    strict_lower = jnp.where(c < r, 1.0, 0.0).astype(BF16)
    before = jnp.dot(strict_lower, chosen.astype(BF16), preferred_element_type=F32) + run_ref[...]
    rank_out = jnp.zeros((tm, LANES), F32)
    for k, (ik, sk) in enumerate(picks):
        rk = jnp.sum(jnp.where(lane_f == ik, before, 0.0), axis=-1, keepdims=True)
        rank_out = jnp.where(lane_o == k, rk, rank_out)
    run_ref[...] = run_ref[...] + jnp.sum(chosen, axis=0, keepdims=True)

    idx_ref[...] = idx_out.astype(I32)
    gw_ref[...] = gw_out
    rank_ref[...] = rank_out.astype(I32)
    cnt_ref[...] = run_ref[...]


def _route(u2, w_router, router_bias):
    T, D = u2.shape
    row = lambda w: pl.BlockSpec((ROUTE_TM, w), lambda i: (i, 0))
    idx, gw, rank, cnt = pl.pallas_call(
        _route_kernel,
        grid=(T // ROUTE_TM,),
        in_specs=[row(D), pl.BlockSpec((D, N_EXPERTS), lambda i: (0, 0)),
                  pl.BlockSpec((1, N_EXPERTS), lambda i: (0, 0))],
        out_specs=[row(LANES), row(LANES), row(LANES), pl.BlockSpec((1, N_EXPERTS), lambda i: (0, 0))],
        out_shape=[jax.ShapeDtypeStruct((T, LANES), I32), jax.ShapeDtypeStruct((T, LANES), F32),
                   jax.ShapeDtypeStruct((T, LANES), I32), jax.ShapeDtypeStruct((1, N_EXPERTS), F32)],
        scratch_shapes=[pltpu.VMEM((1, N_EXPERTS), F32)],
        compiler_params=_cparams(("arbitrary",)),
        name="router",
    )(u2, w_router, router_bias.reshape(1, N_EXPERTS))
    return idx, gw[:, :TOP_K], rank, cnt[0].astype(I32)


def _slot_kernel(idx_ref, rank_ref, start_ref, dest_ref):
    tm = ROUTE_TM
    lane_e = lax.broadcasted_iota(I32, (tm, N_EXPERTS), 1)
    lane_o = lax.broadcasted_iota(I32, (tm, LANES), 1)
    idx = idx_ref[...]
    start = start_ref[...]
    base = jnp.zeros((tm, LANES), F32)
    for k in range(TOP_K):
        sk = jnp.sum(jnp.where(lane_e == idx[:, k:k + 1], start, 0.0), axis=-1, keepdims=True)
        base = jnp.where(lane_o == k, sk, base)
    dest_ref[...] = base.astype(I32) + rank_ref[...]


def _slots(idx, rank, start):
    T = idx.shape[0]
    row = pl.BlockSpec((ROUTE_TM, LANES), lambda i: (i, 0))
    dest = pl.pallas_call(
        _slot_kernel,
        grid=(T // ROUTE_TM,),
        in_specs=[row, row, pl.BlockSpec((1, N_EXPERTS), lambda i: (0, 0))],
        out_specs=row,
        out_shape=jax.ShapeDtypeStruct((T, LANES), I32),
        compiler_params=_cparams(("parallel",)),
        name="slots",
    )(idx, rank, start.astype(F32).reshape(1, N_EXPERTS))
    return dest[:, :TOP_K]


MOE_META_SLOTS = 4
Y_TILE = 8


def _moe_kernel(bexp_ref, nused_ref, meta_hbm, u_hbm, wg_ref, wu_ref, wd_ref, y_hbm,
                meta_smem, xb0, xb1, yb0, yb1, wg_bf, wu_bf, wd_bf, sem_meta, sem_g, sem_s):
    i = pl.program_id(0)
    n_blocks = pl.num_programs(0)
    nused = nused_ref[0]
    last = nused - 1

    def meta_copy(blk, sl):
        row = pl.ds(pl.multiple_of(blk * (2 * MOE_BLOCK), 2 * MOE_BLOCK), 2 * MOE_BLOCK)
        return pltpu.make_async_copy(meta_hbm.at[row], meta_smem.at[sl], sem_meta.at[sl])

    def gather_rows(sl, xdst, sem):
        for j in range(MOE_BLOCK):
            row0 = pl.multiple_of(meta_smem[sl, j] * ROW_SUB, ROW_SUB)
            pltpu.make_async_copy(u_hbm.at[pl.ds(row0, ROW_SUB)], xdst.at[pl.ds(j * ROW_PITCH, ROW_SUB)], sem).start()

    def scatter_rows(sl, ysrc, sem):
        for j in range(MOE_BLOCK):
            dst = meta_smem[sl, MOE_BLOCK + j]
            pltpu.make_async_copy(ysrc.at[pl.ds(j, 1)],
                                  y_hbm.at[dst >> 3, pl.ds(dst & (Y_TILE - 1), 1)], sem).start()

    def wait_gather(xdst, sem):
        n = MOE_BLOCK * ROW_SUB
        pltpu.make_async_copy(u_hbm.at[pl.ds(0, n)], xdst.at[pl.ds(0, n)], sem).wait()

    def wait_rows(buf, sem):
        other = yb1 if buf is yb0 else yb0
        pltpu.make_async_copy(other, buf, sem).wait()

    @pl.when(i >= nused)
    def _():
        yb0[...] = jnp.zeros_like(yb0)
        tiles = MOE_BLOCK // Y_TILE
        for a in range(tiles):
            pltpu.make_async_copy(yb0.at[pl.ds(a * Y_TILE, Y_TILE)], y_hbm.at[i * tiles + a], sem_s.at[0]).start()
        wait_rows(yb0, sem_s.at[0])

    def step(p):
        xcur, xnxt = (xb0, xb1) if p == 0 else (xb1, xb0)
        ycur, yprv = (yb0, yb1) if p == 0 else (yb1, yb0)
        s_cur = i % MOE_META_SLOTS
        s_nxt = (i + 1) % MOE_META_SLOTS
        s_nn = (i + 2) % MOE_META_SLOTS
        s_prv = (i + 3) % MOE_META_SLOTS
        nxt = jnp.minimum(i + 1, last)
        nn = jnp.minimum(i + 2, last)

        @pl.when(i == 0)
        def _():
            first = meta_copy(0, 0)
            first.start()
            first.wait()
            dump = meta_copy(n_blocks, MOE_META_SLOTS - 1)
            dump.start()
            dump.wait()
            meta_copy(nxt, 1).start()
            yprv[...] = jnp.zeros_like(yprv)
            gather_rows(0, xcur, sem_g.at[p])

        meta_copy(nxt, s_nxt).wait()
        meta_copy(nn, s_nn).start()

        @pl.when((i == 0) | (bexp_ref[i] != bexp_ref[jnp.maximum(i - 1, 0)]))
        def _():
            wg_bf[...] = wg_ref[...].astype(BF16)
            wu_bf[...] = wu_ref[...].astype(BF16)
            wd_bf[...] = wd_ref[...].astype(BF16)

        wait_gather(xcur, sem_g.at[p])
        gather_rows(s_nxt, xnxt, sem_g.at[1 - p])
        scatter_rows(s_prv, yprv, sem_s.at[1 - p])
        x = jnp.concatenate([xcur[pl.ds(cb, MOE_BLOCK, stride=ROW_PITCH), :].astype(BF16)
                             for cb in range(ROW_SUB)], axis=1)
        hg = jnp.dot(x, wg_bf[...], preferred_element_type=F32)
        hu = jnp.dot(x, wu_bf[...], preferred_element_type=F32)
        act = (hg * jax.nn.sigmoid(hg) * hu).astype(BF16)

        @pl.when(i >= 1)
        def _():
            wait_rows(ycur, sem_s.at[p])

        ycur[...] = jnp.dot(act, wd_bf[...], preferred_element_type=F32)

        @pl.when(i == last)
        def _():
            scatter_rows(s_cur, ycur, sem_s.at[p])
            wait_rows(ycur, sem_s.at[p])
            wait_rows(yprv, sem_s.at[1 - p])
            wait_gather(xnxt, sem_g.at[1 - p])
            meta_copy(nn, s_nn).wait()

    @pl.when((i < nused) & (i % 2 == 0))
    def _():
        step(0)

    @pl.when((i < nused) & (i % 2 == 1))
    def _():
        step(1)


def _moe(u_rows, bexp, nused, meta, w_gate, w_up, w_down, n_pad):
    D = ROW_SUB * LANES
    n_blocks = n_pad // MOE_BLOCK
    E = EXPERT_DIM
    grid_spec = pltpu.PrefetchScalarGridSpec(
        num_scalar_prefetch=2,
        grid=(n_blocks,),
        in_specs=[pl.BlockSpec(memory_space=pl.ANY),
                  pl.BlockSpec(memory_space=pl.ANY),
                  pl.BlockSpec((None, D, E), lambda i, be, nu: (be[i], 0, 0)),
                  pl.BlockSpec((None, D, E), lambda i, be, nu: (be[i], 0, 0)),
                  pl.BlockSpec((None, E, D), lambda i, be, nu: (be[i], 0, 0))],
        out_specs=pl.BlockSpec(memory_space=pl.ANY),
        scratch_shapes=[pltpu.SMEM((MOE_META_SLOTS, 2 * MOE_BLOCK), I32),
                        pltpu.VMEM((MOE_BLOCK * ROW_PITCH, LANES), F32),
                        pltpu.VMEM((MOE_BLOCK * ROW_PITCH, LANES), F32),
                        pltpu.VMEM((MOE_BLOCK, D), F32),
                        pltpu.VMEM((MOE_BLOCK, D), F32),
                        pltpu.VMEM((D, E), BF16),
                        pltpu.VMEM((D, E), BF16),
                        pltpu.VMEM((E, D), BF16),
                        pltpu.SemaphoreType.DMA((MOE_META_SLOTS,)),
                        pltpu.SemaphoreType.DMA((2,)),
                        pltpu.SemaphoreType.DMA((2,))])
    return pl.pallas_call(
        _moe_kernel,
        grid_spec=grid_spec,
        out_shape=jax.ShapeDtypeStruct(((n_pad + MOE_BLOCK) // Y_TILE, Y_TILE, D), F32),
        compiler_params=_cparams(("arbitrary",)),
        name="moe_experts",
    )(bexp, nused, meta, u_rows, w_gate, w_up, w_down)


def _dispatch_plan(idx, rank, counts, T):
    n_assign = T * TOP_K
    n_blocks = -(-(n_assign + N_EXPERTS * (MOE_BLOCK - 1)) // MOE_BLOCK)
    n_pad = n_blocks * MOE_BLOCK
    padded = (counts + MOE_BLOCK - 1) // MOE_BLOCK * MOE_BLOCK
    padded_end = jnp.cumsum(padded)
    start = padded_end - padded
    dest = _slots(idx, rank, start)
    src = jnp.full((n_pad,), -1, I32).at[dest.reshape(-1)].set(jnp.arange(n_assign, dtype=I32))
    is_pad = src < 0
    pad_rank = jnp.cumsum(is_pad.astype(I32)) - 1
    tok = jnp.where(is_pad, 0, src // TOP_K)
    dst = jnp.where(is_pad, n_assign + pad_rank, src)
    tok = jnp.concatenate([tok, jnp.zeros((MOE_BLOCK,), I32)]).reshape(n_blocks + 1, MOE_BLOCK)
    dst = jnp.concatenate([dst, n_pad + jnp.arange(MOE_BLOCK, dtype=I32)]).reshape(n_blocks + 1, MOE_BLOCK)
    meta = jnp.concatenate([tok, dst], axis=1).reshape(-1)
    block_row0 = jnp.arange(n_blocks, dtype=I32) * MOE_BLOCK
    bexp = jnp.minimum(jnp.sum((padded_end[None, :] <= block_row0[:, None]).astype(I32), axis=1),
                       N_EXPERTS - 1).astype(I32)
    nused = (padded_end[-1] // MOE_BLOCK).astype(I32).reshape(1)
    return bexp, nused, meta.astype(I32), n_pad


def _final_kernel(u_ref, y_ref, gw_ref, x1_ref, wg_ref, wu_ref, wd_ref, gate_ref, lng_ref, lnb_ref, o_ref):
    u = u_ref[...].astype(BF16)
    hg = jnp.dot(u, wg_ref[...], preferred_element_type=F32)
    hu = jnp.dot(u, wu_ref[...], preferred_element_type=F32)
    act = (hg * jax.nn.sigmoid(hg) * hu).astype(BF16)
    ffn = jnp.dot(act, wd_ref[...], preferred_element_type=F32)
    gw = gw_ref[...]
    for k in range(TOP_K):
        ffn = ffn + y_ref[:, k, :] * gw[:, k:k + 1]
    o_ref[...] = _layer_norm_rows(ALPHA * x1_ref[...] + gate_ref[...] * ffn, lng_ref[...], lnb_ref[...])


def _final(u2, y_flat, gw, x1, ws_gate_bf, ws_up_bf, ws_down_bf, gate2, ln_g, ln_b, S):
    T, D = u2.shape
    E = ws_gate_bf.shape[1]
    per_b = S // FINAL_TM
    assert Y_TILE == TOP_K
    row = lambda w: pl.BlockSpec((FINAL_TM, w), lambda i: (i, 0))
    const = lambda shape: pl.BlockSpec(shape, lambda i: (0,) * len(shape))
    return pl.pallas_call(
        _final_kernel,
        grid=(T // FINAL_TM,),
        in_specs=[row(D), pl.BlockSpec((FINAL_TM, TOP_K, D), lambda i: (i, 0, 0)), row(TOP_K), row(D),
                  const((D, E)), const((D, E)), const((E, D)),
                  pl.BlockSpec((None, 1, D), lambda i: (i // per_b, 0, 0)), const((1, D)), const((1, D))],
        out_specs=row(D),
        out_shape=jax.ShapeDtypeStruct((T, D), F32),
        compiler_params=_cparams(("parallel",)),
        name="shared_combine_ln2",
    )(u2, y_flat, gw, x1, ws_gate_bf, ws_up_bf, ws_down_bf, gate2, ln_g.reshape(1, D), ln_b.reshape(1, D))


def kernel(x, c, positions, w_ada, b_ada, w_in, lb_logits, attn_norm_g, hgrn_norm_g, w_out, ln1_g, ln1_b,
           w_router, router_bias, expert_w_gate, expert_w_up, expert_w_down, shared_w_gate, shared_w_up,
           shared_w_down, ln2_g, ln2_b):
    B, S, D = x.shape
    T = B * S
    layer = 0
    lower_bounds = jnp.cumsum(jax.nn.softmax(lb_logits.astype(F32), axis=1), axis=1)

    mod = _ada_mod(c, w_ada[layer], b_ada[layer])
    shift1, scale1, gate1, shift2, scale2, gate2 = [m.reshape(B, 1, D) for m in jnp.split(mod, 6, axis=-1)]
    cos_t, sin_t = _rope_tables(positions)
    x2 = x.reshape(T, D)

    proj, qkv4, qkv16 = _in_proj(x2, scale1, shift1, w_in[layer].astype(BF16), cos_t, sin_t, B, S)
    branches = [_attn_branch(qkv, d, B, S)
                for qkv, d in zip((proj.reshape(B, S, -1), qkv4, qkv16), DILATIONS)]
    rec = _hgrn(proj, lower_bounds[0, layer], lower_bounds[1, layer], hgrn_norm_g[layer], B, S)
    x1, u2, u_rows = _mix([o for o, _ in branches], [l for _, l in branches], rec, x2, attn_norm_g[layer],
                  w_out[layer].astype(BF16), gate1, ln1_g[layer], ln1_b[layer], scale2, shift2, S)

    idx, gw, rank, counts = _route(u2, w_router[layer], router_bias[layer])
    bexp, nused, meta, n_pad = _dispatch_plan(idx, rank, counts, T)
    y_flat = _moe(u_rows, bexp, nused, meta, expert_w_gate[layer], expert_w_up[layer], expert_w_down[layer], n_pad)
    out = _final(u2, y_flat, gw, x1, shared_w_gate[layer].astype(BF16), shared_w_up[layer].astype(BF16),
                 shared_w_down[layer].astype(BF16), gate2, ln2_g[layer], ln2_b[layer], S)
    return out.reshape(B, S, D)
```

```python
import functools

import jax
import jax.numpy as jnp
from jax import lax
from jax.experimental import pallas as pl
from jax.experimental.pallas import tpu as pltpu

F32 = jnp.float32
BF16 = jnp.bfloat16
I32 = jnp.int32
HIGHEST = lax.Precision.HIGHEST

D_MODEL = 2048
ATTN_HEADS = 16
ATTN_HEAD_DIM = 64
ATTN_WIDTH = ATTN_HEADS * ATTN_HEAD_DIM
HGRN_HEADS = 8
HGRN_HEAD_DIM = 128
HGRN_WIDTH = HGRN_HEADS * HGRN_HEAD_DIM
IN_PROJ_WIDTH = 3 * ATTN_WIDTH + 5 * HGRN_WIDTH
DILATIONS = (1, 4, 16)
ATTN_HALF = 64
ROPE_THETA = 10000.0
N_EXPERTS = 256
N_EXPERT_GROUPS = 8
GROUP_SIZE = N_EXPERTS // N_EXPERT_GROUPS
TOPK_GROUPS = 4
TOP_K = 8
EXPERT_DIM = 512
ROUTED_SCALE = 2.5
DEPTH = 1
ALPHA = (2 * DEPTH) ** 0.25
LN_EPS = 1e-5
RMS_EPS = 1e-6
NEG_INF = -1e30
LOG2E = 1.4426950408889634

LANES = 128
VMEM_LIMIT = 56 * 1024 * 1024

ADA_TN = 1024
ROPE_TM = 2048
INPROJ_TM = 512
INPROJ_TN = 1024
INPROJ_NCHUNK = 256
ATTN_TQ = 128
ATTN_WK = 256
ATTN_UNROLL = 4
HGRN_CHUNK = 64
HGRN_SUB = 16
MIX_TM = 256
ROUTE_TM = 256
MOE_BLOCK = 128
ROW_SUB = D_MODEL // LANES
ROW_PITCH = 24
FINAL_TM = 128


def _cparams(sem):
    return pltpu.CompilerParams(dimension_semantics=sem, vmem_limit_bytes=VMEM_LIMIT)


def _ada_kernel(c_ref, w_ref, b_ref, o_ref):
    c = c_ref[...]
    sc = c * jax.nn.sigmoid(c)
    o_ref[...] = jnp.dot(sc, w_ref[...], preferred_element_type=F32, precision=HIGHEST) + b_ref[...]


def _ada_mod(c, w_ada, b_ada):
    B, D = c.shape
    N = w_ada.shape[1]
    c8 = jnp.zeros((8, D), F32).at[:B].set(c)
    out = pl.pallas_call(
        _ada_kernel,
        grid=(N // ADA_TN,),
        in_specs=[pl.BlockSpec((8, D), lambda j: (0, 0)),
                  pl.BlockSpec((D, ADA_TN), lambda j: (0, j)),
                  pl.BlockSpec((1, ADA_TN), lambda j: (0, j))],
        out_specs=pl.BlockSpec((8, ADA_TN), lambda j: (0, j)),
        out_shape=jax.ShapeDtypeStruct((8, N), F32),
        compiler_params=_cparams(("parallel",)),
        name="ada_mod",
    )(c8, w_ada, b_ada.reshape(1, N))
    return out[:B]


def _rope_kernel(pos_ref, invf_ref, sign_ref, cos_ref, sin_ref):
    ang = pos_ref[...].astype(F32) * invf_ref[...]
    cos_ref[...] = jnp.cos(ang)
    sin_ref[...] = jnp.sin(ang) * sign_ref[...]


def _rope_tables(positions):
    T = positions.size
    half = ATTN_HEAD_DIM // 2
    inv_freq = ROPE_THETA ** (-jnp.arange(half, dtype=F32) / half)
    lane = jnp.arange(LANES)
    invf = inv_freq[lane % half].reshape(1, LANES)
    sign = jnp.where((lane % ATTN_HEAD_DIM) < half, -1.0, 1.0).astype(F32).reshape(1, LANES)
    return pl.pallas_call(
        _rope_kernel,
        grid=(T // ROPE_TM,),
        in_specs=[pl.BlockSpec((ROPE_TM, 1), lambda i: (i, 0)),
                  pl.BlockSpec((1, LANES), lambda i: (0, 0)),
                  pl.BlockSpec((1, LANES), lambda i: (0, 0))],
        out_specs=[pl.BlockSpec((ROPE_TM, LANES), lambda i: (i, 0)),
                   pl.BlockSpec((ROPE_TM, LANES), lambda i: (i, 0))],
        out_shape=[jax.ShapeDtypeStruct((T, LANES), F32)] * 2,
        compiler_params=_cparams(("parallel",)),
        name="rope_tables",
    )(positions.reshape(T, 1), invf, sign)


def _inproj_kernel(x_ref, sc_ref, sh_ref, w_ref, cos_ref, sin_ref, o_ref, o4_ref, o16_ref, stage_ref):
    j = pl.program_id(1)
    u = (x_ref[...] * (1.0 + sc_ref[...]) + sh_ref[...]).astype(BF16)
    per_chunk = INPROJ_NCHUNK // LANES

    def emit(cb):
        o_ref[:, cb * LANES:(cb + 1) * LANES] = stage_ref[cb].astype(BF16)
        for d, od_ref in ((DILATIONS[1], o4_ref), (DILATIONS[2], o16_ref)):
            rows = INPROJ_TM // d
            for r in range(d):
                c0 = r * INPROJ_TN + cb * LANES
                od_ref[:, c0:c0 + LANES] = stage_ref.at[cb][pl.ds(r, rows, stride=d), :].astype(BF16)

    @pl.when(j < 2)
    def _():
        qscale = jnp.where(j == 0, ATTN_HEAD_DIM ** -0.5, 1.0).astype(F32)
        cos = cos_ref[...] * qscale
        sin = sin_ref[...] * qscale
        lane = lax.broadcasted_iota(I32, cos.shape, 1)
        first = (lane % ATTN_HEAD_DIM) < (ATTN_HEAD_DIM // 2)
        for nb in range(INPROJ_TN // INPROJ_NCHUNK):
            acc = jnp.dot(u, w_ref[:, nb * INPROJ_NCHUNK:(nb + 1) * INPROJ_NCHUNK], preferred_element_type=F32)
            for h in range(per_chunk):
                a = acc[:, h * LANES:(h + 1) * LANES]
                partner = jnp.where(first, pltpu.roll(a, LANES - ATTN_HEAD_DIM // 2, 1),
                                    pltpu.roll(a, ATTN_HEAD_DIM // 2, 1))
                stage_ref[nb * per_chunk + h] = a * cos + partner * sin
                emit(nb * per_chunk + h)

    @pl.when(j == 2)
    def _():
        for nb in range(INPROJ_TN // INPROJ_NCHUNK):
            acc = jnp.dot(u, w_ref[:, nb * INPROJ_NCHUNK:(nb + 1) * INPROJ_NCHUNK], preferred_element_type=F32)
            for h in range(per_chunk):
                stage_ref[nb * per_chunk + h] = acc[:, h * LANES:(h + 1) * LANES]
                emit(nb * per_chunk + h)

    @pl.when(j >= 3)
    def _():
        o_ref[...] = jnp.dot(u, w_ref[...], preferred_element_type=F32).astype(BF16)


def _in_proj(x2, scale1, shift1, w_in_bf, cos_t, sin_t, B, S):
    T, D = x2.shape
    P = w_in_bf.shape[1]
    per_b = S // INPROJ_TM
    assert INPROJ_TN == ATTN_WIDTH
    d4, d16 = DILATIONS[1], DILATIONS[2]
    strided_spec = lambda d: pl.BlockSpec((None, INPROJ_TM // d, d * INPROJ_TN),
                                          lambda i, j: (i // per_b, i % per_b, jnp.minimum(j, 2)))
    strided_shape = lambda d: jax.ShapeDtypeStruct((B, S // d, 3 * d * ATTN_WIDTH), BF16)
    return pl.pallas_call(
        _inproj_kernel,
        grid=(T // INPROJ_TM, P // INPROJ_TN),
        in_specs=[pl.BlockSpec((INPROJ_TM, D), lambda i, j: (i, 0)),
                  pl.BlockSpec((None, 1, D), lambda i, j: (i // per_b, 0, 0)),
                  pl.BlockSpec((None, 1, D), lambda i, j: (i // per_b, 0, 0)),
                  pl.BlockSpec((D, INPROJ_TN), lambda i, j: (0, j)),
                  pl.BlockSpec((INPROJ_TM, LANES), lambda i, j: (i, 0)),
                  pl.BlockSpec((INPROJ_TM, LANES), lambda i, j: (i, 0))],
        out_specs=[pl.BlockSpec((INPROJ_TM, INPROJ_TN), lambda i, j: (i, j)), strided_spec(d4), strided_spec(d16)],
        out_shape=[jax.ShapeDtypeStruct((T, P), BF16), strided_shape(d4), strided_shape(d16)],
        scratch_shapes=[pltpu.VMEM((INPROJ_TN // LANES, INPROJ_TM, LANES), F32)],
        compiler_params=_cparams(("parallel", "arbitrary")),
        name="in_proj",
    )(x2, scale1, shift1, w_in_bf, cos_t, sin_t)


def _attn_kernel(q_ref, k_ref, v_ref, o_ref, lse_ref, *, L):
    nq = L // ATTN_TQ
    lane = lax.broadcasted_iota(I32, (1, LANES), 1)
    head0 = lane < ATTN_HEAD_DIM
    rel = (lax.broadcasted_iota(I32, (ATTN_TQ, ATTN_WK), 1)
           - lax.broadcasted_iota(I32, (ATTN_TQ, ATTN_WK), 0))

    def body(i, carry):
        q0 = pl.multiple_of(i * ATTN_TQ, ATTN_TQ)
        ks = pl.multiple_of(jnp.clip(i * ATTN_TQ - ATTN_HALF, 0, L - ATTN_WK), ATTN_HALF)
        q = q_ref[pl.ds(q0, ATTN_TQ), :]
        k = k_ref[pl.ds(ks, ATTN_WK), :]
        v = v_ref[pl.ds(ks, ATTN_WK), :]
        mask = jnp.abs(rel + (ks - q0)) <= ATTN_HALF
        outs, lses = [], []
        for hmask in (head0, jnp.logical_not(head0)):
            qh = jnp.where(hmask, q, jnp.zeros_like(q))
            s = lax.dot_general(qh, k, (((1,), (1,)), ((), ())), preferred_element_type=F32)
            s = jnp.where(mask, s, NEG_INF)
            m = jnp.max(s, axis=-1, keepdims=True)
            p = jnp.exp(s - m)
            l = jnp.sum(p, axis=-1, keepdims=True)
            outs.append(jnp.dot(p.astype(BF16), v, preferred_element_type=F32) / l)
            lses.append(m + jnp.log(l))
        o_ref[pl.ds(q0, ATTN_TQ), :] = jnp.where(head0, outs[0], outs[1]).astype(BF16)
        lse_ref[pl.ds(q0, ATTN_TQ), :] = jnp.where(head0, lses[0], lses[1])
        return carry

    lax.fori_loop(0, nq, body, 0, unroll=ATTN_UNROLL)


def _attn_branch(qkv, dilation, B, S):
    L = S // dilation
    pv = qkv
    acb = ATTN_WIDTH // LANES
    in_spec = lambda part: pl.BlockSpec((None, L, LANES),
                                        lambda b, r, h: (b, 0, (part * dilation + r) * acb + h))
    out_spec = pl.BlockSpec((None, L, LANES), lambda b, r, h: (b, 0, r * acb + h))
    o, lse = pl.pallas_call(
        functools.partial(_attn_kernel, L=L),
        grid=(B, dilation, acb),
        in_specs=[in_spec(0), in_spec(1), in_spec(2)],
        out_specs=[out_spec, out_spec],
        out_shape=[jax.ShapeDtypeStruct((B, L, dilation * ATTN_WIDTH), BF16),
                   jax.ShapeDtypeStruct((B, L, dilation * ATTN_WIDTH), F32)],
        compiler_params=_cparams(("parallel", "parallel", "parallel")),
        name=f"attn_d{dilation}",
    )(pv, pv, pv)
    return o.reshape(B * S, ATTN_WIDTH), lse.reshape(B * S, ATTN_WIDTH)


def _hgrn_chunk(q, kk, lf, v, state_t, tri, reverse):
    C = HGRN_CHUNK
    SUB = HGRN_SUB
    nsub = C // SUB
    b = jnp.dot(tri, lf, preferred_element_type=F32, precision=HIGHEST) * LOG2E
    col = lax.broadcasted_iota(I32, (SUB, C), 1)
    row = lax.broadcasted_iota(I32, (SUB, C), 0)
    score_rows = []
    for i in range(nsub):
        r0 = i * SUB
        bi = b[r0:r0 + SUB]
        qi = q[r0:r0 + SUB]
        ki = kk[r0:r0 + SUB]
        diag = jnp.zeros((SUB, C), F32)
        for s in range(SUB):
            e = jnp.exp2(bi - bi[s:s + 1])
            colv = jnp.sum(qi * e * ki[s:s + 1], axis=-1, keepdims=True)
            diag = jnp.where(col == r0 + s, colv, diag)
        if reverse:
            keep = (col - r0) >= row
        else:
            keep = (col - r0) <= row
        diag = jnp.where(jnp.logical_and(keep, jnp.logical_and(col >= r0, col < r0 + SUB)), diag, 0.0)
        if reverse:
            has_off = i < nsub - 1
            bref = b[r0 + SUB:r0 + SUB + 1] if has_off else None
            off_mask = col >= r0 + SUB
        else:
            has_off = i > 0
            bref = b[r0 - 1:r0] if has_off else None
            off_mask = col < r0
        if has_off:
            qs = (qi * jnp.exp2(bi - bref)).astype(BF16)
            ks = (kk * jnp.exp2(bref - b)).astype(BF16)
            off = lax.dot_general(qs, ks, (((1,), (1,)), ((), ())), preferred_element_type=F32)
            score_rows.append(jnp.where(off_mask, off, diag))
        else:
            score_rows.append(diag)
    scores = jnp.concatenate(score_rows, axis=0)
    b_edge = b[0:1] if reverse else b[C - 1:C]
    o = jnp.dot(scores.astype(BF16), v.astype(BF16), preferred_element_type=F32)
    qd = (q * jnp.exp2(b)).astype(BF16)
    o = o + lax.dot_general(qd, state_t.astype(BF16), (((1,), (1,)), ((), ())),
                            preferred_element_type=F32)
    kd = (kk * jnp.exp2(b_edge - b)).astype(BF16)
    upd = lax.dot_general(v.astype(BF16), kd, (((0,), (0,)), ((), ())), preferred_element_type=F32)
    state_t = state_t * jnp.exp2(b_edge) + upd
    return o, state_t


def _hgrn_kernel(q_ref, zf_ref, zb_ref, v_ref, g_ref, lbf_ref, lbb_ref, ng_ref, o_ref, acc_ref, *, S):
    C = HGRN_CHUNK
    n_chunks = S // C
    r = lax.broadcasted_iota(I32, (C, C), 0)
    c = lax.broadcasted_iota(I32, (C, C), 1)
    tri_f = (c <= r).astype(F32)
    tri_b = (c >= r).astype(F32)
    lbf = lbf_ref[...]
    lbb = lbb_ref[...]
    acc_ref[...] = jnp.zeros_like(acc_ref)

    def gates(z, lb):
        sg = jax.nn.sigmoid(z)
        return jnp.log(lb + (1.0 - lb) * sg), (1.0 - lb) * (1.0 - sg)

    def body(n, carry):
        st_f, st_b = carry
        rf = pl.multiple_of(n * C, C)
        rb = pl.multiple_of((n_chunks - 1 - n) * C, C)
        lf_f, kk_f = gates(zf_ref[pl.ds(rf, C), :].astype(F32), lbf)
        o_f, st_f = _hgrn_chunk(q_ref[pl.ds(rf, C), :].astype(F32), kk_f, lf_f,
                                v_ref[pl.ds(rf, C), :].astype(F32), st_f, tri_f, False)
        acc_ref[pl.ds(rf, C), :] += o_f
        lf_b, kk_b = gates(zb_ref[pl.ds(rb, C), :].astype(F32), lbb)
        o_b, st_b = _hgrn_chunk(q_ref[pl.ds(rb, C), :].astype(F32), kk_b, lf_b,
                                v_ref[pl.ds(rb, C), :].astype(F32), st_b, tri_b, True)
        acc_ref[pl.ds(rb, C), :] += o_b
        return st_f, st_b

    z0 = jnp.zeros((HGRN_HEAD_DIM, HGRN_HEAD_DIM), F32)
    lax.fori_loop(0, n_chunks, body, (z0, z0))

    ng = ng_ref[...]

    def norm_body(n, carry):
        r0 = pl.multiple_of(n * 512, 512)
        t = acc_ref[pl.ds(r0, 512), :]
        g = g_ref[pl.ds(r0, 512), :].astype(F32)
        y = t * lax.rsqrt(jnp.mean(t * t, axis=-1, keepdims=True) + RMS_EPS) * ng
        o_ref[pl.ds(r0, 512), :] = (y * (g * jax.nn.sigmoid(g))).astype(BF16)
        return carry

    lax.fori_loop(0, S // 512, norm_body, 0)


def _hgrn(proj, lb_fwd, lb_bwd, norm_g, B, S):
    P = proj.shape[1]
    pv = proj.reshape(B, S, P)
    base = 3 * ATTN_WIDTH // LANES
    nh = HGRN_HEADS
    in_spec = lambda k: pl.BlockSpec((None, S, LANES), lambda b, h: (b, 0, base + k * nh + h))
    vec_spec = pl.BlockSpec((None, 1, LANES), lambda b, h: (h, 0, 0))
    out = pl.pallas_call(
        functools.partial(_hgrn_kernel, S=S),
        grid=(B, nh),
        in_specs=[in_spec(0), in_spec(1), in_spec(2), in_spec(3), in_spec(4), vec_spec, vec_spec, vec_spec],
        out_specs=pl.BlockSpec((None, S, LANES), lambda b, h: (b, 0, h)),
        out_shape=jax.ShapeDtypeStruct((B, S, HGRN_WIDTH), BF16),
        scratch_shapes=[pltpu.VMEM((S, LANES), F32)],
        compiler_params=_cparams(("parallel", "parallel")),
        name="hgrn2",
    )(pv, pv, pv, pv, pv, lb_fwd.reshape(nh, 1, LANES), lb_bwd.reshape(nh, 1, LANES),
      norm_g.reshape(nh, 1, LANES))
    return out.reshape(B * S, HGRN_WIDTH)


def _layer_norm_rows(y, g, b):
    mu = jnp.mean(y, axis=-1, keepdims=True)
    d = y - mu
    var = jnp.mean(d * d, axis=-1, keepdims=True)
    return d * lax.rsqrt(var + LN_EPS) * g + b


def _mix_kernel(o1_ref, o2_ref, o3_ref, l1_ref, l2_ref, l3_ref, rec_ref, x_ref, grp_ref, ag_ref, w_ref,
                gate_ref, lng_ref, lnb_ref, sc_ref, sh_ref, x1_ref, u2_ref, urows_ref):
    l1, l2, l3 = l1_ref[...], l2_ref[...], l3_ref[...]
    m = jnp.maximum(jnp.maximum(l1, l2), l3)
    e1, e2, e3 = jnp.exp(l1 - m), jnp.exp(l2 - m), jnp.exp(l3 - m)
    attn = (e1 * o1_ref[...].astype(F32) + e2 * o2_ref[...].astype(F32)
            + e3 * o3_ref[...].astype(F32)) / (e1 + e2 + e3)
    ms = jnp.dot((attn * attn).astype(BF16), grp_ref[...], preferred_element_type=F32)
    normed = attn * lax.rsqrt(ms + RMS_EPS) * ag_ref[...]
    mixed = jnp.concatenate([normed.astype(BF16), rec_ref[...]], axis=-1)
    mix = jnp.dot(mixed, w_ref[...], preferred_element_type=F32)
    x1 = _layer_norm_rows(ALPHA * x_ref[...] + gate_ref[...] * mix, lng_ref[...], lnb_ref[...])
    x1_ref[...] = x1
    u2 = x1 * (1.0 + sc_ref[...]) + sh_ref[...]
    u2_ref[...] = u2
    for cb in range(ROW_SUB):
        urows_ref[pl.ds(cb, MIX_TM, stride=ROW_SUB), :] = u2[:, cb * LANES:(cb + 1) * LANES]


def _mix(o_branches, lse_branches, rec, x2, attn_norm_g, w_out_bf, gate1, ln_g, ln_b, scale2, shift2, S):
    T, D = x2.shape
    per_b = S // MIX_TM
    head = jnp.arange(ATTN_WIDTH) // ATTN_HEAD_DIM
    grp = jnp.where(head[:, None] == head[None, :], 1.0 / ATTN_HEAD_DIM, 0.0).astype(BF16)
    row = lambda w: pl.BlockSpec((MIX_TM, w), lambda i: (i, 0))
    const = lambda shape: pl.BlockSpec(shape, lambda i: (0,) * len(shape))
    per_batch = pl.BlockSpec((None, 1, D), lambda i: (i // per_b, 0, 0))
    return pl.pallas_call(
        _mix_kernel,
        grid=(T // MIX_TM,),
        in_specs=[row(ATTN_WIDTH)] * 6 + [row(HGRN_WIDTH), row(D), const((ATTN_WIDTH, ATTN_WIDTH)),
                  const((1, ATTN_WIDTH)), const((D, D)), per_batch, const((1, D)), const((1, D)),
                  per_batch, per_batch],
        out_specs=[row(D), row(D), pl.BlockSpec((MIX_TM * ROW_SUB, LANES), lambda i: (i, 0))],
        out_shape=[jax.ShapeDtypeStruct((T, D), F32)] * 2 + [jax.ShapeDtypeStruct((T * ROW_SUB, LANES), F32)],
        compiler_params=_cparams(("parallel",)),
        name="mix_out_ln1",
    )(*o_branches, *lse_branches, rec, x2, grp, attn_norm_g.reshape(1, -1), w_out_bf, gate1,
      ln_g.reshape(1, D), ln_b.reshape(1, D), scale2, shift2)


def _route_kernel(u_ref, w_ref, bias_ref, idx_ref, gw_ref, rank_ref, cnt_ref, run_ref):
    i = pl.program_id(0)

    @pl.when(i == 0)
    def _():
        run_ref[...] = jnp.zeros_like(run_ref)

    tm = ROUTE_TM
    logits = jnp.dot(u_ref[...], w_ref[...], preferred_element_type=F32, precision=HIGHEST)
    scores = jax.nn.sigmoid(logits)
    sel = scores + bias_ref[...]
    lane = lax.broadcasted_iota(I32, (tm, N_EXPERTS), 1)
    lane_f = lane.astype(F32)
    group = lane // GROUP_SIZE
    neg = jnp.float32(-jnp.inf)

    def first_argmax(vals):
        mx = jnp.max(vals, axis=-1, keepdims=True)
        idx = jnp.min(jnp.where(vals == mx, lane_f, float(N_EXPERTS)), axis=-1, keepdims=True)
        return mx, idx

    gscore = []
    for g in range(N_EXPERT_GROUPS):
        vals = jnp.where(group == g, sel, neg)
        m1, i1 = first_argmax(vals)
        m2 = jnp.max(jnp.where(lane_f == i1, neg, vals), axis=-1, keepdims=True)
        gscore.append(m1 + m2)
    keep_f = jnp.zeros((tm, N_EXPERTS), F32)
    for g in range(N_EXPERT_GROUPS):
        beaten = jnp.zeros((tm, 1), F32)
        for h in range(N_EXPERT_GROUPS):
            if h == g:
                continue
            ahead = (gscore[h] >= gscore[g]) if h < g else (gscore[h] > gscore[g])
            beaten = beaten + jnp.where(ahead, 1.0, 0.0)
        gkeep = jnp.where(beaten < TOPK_GROUPS, 1.0, 0.0)
        keep_f = jnp.where(group == g, gkeep, keep_f)
    vals = jnp.where(keep_f > 0.5, sel, neg)

    lane_o = lax.broadcasted_iota(I32, (tm, LANES), 1)
    idx_out = jnp.zeros((tm, LANES), F32)
    gw_out = jnp.zeros((tm, LANES), F32)
    chosen = jnp.zeros((tm, N_EXPERTS), F32)
    picks = []
    wsum = jnp.zeros((tm, 1), F32)
    for k in range(TOP_K):
        _, ik = first_argmax(vals)
        hit = lane_f == ik
        sk = jnp.sum(jnp.where(hit, scores, 0.0), axis=-1, keepdims=True)
        vals = jnp.where(hit, neg, vals)
        chosen = jnp.where(hit, 1.0, chosen)
        picks.append((ik, sk))
        wsum = wsum + sk
        idx_out = jnp.where(lane_o == k, ik, idx_out)
    for k, (ik, sk) in enumerate(picks):
        gw_out = jnp.where(lane_o == k, sk / wsum * ROUTED_SCALE, gw_out)

    r = lax.broadcasted_iota(I32, (tm, tm), 0)
    c = lax.broadcasted_iota(I32, (tm, tm), 1)
    strict_lower = jnp.where(c < r, 1.0, 0.0).astype(BF16)
    before = jnp.dot(strict_lower, chosen.astype(BF16), preferred_element_type=F32) + run_ref[...]
    rank_out = jnp.zeros((tm, LANES), F32)
    for k, (ik, sk) in enumerate(picks):
        rk = jnp.sum(jnp.where(lane_f == ik, before, 0.0), axis=-1, keepdims=True)
        rank_out = jnp.where(lane_o == k, rk, rank_out)
    run_ref[...] = run_ref[...] + jnp.sum(chosen, axis=0, keepdims=True)

    idx_ref[...] = idx_out.astype(I32)
    gw_ref[...] = gw_out
    rank_ref[...] = rank_out.astype(I32)
    cnt_ref[...] = run_ref[...]


def _route(u2, w_router, router_bias):
    T, D = u2.shape
    row = lambda w: pl.BlockSpec((ROUTE_TM, w), lambda i: (i, 0))
    idx, gw, rank, cnt = pl.pallas_call(
        _route_kernel,
        grid=(T // ROUTE_TM,),
        in_specs=[row(D), pl.BlockSpec((D, N_EXPERTS), lambda i: (0, 0)),
                  pl.BlockSpec((1, N_EXPERTS), lambda i: (0, 0))],
        out_specs=[row(LANES), row(LANES), row(LANES), pl.BlockSpec((1, N_EXPERTS), lambda i: (0, 0))],
        out_shape=[jax.ShapeDtypeStruct((T, LANES), I32), jax.ShapeDtypeStruct((T, LANES), F32),
                   jax.ShapeDtypeStruct((T, LANES), I32), jax.ShapeDtypeStruct((1, N_EXPERTS), F32)],
        scratch_shapes=[pltpu.VMEM((1, N_EXPERTS), F32)],
        compiler_params=_cparams(("arbitrary",)),
        name="router",
    )(u2, w_router, router_bias.reshape(1, N_EXPERTS))
    return idx, gw, rank, cnt[0].astype(I32)


def _slot_kernel(idx_ref, rank_ref, start_ref, dest_ref):
    tm = ROUTE_TM
    lane_e = lax.broadcasted_iota(I32, (tm, N_EXPERTS), 1)
    lane_o = lax.broadcasted_iota(I32, (tm, LANES), 1)
    idx = idx_ref[...]
    start = start_ref[...]
    base = jnp.zeros((tm, LANES), F32)
    for k in range(TOP_K):
        sk = jnp.sum(jnp.where(lane_e == idx[:, k:k + 1], start, 0.0), axis=-1, keepdims=True)
        base = jnp.where(lane_o == k, sk, base)
    dest_ref[...] = base.astype(I32) + rank_ref[...]


def _slots(idx, rank, start):
    T = idx.shape[0]
    row = pl.BlockSpec((ROUTE_TM, LANES), lambda i: (i, 0))
    dest = pl.pallas_call(
        _slot_kernel,
        grid=(T // ROUTE_TM,),
        in_specs=[row, row, pl.BlockSpec((1, N_EXPERTS), lambda i: (0, 0))],
        out_specs=row,
        out_shape=jax.ShapeDtypeStruct((T, LANES), I32),
        compiler_params=_cparams(("parallel",)),
        name="slots",
    )(idx, rank, start.astype(F32).reshape(1, N_EXPERTS))
    return dest[:, :TOP_K]


MOE_META_SLOTS = 4
Y_TILE = 8
Y_TILE_SHIFT = Y_TILE.bit_length() - 1


def _moe_kernel(bexp_ref, wnext_ref, nused_ref, meta_hbm, u_hbm, wg_hbm, wu_hbm, wd_hbm, y_hbm,
                meta_smem, xb0, xb1, yb0, yb1, wg_st, wu_st, wd_st, wg_bf, wu_bf, wd_bf,
                sem_meta, sem_g, sem_s, sem_w):
    i = pl.program_id(0)
    n_blocks = pl.num_programs(0)
    nused = nused_ref[0]
    last = nused - 1

    def meta_copy(blk, sl):
        row = pl.ds(pl.multiple_of(blk * (2 * MOE_BLOCK), 2 * MOE_BLOCK), 2 * MOE_BLOCK)
        return pltpu.make_async_copy(meta_hbm.at[row], meta_smem.at[sl], sem_meta.at[sl])

    def weight_copies(e):
        return (pltpu.make_async_copy(wg_hbm.at[e], wg_st, sem_w.at[0]),
                pltpu.make_async_copy(wu_hbm.at[e], wu_st, sem_w.at[1]),
                pltpu.make_async_copy(wd_hbm.at[e], wd_st, sem_w.at[2]))

    def gather_rows(sl, xdst, sem):
        for j in range(MOE_BLOCK):
            row0 = pl.multiple_of(meta_smem[sl, j] * ROW_SUB, ROW_SUB)
            pltpu.make_async_copy(u_hbm.at[pl.ds(row0, ROW_SUB)], xdst.at[pl.ds(j * ROW_PITCH, ROW_SUB)], sem).start()

    def scatter_rows(sl, ysrc, sem):
        for j in range(MOE_BLOCK):
            dst = meta_smem[sl, MOE_BLOCK + j]
            pltpu.make_async_copy(ysrc.at[pl.ds(j, 1)],
                                  y_hbm.at[dst >> Y_TILE_SHIFT, pl.ds(dst & (Y_TILE - 1), 1)], sem).start()

    def wait_gather(xdst, sem):
        n = MOE_BLOCK * ROW_SUB
        pltpu.make_async_copy(u_hbm.at[pl.ds(0, n)], xdst.at[pl.ds(0, n)], sem).wait()

    def wait_rows(buf, sem):
        other = yb1 if buf is yb0 else yb0
        pltpu.make_async_copy(other, buf, sem).wait()

    @pl.when(i >= nused)
    def _():
        yb0[...] = jnp.zeros_like(yb0)
        tiles = MOE_BLOCK // Y_TILE
        for a in range(tiles):
            pltpu.make_async_copy(yb0.at[pl.ds(a * Y_TILE, Y_TILE)], y_hbm.at[i * tiles + a], sem_s.at[0]).start()
        wait_rows(yb0, sem_s.at[0])

    def step(p):
        xcur, xnxt = (xb0, xb1) if p == 0 else (xb1, xb0)
        ycur, yprv = (yb0, yb1) if p == 0 else (yb1, yb0)
        s_cur = i % MOE_META_SLOTS
        s_nxt = (i + 1) % MOE_META_SLOTS
        s_nn = (i + 2) % MOE_META_SLOTS
        s_prv = (i + 3) % MOE_META_SLOTS
        nxt = jnp.minimum(i + 1, last)
        nn = jnp.minimum(i + 2, last)

        @pl.when(i == 0)
        def _():
            first = meta_copy(0, 0)
            first.start()
            first.wait()
            dump = meta_copy(n_blocks, MOE_META_SLOTS - 1)
            dump.start()
            dump.wait()
            meta_copy(nxt, 1).start()
            yprv[...] = jnp.zeros_like(yprv)
            gather_rows(0, xcur, sem_g.at[p])
            for cp in weight_copies(bexp_ref[0]):
                cp.start()

        meta_copy(nxt, s_nxt).wait()
        meta_copy(nn, s_nn).start()

        @pl.when((i == 0) | (bexp_ref[i] != bexp_ref[jnp.maximum(i - 1, 0)]))
        def _():
            for cp in weight_copies(bexp_ref[i]):
                cp.wait()
            wg_bf[...] = wg_st[...].astype(BF16)
            wu_bf[...] = wu_st[...].astype(BF16)
            wd_bf[...] = wd_st[...].astype(BF16)
            nxt_e = wnext_ref[i]

            @pl.when(nxt_e >= 0)
            def _():
                for cp in weight_copies(nxt_e):
                    cp.start()

        wait_gather(xcur, sem_g.at[p])
        gather_rows(s_nxt, xnxt, sem_g.at[1 - p])
        scatter_rows(s_prv, yprv, sem_s.at[1 - p])
        x = jnp.concatenate([xcur[pl.ds(cb, MOE_BLOCK, stride=ROW_PITCH), :].astype(BF16)
                             for cb in range(ROW_SUB)], axis=1)
        hg = jnp.dot(x, wg_bf[...], preferred_element_type=F32)
        hu = jnp.dot(x, wu_bf[...], preferred_element_type=F32)
        act = (hg * jax.nn.sigmoid(hg) * hu).astype(BF16)

        @pl.when(i >= 1)
        def _():
            wait_rows(ycur, sem_s.at[p])

        ycur[...] = jnp.dot(act, wd_bf[...], preferred_element_type=F32)

        @pl.when(i == last)
        def _():
            scatter_rows(s_cur, ycur, sem_s.at[p])
            wait_rows(ycur, sem_s.at[p])
            wait_rows(yprv, sem_s.at[1 - p])
            wait_gather(xnxt, sem_g.at[1 - p])
            meta_copy(nn, s_nn).wait()

    @pl.when((i < nused) & (i % 2 == 0))
    def _():
        step(0)

    @pl.when((i < nused) & (i % 2 == 1))
    def _():
        step(1)


def _moe(u_rows, bexp, wnext, nused, meta, w_gate, w_up, w_down, n_pad):
    D = ROW_SUB * LANES
    n_blocks = n_pad // MOE_BLOCK
    E = EXPERT_DIM
    grid_spec = pltpu.PrefetchScalarGridSpec(
        num_scalar_prefetch=3,
        grid=(n_blocks,),
        in_specs=[pl.BlockSpec(memory_space=pl.ANY)] * 5,
        out_specs=pl.BlockSpec(memory_space=pl.ANY),
        scratch_shapes=[pltpu.SMEM((MOE_META_SLOTS, 2 * MOE_BLOCK), I32),
                        pltpu.VMEM((MOE_BLOCK * ROW_PITCH, LANES), F32),
                        pltpu.VMEM((MOE_BLOCK * ROW_PITCH, LANES), F32),
                        pltpu.VMEM((MOE_BLOCK, D), F32),
                        pltpu.VMEM((MOE_BLOCK, D), F32),
                        pltpu.VMEM((D, E), F32),
                        pltpu.VMEM((D, E), F32),
                        pltpu.VMEM((E, D), F32),
                        pltpu.VMEM((D, E), BF16),
                        pltpu.VMEM((D, E), BF16),
                        pltpu.VMEM((E, D), BF16),
                        pltpu.SemaphoreType.DMA((MOE_META_SLOTS,)),
                        pltpu.SemaphoreType.DMA((2,)),
                        pltpu.SemaphoreType.DMA((2,)),
                        pltpu.SemaphoreType.DMA((3,))])
    return pl.pallas_call(
        _moe_kernel,
        grid_spec=grid_spec,
        out_shape=jax.ShapeDtypeStruct(((n_pad + MOE_BLOCK) // Y_TILE, Y_TILE, D), F32),
        compiler_params=_cparams(("arbitrary",)),
        name="moe_experts",
    )(bexp, wnext, nused, meta, u_rows, w_gate, w_up, w_down)


def _dispatch_plan(idx, rank, counts, T):
    n_assign = T * TOP_K
    n_blocks = -(-(n_assign + N_EXPERTS * (MOE_BLOCK - 1)) // MOE_BLOCK)
    n_pad = n_blocks * MOE_BLOCK
    padded = (counts + MOE_BLOCK - 1) // MOE_BLOCK * MOE_BLOCK
    padded_end = jnp.cumsum(padded)
    start = padded_end - padded
    dest = _slots(idx, rank, start)
    src = jnp.full((n_pad,), -1, I32).at[dest.reshape(-1)].set(jnp.arange(n_assign, dtype=I32))
    is_pad = src < 0
    pad_rank = jnp.cumsum(is_pad.astype(I32)) - 1
    tok = jnp.where(is_pad, 0, src // TOP_K)
    dst = jnp.where(is_pad, n_assign + pad_rank, src)
    tok = jnp.concatenate([tok, jnp.zeros((MOE_BLOCK,), I32)]).reshape(n_blocks + 1, MOE_BLOCK)
    dst = jnp.concatenate([dst, n_pad + jnp.arange(MOE_BLOCK, dtype=I32)]).reshape(n_blocks + 1, MOE_BLOCK)
    meta = jnp.concatenate([tok, dst], axis=1).reshape(-1)
    block_row0 = jnp.arange(n_blocks, dtype=I32) * MOE_BLOCK
    bexp = jnp.minimum(jnp.sum((padded_end[None, :] <= block_row0[:, None]).astype(I32), axis=1),
                       N_EXPERTS - 1).astype(I32)
    nused = (padded_end[-1] // MOE_BLOCK).astype(I32).reshape(1)
    eid = jnp.arange(N_EXPERTS, dtype=I32)
    later = (eid[None, :] > eid[:, None]) & (counts[None, :] > 0)
    next_e = jnp.min(jnp.where(later, eid[None, :], N_EXPERTS), axis=1)
    onehot = (bexp[:, None] == eid[None, :]).astype(I32)
    wnext = jnp.sum(onehot * next_e[None, :], axis=1)
    wnext = jnp.where(wnext < N_EXPERTS, wnext, -1).astype(I32)
    return bexp, wnext, nused, meta.astype(I32), n_pad


def _final_kernel(u_ref, y_ref, gw_ref, x1_ref, wg_ref, wu_ref, wd_ref, gate_ref, lng_ref, lnb_ref, o_ref):
    tm = FINAL_TM
    nrow = tm * TOP_K
    u = u_ref[...].astype(BF16)
    hg = jnp.dot(u, wg_ref[...], preferred_element_type=F32)
    hu = jnp.dot(u, wu_ref[...], preferred_element_type=F32)
    act = (hg * jax.nn.sigmoid(hg) * hu).astype(BF16)
    ffn = jnp.dot(act, wd_ref[...], preferred_element_type=F32)
    expand = jnp.where((lax.broadcasted_iota(I32, (nrow, tm), 0) // TOP_K) == lax.broadcasted_iota(I32, (nrow, tm), 1),
                       1.0, 0.0)
    gw_rows = jnp.dot(expand, gw_ref[...], preferred_element_type=F32, precision=HIGHEST)
    pick = lax.broadcasted_iota(I32, (nrow, LANES), 1) == (lax.broadcasted_iota(I32, (nrow, LANES), 0) % TOP_K)
    wcol = jnp.sum(jnp.where(pick, gw_rows, 0.0), axis=-1, keepdims=True)
    yw = (y_ref[...] * wcol).astype(BF16)
    fold = jnp.where((lax.broadcasted_iota(I32, (tm, nrow), 1) // TOP_K) == lax.broadcasted_iota(I32, (tm, nrow), 0),
                     1.0, 0.0).astype(BF16)
    ffn = ffn + jnp.dot(fold, yw, preferred_element_type=F32)
    o_ref[...] = _layer_norm_rows(ALPHA * x1_ref[...] + gate_ref[...] * ffn, lng_ref[...], lnb_ref[...])


def _final(u2, y_flat, gw, x1, ws_gate_bf, ws_up_bf, ws_down_bf, gate2, ln_g, ln_b, S):
    T, D = u2.shape
    E = ws_gate_bf.shape[1]
    per_b = S // FINAL_TM
    assert Y_TILE == TOP_K
    row = lambda w: pl.BlockSpec((FINAL_TM, w), lambda i: (i, 0))
    const = lambda shape: pl.BlockSpec(shape, lambda i: (0,) * len(shape))
    return pl.pallas_call(
        _final_kernel,
        grid=(T // FINAL_TM,),
        in_specs=[row(D), pl.BlockSpec((FINAL_TM * TOP_K, D), lambda i: (i, 0)), row(LANES), row(D),
                  const((D, E)), const((D, E)), const((E, D)),
                  pl.BlockSpec((None, 1, D), lambda i: (i // per_b, 0, 0)), const((1, D)), const((1, D))],
        out_specs=row(D),
        out_shape=jax.ShapeDtypeStruct((T, D), F32),
        compiler_params=_cparams(("parallel",)),
        name="shared_combine_ln2",
    )(u2, y_flat.reshape(-1, D), gw, x1, ws_gate_bf, ws_up_bf, ws_down_bf, gate2, ln_g.reshape(1, D), ln_b.reshape(1, D))


def kernel(x, c, positions, w_ada, b_ada, w_in, lb_logits, attn_norm_g, hgrn_norm_g, w_out, ln1_g, ln1_b,
           w_router, router_bias, expert_w_gate, expert_w_up, expert_w_down, shared_w_gate, shared_w_up,
           shared_w_down, ln2_g, ln2_b):
    B, S, D = x.shape
    T = B * S
    layer = 0
    lower_bounds = jnp.cumsum(jax.nn.softmax(lb_logits.astype(F32), axis=1), axis=1)

    mod = _ada_mod(c, w_ada[layer], b_ada[layer])
    shift1, scale1, gate1, shift2, scale2, gate2 = [m.reshape(B, 1, D) for m in jnp.split(mod, 6, axis=-1)]
    cos_t, sin_t = _rope_tables(positions)
    x2 = x.reshape(T, D)

    proj, qkv4, qkv16 = _in_proj(x2, scale1, shift1, w_in[layer].astype(BF16), cos_t, sin_t, B, S)
    branches = [_attn_branch(qkv, d, B, S)
                for qkv, d in zip((proj.reshape(B, S, -1), qkv4, qkv16), DILATIONS)]
    rec = _hgrn(proj, lower_bounds[0, layer], lower_bounds[1, layer], hgrn_norm_g[layer], B, S)
    x1, u2, u_rows = _mix([o for o, _ in branches], [l for _, l in branches], rec, x2, attn_norm_g[layer],
                  w_out[layer].astype(BF16), gate1, ln1_g[layer], ln1_b[layer], scale2, shift2, S)

    idx, gw, rank, counts = _route(u2, w_router[layer], router_bias[layer])
    bexp, wnext, nused, meta, n_pad = _dispatch_plan(idx, rank, counts, T)
    y_flat = _moe(u_rows, bexp, wnext, nused, meta, expert_w_gate[layer], expert_w_up[layer], expert_w_down[layer], n_pad)
    out = _final(u2, y_flat, gw, x1, shared_w_gate[layer].astype(BF16), shared_w_up[layer].astype(BF16),
                 shared_w_down[layer].astype(BF16), gate2, ln2_g[layer], ln2_b[layer], S)
    return out.reshape(B, S, D)
```

```python
import functools

import jax
import jax.numpy as jnp
from jax import lax
from jax.experimental import pallas as pl
from jax.experimental.pallas import tpu as pltpu

F32 = jnp.float32
BF16 = jnp.bfloat16
I32 = jnp.int32
HIGHEST = lax.Precision.HIGHEST

D_MODEL = 2048
ATTN_HEADS = 16
ATTN_HEAD_DIM = 64
ATTN_WIDTH = ATTN_HEADS * ATTN_HEAD_DIM
HGRN_HEADS = 8
HGRN_HEAD_DIM = 128
HGRN_WIDTH = HGRN_HEADS * HGRN_HEAD_DIM
IN_PROJ_WIDTH = 3 * ATTN_WIDTH + 5 * HGRN_WIDTH
DILATIONS = (1, 4, 16)
ATTN_HALF = 64
ROPE_THETA = 10000.0
N_EXPERTS = 256
N_EXPERT_GROUPS = 8
GROUP_SIZE = N_EXPERTS // N_EXPERT_GROUPS
TOPK_GROUPS = 4
TOP_K = 8
EXPERT_DIM = 512
ROUTED_SCALE = 2.5
DEPTH = 1
ALPHA = (2 * DEPTH) ** 0.25
LN_EPS = 1e-5
RMS_EPS = 1e-6
NEG_INF = -1e30
LOG2E = 1.4426950408889634

LANES = 128
VMEM_LIMIT = 56 * 1024 * 1024

ADA_TN = 1024
ROPE_TM = 2048
INPROJ_TM = 512
INPROJ_TN = 1024
ATTN_TQ = 128
ATTN_WK = 256
ATTN_UNROLL = 4
HGRN_CHUNK = 64
HGRN_SUB = 16
MIX_TM = 256
ROUTE_TM = 256
MOE_BLOCK = 128
ROW_SUB = D_MODEL // LANES
ROW_PITCH = 24
FINAL_TM = 128


def _cparams(sem):
    return pltpu.CompilerParams(dimension_semantics=sem, vmem_limit_bytes=VMEM_LIMIT)


def _ada_kernel(c_ref, w_ref, b_ref, o_ref):
    c = c_ref[...]
    sc = c * jax.nn.sigmoid(c)
    o_ref[...] = jnp.dot(sc, w_ref[...], preferred_element_type=F32, precision=HIGHEST) + b_ref[...]


def _ada_mod(c, w_ada, b_ada):
    B, D = c.shape
    N = w_ada.shape[1]
    c8 = jnp.zeros((8, D), F32).at[:B].set(c)
    out = pl.pallas_call(
        _ada_kernel,
        grid=(N // ADA_TN,),
        in_specs=[pl.BlockSpec((8, D), lambda j: (0, 0)),
                  pl.BlockSpec((D, ADA_TN), lambda j: (0, j)),
                  pl.BlockSpec((1, ADA_TN), lambda j: (0, j))],
        out_specs=pl.BlockSpec((8, ADA_TN), lambda j: (0, j)),
        out_shape=jax.ShapeDtypeStruct((8, N), F32),
        compiler_params=_cparams(("parallel",)),
        name="ada_mod",
    )(c8, w_ada, b_ada.reshape(1, N))
    return out[:B]


def _rope_kernel(pos_ref, invf_ref, sign_ref, cos_ref, sin_ref):
    ang = pos_ref[...].astype(F32) * invf_ref[...]
    cos_ref[...] = jnp.cos(ang)
    sin_ref[...] = jnp.sin(ang) * sign_ref[...]


def _rope_tables(positions):
    T = positions.size
    half = ATTN_HEAD_DIM // 2
    inv_freq = ROPE_THETA ** (-jnp.arange(half, dtype=F32) / half)
    lane = jnp.arange(LANES)
    invf = inv_freq[lane % half].reshape(1, LANES)
    sign = jnp.where((lane % ATTN_HEAD_DIM) < half, -1.0, 1.0).astype(F32).reshape(1, LANES)
    return pl.pallas_call(
        _rope_kernel,
        grid=(T // ROPE_TM,),
        in_specs=[pl.BlockSpec((ROPE_TM, 1), lambda i: (i, 0)),
                  pl.BlockSpec((1, LANES), lambda i: (0, 0)),
                  pl.BlockSpec((1, LANES), lambda i: (0, 0))],
        out_specs=[pl.BlockSpec((ROPE_TM, LANES), lambda i: (i, 0)),
                   pl.BlockSpec((ROPE_TM, LANES), lambda i: (i, 0))],
        out_shape=[jax.ShapeDtypeStruct((T, LANES), F32)] * 2,
        compiler_params=_cparams(("parallel",)),
        name="rope_tables",
    )(positions.reshape(T, 1), invf, sign)


def _inproj_kernel(x_ref, sc_ref, sh_ref, w_ref, cos_ref, sin_ref, o_ref, o4_ref, o16_ref, stage_ref):
    j = pl.program_id(1)
    u = (x_ref[...] * (1.0 + sc_ref[...]) + sh_ref[...]).astype(BF16)
    acc = jnp.dot(u, w_ref[...], preferred_element_type=F32)

    @pl.when(j < 2)
    def _():
        qscale = jnp.where(j == 0, ATTN_HEAD_DIM ** -0.5, 1.0).astype(F32)
        cos = cos_ref[...] * qscale
        sin = sin_ref[...] * qscale
        lane = lax.broadcasted_iota(I32, cos.shape, 1)
        first = (lane % ATTN_HEAD_DIM) < (ATTN_HEAD_DIM // 2)
        for cb in range(INPROJ_TN // LANES):
            a = acc[:, cb * LANES:(cb + 1) * LANES]
            partner = jnp.where(first, pltpu.roll(a, LANES - ATTN_HEAD_DIM // 2, 1),
                                pltpu.roll(a, ATTN_HEAD_DIM // 2, 1))
            stage_ref[cb] = a * cos + partner * sin

    @pl.when(j == 2)
    def _():
        for cb in range(INPROJ_TN // LANES):
            stage_ref[cb] = acc[:, cb * LANES:(cb + 1) * LANES]

    @pl.when(j < 3)
    def _():
        for cb in range(INPROJ_TN // LANES):
            o_ref[:, cb * LANES:(cb + 1) * LANES] = stage_ref[cb].astype(BF16)
            for d, od_ref in ((DILATIONS[1], o4_ref), (DILATIONS[2], o16_ref)):
                rows = INPROJ_TM // d
                for r in range(d):
                    c0 = r * INPROJ_TN + cb * LANES
                    od_ref[:, c0:c0 + LANES] = stage_ref[cb, pl.ds(r, rows, stride=d), :].astype(BF16)

    @pl.when(j >= 3)
    def _():
        o_ref[...] = acc.astype(BF16)


def _in_proj(x2, scale1, shift1, w_in_bf, cos_t, sin_t, B, S):
    T, D = x2.shape
    P = w_in_bf.shape[1]
    per_b = S // INPROJ_TM
    assert INPROJ_TN == ATTN_WIDTH
    d4, d16 = DILATIONS[1], DILATIONS[2]
    strided_spec = lambda d: pl.BlockSpec((None, INPROJ_TM // d, d * INPROJ_TN),
                                          lambda i, j: (i // per_b, i % per_b, jnp.minimum(j, 2)))
    strided_shape = lambda d: jax.ShapeDtypeStruct((B, S // d, 3 * d * ATTN_WIDTH), BF16)
    return pl.pallas_call(
        _inproj_kernel,
        grid=(T // INPROJ_TM, P // INPROJ_TN),
        in_specs=[pl.BlockSpec((INPROJ_TM, D), lambda i, j: (i, 0)),
                  pl.BlockSpec((None, 1, D), lambda i, j: (i // per_b, 0, 0)),
                  pl.BlockSpec((None, 1, D), lambda i, j: (i // per_b, 0, 0)),
                  pl.BlockSpec((D, INPROJ_TN), lambda i, j: (0, j)),
                  pl.BlockSpec((INPROJ_TM, LANES), lambda i, j: (i, 0)),
                  pl.BlockSpec((INPROJ_TM, LANES), lambda i, j: (i, 0))],
        out_specs=[pl.BlockSpec((INPROJ_TM, INPROJ_TN), lambda i, j: (i, j)), strided_spec(d4), strided_spec(d16)],
        out_shape=[jax.ShapeDtypeStruct((T, P), BF16), strided_shape(d4), strided_shape(d16)],
        scratch_shapes=[pltpu.VMEM((INPROJ_TN // LANES, INPROJ_TM, LANES), F32)],
        compiler_params=_cparams(("parallel", "arbitrary")),
        name="in_proj",
    )(x2, scale1, shift1, w_in_bf, cos_t, sin_t)


def _attn_kernel(q_ref, k_ref, v_ref, o_ref, lse_ref, *, L):
    nq = L // ATTN_TQ
    lane = lax.broadcasted_iota(I32, (1, LANES), 1)
    head0 = lane < ATTN_HEAD_DIM
    rel = (lax.broadcasted_iota(I32, (ATTN_TQ, ATTN_WK), 1)
           - lax.broadcasted_iota(I32, (ATTN_TQ, ATTN_WK), 0))

    def body(i, carry):
        q0 = pl.multiple_of(i * ATTN_TQ, ATTN_TQ)
        ks = pl.multiple_of(jnp.clip(i * ATTN_TQ - ATTN_HALF, 0, L - ATTN_WK), ATTN_HALF)
        q = q_ref[pl.ds(q0, ATTN_TQ), :]
        k = k_ref[pl.ds(ks, ATTN_WK), :]
        v = v_ref[pl.ds(ks, ATTN_WK), :]
        mask = jnp.abs(rel + (ks - q0)) <= ATTN_HALF
        outs, lses = [], []
        for hmask in (head0, jnp.logical_not(head0)):
            qh = jnp.where(hmask, q, jnp.zeros_like(q))
            s = lax.dot_general(qh, k, (((1,), (1,)), ((), ())), preferred_element_type=F32)
            s = jnp.where(mask, s, NEG_INF)
            m = jnp.max(s, axis=-1, keepdims=True)
            p = jnp.exp(s - m)
            l = jnp.sum(p, axis=-1, keepdims=True)
            outs.append(jnp.dot(p.astype(BF16), v, preferred_element_type=F32) / l)
            lses.append(m + jnp.log(l))
        o_ref[pl.ds(q0, ATTN_TQ), :] = jnp.where(head0, outs[0], outs[1]).astype(BF16)
        lse_ref[pl.ds(q0, ATTN_TQ), :] = jnp.where(head0, lses[0], lses[1])
        return carry

    lax.fori_loop(0, nq, body, 0, unroll=ATTN_UNROLL)


def _attn_branch(qkv, dilation, B, S):
    L = S // dilation
    pv = qkv
    acb = ATTN_WIDTH // LANES
    in_spec = lambda part: pl.BlockSpec((None, L, LANES),
                                        lambda b, r, h: (b, 0, (part * dilation + r) * acb + h))
    out_spec = pl.BlockSpec((None, L, LANES), lambda b, r, h: (b, 0, r * acb + h))
    o, lse = pl.pallas_call(
        functools.partial(_attn_kernel, L=L),
        grid=(B, dilation, acb),
        in_specs=[in_spec(0), in_spec(1), in_spec(2)],
        out_specs=[out_spec, out_spec],
        out_shape=[jax.ShapeDtypeStruct((B, L, dilation * ATTN_WIDTH), BF16),
                   jax.ShapeDtypeStruct((B, L, dilation * ATTN_WIDTH), F32)],
        compiler_params=_cparams(("parallel", "parallel", "parallel")),
        name=f"attn_d{dilation}",
    )(pv, pv, pv)
    return o.reshape(B * S, ATTN_WIDTH), lse.reshape(B * S, ATTN_WIDTH)


def _hgrn_chunk(q, kk, lf, v, state_t, tri, reverse):
    C = HGRN_CHUNK
    SUB = HGRN_SUB
    nsub = C // SUB
    b = jnp.dot(tri, lf, preferred_element_type=F32, precision=HIGHEST) * LOG2E
    col = lax.broadcasted_iota(I32, (SUB, C), 1)
    row = lax.broadcasted_iota(I32, (SUB, C), 0)
    score_rows = []
    for i in range(nsub):
        r0 = i * SUB
        bi = b[r0:r0 + SUB]
        qi = q[r0:r0 + SUB]
        ki = kk[r0:r0 + SUB]
        diag = jnp.zeros((SUB, C), F32)
        for s in range(SUB):
            e = jnp.exp2(bi - bi[s:s + 1])
            colv = jnp.sum(qi * e * ki[s:s + 1], axis=-1, keepdims=True)
            diag = jnp.where(col == r0 + s, colv, diag)
        if reverse:
            keep = (col - r0) >= row
        else:
            keep = (col - r0) <= row
        diag = jnp.where(jnp.logical_and(keep, jnp.logical_and(col >= r0, col < r0 + SUB)), diag, 0.0)
        if reverse:
            has_off = i < nsub - 1
            bref = b[r0 + SUB:r0 + SUB + 1] if has_off else None
            off_mask = col >= r0 + SUB
        else:
            has_off = i > 0
            bref = b[r0 - 1:r0] if has_off else None
            off_mask = col < r0
        if has_off:
            qs = (qi * jnp.exp2(bi - bref)).astype(BF16)
            ks = (kk * jnp.exp2(bref - b)).astype(BF16)
            off = lax.dot_general(qs, ks, (((1,), (1,)), ((), ())), preferred_element_type=F32)
            score_rows.append(jnp.where(off_mask, off, diag))
        else:
            score_rows.append(diag)
    scores = jnp.concatenate(score_rows, axis=0)
    b_edge = b[0:1] if reverse else b[C - 1:C]
    o = jnp.dot(scores.astype(BF16), v.astype(BF16), preferred_element_type=F32)
    qd = (q * jnp.exp2(b)).astype(BF16)
    o = o + lax.dot_general(qd, state_t.astype(BF16), (((1,), (1,)), ((), ())),
                            preferred_element_type=F32)
    kd = (kk * jnp.exp2(b_edge - b)).astype(BF16)
    upd = lax.dot_general(v.astype(BF16), kd, (((0,), (0,)), ((), ())), preferred_element_type=F32)
    state_t = state_t * jnp.exp2(b_edge) + upd
    return o, state_t


def _hgrn_kernel(q_ref, zf_ref, zb_ref, v_ref, g_ref, lbf_ref, lbb_ref, ng_ref, o_ref, acc_ref, *, S):
    C = HGRN_CHUNK
    n_chunks = S // C
    r = lax.broadcasted_iota(I32, (C, C), 0)
    c = lax.broadcasted_iota(I32, (C, C), 1)
    tri_f = (c <= r).astype(F32)
    tri_b = (c >= r).astype(F32)
    lbf = lbf_ref[...]
    lbb = lbb_ref[...]
    acc_ref[...] = jnp.zeros_like(acc_ref)

    def gates(z, lb):
        sg = jax.nn.sigmoid(z)
        return jnp.log(lb + (1.0 - lb) * sg), (1.0 - lb) * (1.0 - sg)

    def body(n, carry):
        st_f, st_b = carry
        rf = pl.multiple_of(n * C, C)
        rb = pl.multiple_of((n_chunks - 1 - n) * C, C)
        lf_f, kk_f = gates(zf_ref[pl.ds(rf, C), :].astype(F32), lbf)
        o_f, st_f = _hgrn_chunk(q_ref[pl.ds(rf, C), :].astype(F32), kk_f, lf_f,
                                v_ref[pl.ds(rf, C), :].astype(F32), st_f, tri_f, False)
        acc_ref[pl.ds(rf, C), :] += o_f
        lf_b, kk_b = gates(zb_ref[pl.ds(rb, C), :].astype(F32), lbb)
        o_b, st_b = _hgrn_chunk(q_ref[pl.ds(rb, C), :].astype(F32), kk_b, lf_b,
                                v_ref[pl.ds(rb, C), :].astype(F32), st_b, tri_b, True)
        acc_ref[pl.ds(rb, C), :] += o_b
        return st_f, st_b

    z0 = jnp.zeros((HGRN_HEAD_DIM, HGRN_HEAD_DIM), F32)
    lax.fori_loop(0, n_chunks, body, (z0, z0))

    ng = ng_ref[...]

    def norm_body(n, carry):
        r0 = pl.multiple_of(n * 512, 512)
        t = acc_ref[pl.ds(r0, 512), :]
        g = g_ref[pl.ds(r0, 512), :].astype(F32)
        y = t * lax.rsqrt(jnp.mean(t * t, axis=-1, keepdims=True) + RMS_EPS) * ng
        o_ref[pl.ds(r0, 512), :] = (y * (g * jax.nn.sigmoid(g))).astype(BF16)
        return carry

    lax.fori_loop(0, S // 512, norm_body, 0)


def _hgrn(proj, lb_fwd, lb_bwd, norm_g, B, S):
    P = proj.shape[1]
    pv = proj.reshape(B, S, P)
    base = 3 * ATTN_WIDTH // LANES
    nh = HGRN_HEADS
    in_spec = lambda k: pl.BlockSpec((None, S, LANES), lambda b, h: (b, 0, base + k * nh + h))
    vec_spec = pl.BlockSpec((None, 1, LANES), lambda b, h: (h, 0, 0))
    out = pl.pallas_call(
        functools.partial(_hgrn_kernel, S=S),
        grid=(B, nh),
        in_specs=[in_spec(0), in_spec(1), in_spec(2), in_spec(3), in_spec(4), vec_spec, vec_spec, vec_spec],
        out_specs=pl.BlockSpec((None, S, LANES), lambda b, h: (b, 0, h)),
        out_shape=jax.ShapeDtypeStruct((B, S, HGRN_WIDTH), BF16),
        scratch_shapes=[pltpu.VMEM((S, LANES), F32)],
        compiler_params=_cparams(("parallel", "parallel")),
        name="hgrn2",
    )(pv, pv, pv, pv, pv, lb_fwd.reshape(nh, 1, LANES), lb_bwd.reshape(nh, 1, LANES),
      norm_g.reshape(nh, 1, LANES))
    return out.reshape(B * S, HGRN_WIDTH)


def _layer_norm_rows(y, g, b):
    mu = jnp.mean(y, axis=-1, keepdims=True)
    d = y - mu
    var = jnp.mean(d * d, axis=-1, keepdims=True)
    return d * lax.rsqrt(var + LN_EPS) * g + b


def _mix_kernel(o1_ref, o2_ref, o3_ref, l1_ref, l2_ref, l3_ref, rec_ref, x_ref, grp_ref, ag_ref, w_ref,
                gate_ref, lng_ref, lnb_ref, sc_ref, sh_ref, x1_ref, u2_ref, urows_ref):
    l1, l2, l3 = l1_ref[...], l2_ref[...], l3_ref[...]
    m = jnp.maximum(jnp.maximum(l1, l2), l3)
    e1, e2, e3 = jnp.exp(l1 - m), jnp.exp(l2 - m), jnp.exp(l3 - m)
    attn = (e1 * o1_ref[...].astype(F32) + e2 * o2_ref[...].astype(F32)
            + e3 * o3_ref[...].astype(F32)) / (e1 + e2 + e3)
    ms = jnp.dot((attn * attn).astype(BF16), grp_ref[...], preferred_element_type=F32)
    normed = attn * lax.rsqrt(ms + RMS_EPS) * ag_ref[...]
    mixed = jnp.concatenate([normed.astype(BF16), rec_ref[...]], axis=-1)
    mix = jnp.dot(mixed, w_ref[...], preferred_element_type=F32)
    x1 = _layer_norm_rows(ALPHA * x_ref[...] + gate_ref[...] * mix, lng_ref[...], lnb_ref[...])
    x1_ref[...] = x1
    u2 = x1 * (1.0 + sc_ref[...]) + sh_ref[...]
    u2_ref[...] = u2
    for cb in range(ROW_SUB):
        urows_ref[pl.ds(cb, MIX_TM, stride=ROW_SUB), :] = u2[:, cb * LANES:(cb + 1) * LANES]


def _mix(o_branches, lse_branches, rec, x2, attn_norm_g, w_out_bf, gate1, ln_g, ln_b, scale2, shift2, S):
    T, D = x2.shape
    per_b = S // MIX_TM
    head = jnp.arange(ATTN_WIDTH) // ATTN_HEAD_DIM
    grp = jnp.where(head[:, None] == head[None, :], 1.0 / ATTN_HEAD_DIM, 0.0).astype(BF16)
    row = lambda w: pl.BlockSpec((MIX_TM, w), lambda i: (i, 0))
    const = lambda shape: pl.BlockSpec(shape, lambda i: (0,) * len(shape))
    per_batch = pl.BlockSpec((None, 1, D), lambda i: (i // per_b, 0, 0))
    return pl.pallas_call(
        _mix_kernel,
        grid=(T // MIX_TM,),
        in_specs=[row(ATTN_WIDTH)] * 6 + [row(HGRN_WIDTH), row(D), const((ATTN_WIDTH, ATTN_WIDTH)),
                  const((1, ATTN_WIDTH)), const((D, D)), per_batch, const((1, D)), const((1, D)),
                  per_batch, per_batch],
        out_specs=[row(D), row(D), pl.BlockSpec((MIX_TM * ROW_SUB, LANES), lambda i: (i, 0))],
        out_shape=[jax.ShapeDtypeStruct((T, D), F32)] * 2 + [jax.ShapeDtypeStruct((T * ROW_SUB, LANES), F32)],
        compiler_params=_cparams(("parallel",)),
        name="mix_out_ln1",
    )(*o_branches, *lse_branches, rec, x2, grp, attn_norm_g.reshape(1, -1), w_out_bf, gate1,
      ln_g.reshape(1, D), ln_b.reshape(1, D), scale2, shift2)


def _route_kernel(u_ref, w_ref, bias_ref, idx_ref, gw_ref, rank_ref, cnt_ref, run_ref):
    i = pl.program_id(0)

    @pl.when(i == 0)
    def _():
        run_ref[...] = jnp.zeros_like(run_ref)

    tm = ROUTE_TM
    logits = jnp.dot(u_ref[...], w_ref[...], preferred_element_type=F32, precision=HIGHEST)
    scores = jax.nn.sigmoid(logits)
    sel = scores + bias_ref[...]
    lane = lax.broadcasted_iota(I32, (tm, N_EXPERTS), 1)
    lane_f = lane.astype(F32)
    group = lane // GROUP_SIZE
    neg = jnp.float32(-jnp.inf)

    def first_argmax(vals):
        mx = jnp.max(vals, axis=-1, keepdims=True)
        idx = jnp.min(jnp.where(vals == mx, lane_f, float(N_EXPERTS)), axis=-1, keepdims=True)
        return mx, idx

    gscore = []
    for g in range(N_EXPERT_GROUPS):
        vals = jnp.where(group == g, sel, neg)
        m1, i1 = first_argmax(vals)
        m2 = jnp.max(jnp.where(lane_f == i1, neg, vals), axis=-1, keepdims=True)
        gscore.append(m1 + m2)
    keep_f = jnp.zeros((tm, N_EXPERTS), F32)
    for g in range(N_EXPERT_GROUPS):
        beaten = jnp.zeros((tm, 1), F32)
        for h in range(N_EXPERT_GROUPS):
            if h == g:
                continue
            ahead = (gscore[h] >= gscore[g]) if h < g else (gscore[h] > gscore[g])
            beaten = beaten + jnp.where(ahead, 1.0, 0.0)
        gkeep = jnp.where(beaten < TOPK_GROUPS, 1.0, 0.0)
        keep_f = jnp.where(group == g, gkeep, keep_f)
    vals = jnp.where(keep_f > 0.5, sel, neg)

    lane_o = lax.broadcasted_iota(I32, (tm, LANES), 1)
    idx_out = jnp.zeros((tm, LANES), F32)
    gw_out = jnp.zeros((tm, LANES), F32)
    chosen = jnp.zeros((tm, N_EXPERTS), F32)
    picks = []
    wsum = jnp.zeros((tm, 1), F32)
    for k in range(TOP_K):
        _, ik = first_argmax(vals)
        hit = lane_f == ik
        sk = jnp.sum(jnp.where(hit, scores, 0.0), axis=-1, keepdims=True)
        vals = jnp.where(hit, neg, vals)
        chosen = jnp.where(hit, 1.0, chosen)
        picks.append((ik, sk))
        wsum = wsum + sk
        idx_out = jnp.where(lane_o == k, ik, idx_out)
    for k, (ik, sk) in enumerate(picks):
        gw_out = jnp.where(lane_o == k, sk / wsum * ROUTED_SCALE, gw_out)

    r = lax.broadcasted_iota(I32, (tm, tm), 0)
    c = lax.broadcasted_iota(I32, (tm, tm), 1)
    strict_lower = jnp.where(c < r, 1.0, 0.0).astype(BF16)
    before = jnp.dot(strict_lower, chosen.astype(BF16), preferred_element_type=F32) + run_ref[...]
    rank_out = jnp.zeros((tm, LANES), F32)
    for k, (ik, sk) in enumerate(picks):
        rk = jnp.sum(jnp.where(lane_f == ik, before, 0.0), axis=-1, keepdims=True)
        rank_out = jnp.where(lane_o == k, rk, rank_out)
    run_ref[...] = run_ref[...] + jnp.sum(chosen, axis=0, keepdims=True)

    idx_ref[...] = idx_out.astype(I32)
    gw_ref[...] = gw_out
    rank_ref[...] = rank_out.astype(I32)
    cnt_ref[...] = run_ref[...]


def _route(u2, w_router, router_bias):
    T, D = u2.shape
    row = lambda w: pl.BlockSpec((ROUTE_TM, w), lambda i: (i, 0))
    idx, gw, rank, cnt = pl.pallas_call(
        _route_kernel,
        grid=(T // ROUTE_TM,),
        in_specs=[row(D), pl.BlockSpec((D, N_EXPERTS), lambda i: (0, 0)),
                  pl.BlockSpec((1, N_EXPERTS), lambda i: (0, 0))],
        out_specs=[row(LANES), row(LANES), row(LANES), pl.BlockSpec((1, N_EXPERTS), lambda i: (0, 0))],
        out_shape=[jax.ShapeDtypeStruct((T, LANES), I32), jax.ShapeDtypeStruct((T, LANES), F32),
                   jax.ShapeDtypeStruct((T, LANES), I32), jax.ShapeDtypeStruct((1, N_EXPERTS), F32)],
        scratch_shapes=[pltpu.VMEM((1, N_EXPERTS), F32)],
        compiler_params=_cparams(("arbitrary",)),
        name="router",
    )(u2, w_router, router_bias.reshape(1, N_EXPERTS))
    return idx, gw, rank, cnt[0].astype(I32)


def _slot_kernel(idx_ref, rank_ref, start_ref, dest_ref):
    tm = ROUTE_TM
    lane_e = lax.broadcasted_iota(I32, (tm, N_EXPERTS), 1)
    lane_o = lax.broadcasted_iota(I32, (tm, LANES), 1)
    idx = idx_ref[...]
    start = start_ref[...]
    base = jnp.zeros((tm, LANES), F32)
    for k in range(TOP_K):
        sk = jnp.sum(jnp.where(lane_e == idx[:, k:k + 1], start, 0.0), axis=-1, keepdims=True)
        base = jnp.where(lane_o == k, sk, base)
    dest_ref[...] = base.astype(I32) + rank_ref[...]


def _slots(idx, rank, start):
    T = idx.shape[0]
    row = pl.BlockSpec((ROUTE_TM, LANES), lambda i: (i, 0))
    dest = pl.pallas_call(
        _slot_kernel,
        grid=(T // ROUTE_TM,),
        in_specs=[row, row, pl.BlockSpec((1, N_EXPERTS), lambda i: (0, 0))],
        out_specs=row,
        out_shape=jax.ShapeDtypeStruct((T, LANES), I32),
        compiler_params=_cparams(("parallel",)),
        name="slots",
    )(idx, rank, start.astype(F32).reshape(1, N_EXPERTS))
    return dest[:, :TOP_K]


MOE_META_SLOTS = 4
Y_TILE = 8
Y_TILE_SHIFT = Y_TILE.bit_length() - 1


def _moe_kernel(bexp_ref, wnext_ref, nused_ref, meta_hbm, u_hbm, wg_hbm, wu_hbm, wd_hbm, y_hbm,
                meta_smem, xb0, xb1, yb0, yb1, wg_st, wu_st, wd_st, wg_bf, wu_bf, wd_bf,
                sem_meta, sem_g, sem_s, sem_w):
    i = pl.program_id(0)
    n_blocks = pl.num_programs(0)
    nused = nused_ref[0]
    last = nused - 1

    def meta_copy(blk, sl):
        row = pl.ds(pl.multiple_of(blk * (2 * MOE_BLOCK), 2 * MOE_BLOCK), 2 * MOE_BLOCK)
        return pltpu.make_async_copy(meta_hbm.at[row], meta_smem.at[sl], sem_meta.at[sl])

    def weight_copies(e):
        return (pltpu.make_async_copy(wg_hbm.at[e], wg_st, sem_w.at[0]),
                pltpu.make_async_copy(wu_hbm.at[e], wu_st, sem_w.at[1]),
                pltpu.make_async_copy(wd_hbm.at[e], wd_st, sem_w.at[2]))

    def gather_rows(sl, xdst, sem):
        for j in range(MOE_BLOCK):
            row0 = pl.multiple_of(meta_smem[sl, j] * ROW_SUB, ROW_SUB)
            pltpu.make_async_copy(u_hbm.at[pl.ds(row0, ROW_SUB)], xdst.at[pl.ds(j * ROW_PITCH, ROW_SUB)], sem).start()

    def scatter_rows(sl, ysrc, sem):
        for j in range(MOE_BLOCK):
            dst = meta_smem[sl, MOE_BLOCK + j]
            pltpu.make_async_copy(ysrc.at[pl.ds(j, 1)],
                                  y_hbm.at[dst >> Y_TILE_SHIFT, pl.ds(dst & (Y_TILE - 1), 1)], sem).start()

    def wait_gather(xdst, sem):
        n = MOE_BLOCK * ROW_SUB
        pltpu.make_async_copy(u_hbm.at[pl.ds(0, n)], xdst.at[pl.ds(0, n)], sem).wait()

    def wait_rows(buf, sem):
        other = yb1 if buf is yb0 else yb0
        pltpu.make_async_copy(other, buf, sem).wait()

    @pl.when(i >= nused)
    def _():
        yb0[...] = jnp.zeros_like(yb0)
        tiles = MOE_BLOCK // Y_TILE
        for a in range(tiles):
            pltpu.make_async_copy(yb0.at[pl.ds(a * Y_TILE, Y_TILE)], y_hbm.at[i * tiles + a], sem_s.at[0]).start()
        wait_rows(yb0, sem_s.at[0])

    def step(p):
        xcur, xnxt = (xb0, xb1) if p == 0 else (xb1, xb0)
        ycur, yprv = (yb0, yb1) if p == 0 else (yb1, yb0)
        s_cur = i % MOE_META_SLOTS
        s_nxt = (i + 1) % MOE_META_SLOTS
        s_nn = (i + 2) % MOE_META_SLOTS
        s_prv = (i + 3) % MOE_META_SLOTS
        nxt = jnp.minimum(i + 1, last)
        nn = jnp.minimum(i + 2, last)

        @pl.when(i == 0)
        def _():
            first = meta_copy(0, 0)
            first.start()
            first.wait()
            dump = meta_copy(n_blocks, MOE_META_SLOTS - 1)
            dump.start()
            dump.wait()
            meta_copy(nxt, 1).start()
            yprv[...] = jnp.zeros_like(yprv)
            gather_rows(0, xcur, sem_g.at[p])
            for cp in weight_copies(bexp_ref[0]):
                cp.start(priority=1)

        meta_copy(nxt, s_nxt).wait()
        meta_copy(nn, s_nn).start()

        @pl.when((i == 0) | (bexp_ref[i] != bexp_ref[jnp.maximum(i - 1, 0)]))
        def _():
            for cp in weight_copies(bexp_ref[i]):
                cp.wait()
            wg_bf[...] = wg_st[...].astype(BF16)
            wu_bf[...] = wu_st[...].astype(BF16)
            wd_bf[...] = wd_st[...].astype(BF16)
            nxt_e = wnext_ref[i]

            @pl.when(nxt_e >= 0)
            def _():
                for cp in weight_copies(nxt_e):
                    cp.start(priority=1)

        wait_gather(xcur, sem_g.at[p])
        gather_rows(s_nxt, xnxt, sem_g.at[1 - p])
        scatter_rows(s_prv, yprv, sem_s.at[1 - p])
        x = jnp.concatenate([xcur[pl.ds(cb, MOE_BLOCK, stride=ROW_PITCH), :].astype(BF16)
                             for cb in range(ROW_SUB)], axis=1)
        hg = jnp.dot(x, wg_bf[...], preferred_element_type=F32)
        hu = jnp.dot(x, wu_bf[...], preferred_element_type=F32)
        act = (hg * jax.nn.sigmoid(hg) * hu).astype(BF16)

        @pl.when(i >= 1)
        def _():
            wait_rows(ycur, sem_s.at[p])

        ycur[...] = jnp.dot(act, wd_bf[...], preferred_element_type=F32)

        @pl.when(i == last)
        def _():
            scatter_rows(s_cur, ycur, sem_s.at[p])
            wait_rows(ycur, sem_s.at[p])
            wait_rows(yprv, sem_s.at[1 - p])
            wait_gather(xnxt, sem_g.at[1 - p])
            meta_copy(nn, s_nn).wait()

    @pl.when((i < nused) & (i % 2 == 0))
    def _():
        step(0)

    @pl.when((i < nused) & (i % 2 == 1))
    def _():
        step(1)


def _moe(u_rows, bexp, wnext, nused, meta, w_gate, w_up, w_down, n_pad):
    D = ROW_SUB * LANES
    n_blocks = n_pad // MOE_BLOCK
    E = EXPERT_DIM
    grid_spec = pltpu.PrefetchScalarGridSpec(
        num_scalar_prefetch=3,
        grid=(n_blocks,),
        in_specs=[pl.BlockSpec(memory_space=pl.ANY)] * 5,
        out_specs=pl.BlockSpec(memory_space=pl.ANY),
        scratch_shapes=[pltpu.SMEM((MOE_META_SLOTS, 2 * MOE_BLOCK), I32),
                        pltpu.VMEM((MOE_BLOCK * ROW_PITCH, LANES), F32),
                        pltpu.VMEM((MOE_BLOCK * ROW_PITCH, LANES), F32),
                        pltpu.VMEM((MOE_BLOCK, D), F32),
                        pltpu.VMEM((MOE_BLOCK, D), F32),
                        pltpu.VMEM((D, E), F32),
                        pltpu.VMEM((D, E), F32),
                        pltpu.VMEM((E, D), F32),
                        pltpu.VMEM((D, E), BF16),
                        pltpu.VMEM((D, E), BF16),
                        pltpu.VMEM((E, D), BF16),
                        pltpu.SemaphoreType.DMA((MOE_META_SLOTS,)),
                        pltpu.SemaphoreType.DMA((2,)),
                        pltpu.SemaphoreType.DMA((2,)),
                        pltpu.SemaphoreType.DMA((3,))])
    return pl.pallas_call(
        _moe_kernel,
        grid_spec=grid_spec,
        out_shape=jax.ShapeDtypeStruct(((n_pad + MOE_BLOCK) // Y_TILE, Y_TILE, D), F32),
        compiler_params=_cparams(("arbitrary",)),
        name="moe_experts",
    )(bexp, wnext, nused, meta, u_rows, w_gate, w_up, w_down)


def _dispatch_plan(idx, rank, counts, T):
    n_assign = T * TOP_K
    n_blocks = -(-(n_assign + N_EXPERTS * (MOE_BLOCK - 1)) // MOE_BLOCK)
    n_pad = n_blocks * MOE_BLOCK
    padded = (counts + MOE_BLOCK - 1) // MOE_BLOCK * MOE_BLOCK
    padded_end = jnp.cumsum(padded)
    start = padded_end - padded
    dest = _slots(idx, rank, start)
    src = jnp.full((n_pad,), -1, I32).at[dest.reshape(-1)].set(jnp.arange(n_assign, dtype=I32))
    is_pad = src < 0
    pad_rank = jnp.cumsum(is_pad.astype(I32)) - 1
    tok = jnp.where(is_pad, 0, src // TOP_K)
    dst = jnp.where(is_pad, n_assign + pad_rank, src)
    tok = jnp.concatenate([tok, jnp.zeros((MOE_BLOCK,), I32)]).reshape(n_blocks + 1, MOE_BLOCK)
    dst = jnp.concatenate([dst, n_pad + jnp.arange(MOE_BLOCK, dtype=I32)]).reshape(n_blocks + 1, MOE_BLOCK)
    meta = jnp.concatenate([tok, dst], axis=1).reshape(-1)
    block_row0 = jnp.arange(n_blocks, dtype=I32) * MOE_BLOCK
    bexp = jnp.minimum(jnp.sum((padded_end[None, :] <= block_row0[:, None]).astype(I32), axis=1),
                       N_EXPERTS - 1).astype(I32)
    nused = (padded_end[-1] // MOE_BLOCK).astype(I32).reshape(1)
    eid = jnp.arange(N_EXPERTS, dtype=I32)
    later = (eid[None, :] > eid[:, None]) & (counts[None, :] > 0)
    next_e = jnp.min(jnp.where(later, eid[None, :], N_EXPERTS), axis=1)
    onehot = (bexp[:, None] == eid[None, :]).astype(I32)
    wnext = jnp.sum(onehot * next_e[None, :], axis=1)
    wnext = jnp.where(wnext < N_EXPERTS, wnext, -1).astype(I32)
    return bexp, wnext, nused, meta.astype(I32), n_pad


def _final_kernel(u_ref, y_ref, gw_ref, x1_ref, wg_ref, wu_ref, wd_ref, gate_ref, lng_ref, lnb_ref, o_ref):
    tm = FINAL_TM
    nrow = tm * TOP_K
    u = u_ref[...].astype(BF16)
    hg = jnp.dot(u, wg_ref[...], preferred_element_type=F32)
    hu = jnp.dot(u, wu_ref[...], preferred_element_type=F32)
    act = (hg * jax.nn.sigmoid(hg) * hu).astype(BF16)
    ffn = jnp.dot(act, wd_ref[...], preferred_element_type=F32)
    expand = jnp.where((lax.broadcasted_iota(I32, (nrow, tm), 0) // TOP_K) == lax.broadcasted_iota(I32, (nrow, tm), 1),
                       1.0, 0.0)
    gw_rows = jnp.dot(expand, gw_ref[...], preferred_element_type=F32, precision=HIGHEST)
    pick = lax.broadcasted_iota(I32, (nrow, LANES), 1) == (lax.broadcasted_iota(I32, (nrow, LANES), 0) % TOP_K)
    wcol = jnp.sum(jnp.where(pick, gw_rows, 0.0), axis=-1, keepdims=True)
    yw = (y_ref[...] * wcol).astype(BF16)
    fold = jnp.where((lax.broadcasted_iota(I32, (tm, nrow), 1) // TOP_K) == lax.broadcasted_iota(I32, (tm, nrow), 0),
                     1.0, 0.0).astype(BF16)
    ffn = ffn + jnp.dot(fold, yw, preferred_element_type=F32)
    o_ref[...] = _layer_norm_rows(ALPHA * x1_ref[...] + gate_ref[...] * ffn, lng_ref[...], lnb_ref[...])


def _final(u2, y_flat, gw, x1, ws_gate_bf, ws_up_bf, ws_down_bf, gate2, ln_g, ln_b, S):
    T, D = u2.shape
    E = ws_gate_bf.shape[1]
    per_b = S // FINAL_TM
    assert Y_TILE == TOP_K
    row = lambda w: pl.BlockSpec((FINAL_TM, w), lambda i: (i, 0))
    const = lambda shape: pl.BlockSpec(shape, lambda i: (0,) * len(shape))
    return pl.pallas_call(
        _final_kernel,
        grid=(T // FINAL_TM,),
        in_specs=[row(D), pl.BlockSpec((FINAL_TM * TOP_K, D), lambda i: (i, 0)), row(LANES), row(D),
                  const((D, E)), const((D, E)), const((E, D)),
                  pl.BlockSpec((None, 1, D), lambda i: (i // per_b, 0, 0)), const((1, D)), const((1, D))],
        out_specs=row(D),
        out_shape=jax.ShapeDtypeStruct((T, D), F32),
        compiler_params=_cparams(("parallel",)),
        name="shared_combine_ln2",
    )(u2, y_flat.reshape(-1, D), gw, x1, ws_gate_bf, ws_up_bf, ws_down_bf, gate2, ln_g.reshape(1, D), ln_b.reshape(1, D))


def kernel(x, c, positions, w_ada, b_ada, w_in, lb_logits, attn_norm_g, hgrn_norm_g, w_out, ln1_g, ln1_b,
           w_router, router_bias, expert_w_gate, expert_w_up, expert_w_down, shared_w_gate, shared_w_up,
           shared_w_down, ln2_g, ln2_b):
    B, S, D = x.shape
    T = B * S
    layer = 0
    lower_bounds = jnp.cumsum(jax.nn.softmax(lb_logits.astype(F32), axis=1), axis=1)

    mod = _ada_mod(c, w_ada[layer], b_ada[layer])
    shift1, scale1, gate1, shift2, scale2, gate2 = [m.reshape(B, 1, D) for m in jnp.split(mod, 6, axis=-1)]
    cos_t, sin_t = _rope_tables(positions)
    x2 = x.reshape(T, D)

    proj, qkv4, qkv16 = _in_proj(x2, scale1, shift1, w_in[layer].astype(BF16), cos_t, sin_t, B, S)
    branches = [_attn_branch(qkv, d, B, S)
                for qkv, d in zip((proj.reshape(B, S, -1), qkv4, qkv16), DILATIONS)]
    rec = _hgrn(proj, lower_bounds[0, layer], lower_bounds[1, layer], hgrn_norm_g[layer], B, S)
    x1, u2, u_rows = _mix([o for o, _ in branches], [l for _, l in branches], rec, x2, attn_norm_g[layer],
                  w_out[layer].astype(BF16), gate1, ln1_g[layer], ln1_b[layer], scale2, shift2, S)

    idx, gw, rank, counts = _route(u2, w_router[layer], router_bias[layer])
    bexp, wnext, nused, meta, n_pad = _dispatch_plan(idx, rank, counts, T)
    y_flat = _moe(u_rows, bexp, wnext, nused, meta, expert_w_gate[layer], expert_w_up[layer], expert_w_down[layer], n_pad)
    out = _final(u2, y_flat, gw, x1, shared_w_gate[layer].astype(BF16), shared_w_up[layer].astype(BF16),
                 shared_w_down[layer].astype(BF16), gate2, ln2_g[layer], ln2_b[layer], S)
    return out.reshape(B, S, D)
```

```python
import functools

import jax
import jax.numpy as jnp
from jax import lax
from jax.experimental import pallas as pl
from jax.experimental.pallas import tpu as pltpu

F32 = jnp.float32
BF16 = jnp.bfloat16
I32 = jnp.int32
HIGHEST = lax.Precision.HIGHEST

D_MODEL = 2048
ATTN_HEADS = 16
ATTN_HEAD_DIM = 64
ATTN_WIDTH = ATTN_HEADS * ATTN_HEAD_DIM
HGRN_HEADS = 8
HGRN_HEAD_DIM = 128
HGRN_WIDTH = HGRN_HEADS * HGRN_HEAD_DIM
IN_PROJ_WIDTH = 3 * ATTN_WIDTH + 5 * HGRN_WIDTH
DILATIONS = (1, 4, 16)
ATTN_HALF = 64
ROPE_THETA = 10000.0
N_EXPERTS = 256
N_EXPERT_GROUPS = 8
GROUP_SIZE = N_EXPERTS // N_EXPERT_GROUPS
TOPK_GROUPS = 4
TOP_K = 8
EXPERT_DIM = 512
ROUTED_SCALE = 2.5
DEPTH = 1
ALPHA = (2 * DEPTH) ** 0.25
LN_EPS = 1e-5
RMS_EPS = 1e-6
NEG_INF = -1e30
LOG2E = 1.4426950408889634

LANES = 128
VMEM_LIMIT = 56 * 1024 * 1024

ADA_TN = 1024
ROPE_TM = 2048
INPROJ_TM = 512
INPROJ_TN = 1024
ATTN_TQ = 128
ATTN_WK = 256
ATTN_UNROLL = 4
HGRN_CHUNK = 64
HGRN_SUB = 16
MIX_TM = 256
ROUTE_TM = 256
MOE_BLOCK = 128
ROW_SUB = D_MODEL // LANES
ROW_PITCH = 24
FINAL_TM = 128


def _cparams(sem):
    return pltpu.CompilerParams(dimension_semantics=sem, vmem_limit_bytes=VMEM_LIMIT)


def _ada_kernel(c_ref, w_ref, b_ref, o_ref):
    c = c_ref[...]
    sc = c * jax.nn.sigmoid(c)
    o_ref[...] = jnp.dot(sc, w_ref[...], preferred_element_type=F32, precision=HIGHEST) + b_ref[...]


def _ada_mod(c, w_ada, b_ada):
    B, D = c.shape
    N = w_ada.shape[1]
    c8 = jnp.zeros((8, D), F32).at[:B].set(c)
    out = pl.pallas_call(
        _ada_kernel,
        grid=(N // ADA_TN,),
        in_specs=[pl.BlockSpec((8, D), lambda j: (0, 0)),
                  pl.BlockSpec((D, ADA_TN), lambda j: (0, j)),
                  pl.BlockSpec((1, ADA_TN), lambda j: (0, j))],
        out_specs=pl.BlockSpec((8, ADA_TN), lambda j: (0, j)),
        out_shape=jax.ShapeDtypeStruct((8, N), F32),
        compiler_params=_cparams(("parallel",)),
        name="ada_mod",
    )(c8, w_ada, b_ada.reshape(1, N))
    return out[:B]


def _rope_kernel(pos_ref, invf_ref, sign_ref, cos_ref, sin_ref):
    ang = pos_ref[...].astype(F32) * invf_ref[...]
    cos_ref[...] = jnp.cos(ang)
    sin_ref[...] = jnp.sin(ang) * sign_ref[...]


def _rope_tables(positions):
    T = positions.size
    half = ATTN_HEAD_DIM // 2
    inv_freq = ROPE_THETA ** (-jnp.arange(half, dtype=F32) / half)
    lane = jnp.arange(LANES)
    invf = inv_freq[lane % half].reshape(1, LANES)
    sign = jnp.where((lane % ATTN_HEAD_DIM) < half, -1.0, 1.0).astype(F32).reshape(1, LANES)
    return pl.pallas_call(
        _rope_kernel,
        grid=(T // ROPE_TM,),
        in_specs=[pl.BlockSpec((ROPE_TM, 1), lambda i: (i, 0)),
                  pl.BlockSpec((1, LANES), lambda i: (0, 0)),
                  pl.BlockSpec((1, LANES), lambda i: (0, 0))],
        out_specs=[pl.BlockSpec((ROPE_TM, LANES), lambda i: (i, 0)),
                   pl.BlockSpec((ROPE_TM, LANES), lambda i: (i, 0))],
        out_shape=[jax.ShapeDtypeStruct((T, LANES), F32)] * 2,
        compiler_params=_cparams(("parallel",)),
        name="rope_tables",
    )(positions.reshape(T, 1), invf, sign)


def _inproj_kernel(x_ref, sc_ref, sh_ref, w_ref, cos_ref, sin_ref, o_ref, o4_ref, o16_ref, stage_ref):
    j = pl.program_id(1)
    u = (x_ref[...] * (1.0 + sc_ref[...]) + sh_ref[...]).astype(BF16)
    acc = jnp.dot(u, w_ref[...], preferred_element_type=F32)

    @pl.when(j < 2)
    def _():
        qscale = jnp.where(j == 0, ATTN_HEAD_DIM ** -0.5, 1.0).astype(F32)
        cos = cos_ref[...] * qscale
        sin = sin_ref[...] * qscale
        lane = lax.broadcasted_iota(I32, cos.shape, 1)
        first = (lane % ATTN_HEAD_DIM) < (ATTN_HEAD_DIM // 2)
        for cb in range(INPROJ_TN // LANES):
            a = acc[:, cb * LANES:(cb + 1) * LANES]
            partner = jnp.where(first, pltpu.roll(a, LANES - ATTN_HEAD_DIM // 2, 1),
                                pltpu.roll(a, ATTN_HEAD_DIM // 2, 1))
            stage_ref[cb] = a * cos + partner * sin

    @pl.when(j == 2)
    def _():
        for cb in range(INPROJ_TN // LANES):
            stage_ref[cb] = acc[:, cb * LANES:(cb + 1) * LANES]

    @pl.when(j < 3)
    def _():
        for cb in range(INPROJ_TN // LANES):
            o_ref[:, cb * LANES:(cb + 1) * LANES] = stage_ref[cb].astype(BF16)
            for d, od_ref in ((DILATIONS[1], o4_ref), (DILATIONS[2], o16_ref)):
                rows = INPROJ_TM // d
                for r in range(d):
                    c0 = r * INPROJ_TN + cb * LANES
                    od_ref[:, c0:c0 + LANES] = stage_ref[cb, pl.ds(r, rows, stride=d), :].astype(BF16)

    @pl.when(j >= 3)
    def _():
        o_ref[...] = acc.astype(BF16)


def _in_proj(x2, scale1, shift1, w_in_bf, cos_t, sin_t, B, S):
    T, D = x2.shape
    P = w_in_bf.shape[1]
    per_b = S // INPROJ_TM
    assert INPROJ_TN == ATTN_WIDTH
    d4, d16 = DILATIONS[1], DILATIONS[2]
    strided_spec = lambda d: pl.BlockSpec((None, INPROJ_TM // d, d * INPROJ_TN),
                                          lambda i, j: (i // per_b, i % per_b, jnp.minimum(j, 2)))
    strided_shape = lambda d: jax.ShapeDtypeStruct((B, S // d, 3 * d * ATTN_WIDTH), BF16)
    return pl.pallas_call(
        _inproj_kernel,
        grid=(T // INPROJ_TM, P // INPROJ_TN),
        in_specs=[pl.BlockSpec((INPROJ_TM, D), lambda i, j: (i, 0)),
                  pl.BlockSpec((None, 1, D), lambda i, j: (i // per_b, 0, 0)),
                  pl.BlockSpec((None, 1, D), lambda i, j: (i // per_b, 0, 0)),
                  pl.BlockSpec((D, INPROJ_TN), lambda i, j: (0, j)),
                  pl.BlockSpec((INPROJ_TM, LANES), lambda i, j: (i, 0)),
                  pl.BlockSpec((INPROJ_TM, LANES), lambda i, j: (i, 0))],
        out_specs=[pl.BlockSpec((INPROJ_TM, INPROJ_TN), lambda i, j: (i, j)), strided_spec(d4), strided_spec(d16)],
        out_shape=[jax.ShapeDtypeStruct((T, P), BF16), strided_shape(d4), strided_shape(d16)],
        scratch_shapes=[pltpu.VMEM((INPROJ_TN // LANES, INPROJ_TM, LANES), F32)],
        compiler_params=_cparams(("parallel", "arbitrary")),
        name="in_proj",
    )(x2, scale1, shift1, w_in_bf, cos_t, sin_t)


def _attn_kernel(q_ref, k_ref, v_ref, o_ref, lse_ref, *, L):
    nq = L // ATTN_TQ
    lane = lax.broadcasted_iota(I32, (1, LANES), 1)
    head0 = lane < ATTN_HEAD_DIM
    rel = (lax.broadcasted_iota(I32, (ATTN_TQ, ATTN_WK), 1)
           - lax.broadcasted_iota(I32, (ATTN_TQ, ATTN_WK), 0))

    def body(i, carry):
        q0 = pl.multiple_of(i * ATTN_TQ, ATTN_TQ)
        ks = pl.multiple_of(jnp.clip(i * ATTN_TQ - ATTN_HALF, 0, L - ATTN_WK), ATTN_HALF)
        q = q_ref[pl.ds(q0, ATTN_TQ), :]
        k = k_ref[pl.ds(ks, ATTN_WK), :]
        v = v_ref[pl.ds(ks, ATTN_WK), :]
        mask = jnp.abs(rel + (ks - q0)) <= ATTN_HALF
        outs, lses = [], []
        for hmask in (head0, jnp.logical_not(head0)):
            qh = jnp.where(hmask, q, jnp.zeros_like(q))
            s = lax.dot_general(qh, k, (((1,), (1,)), ((), ())), preferred_element_type=F32)
            s = jnp.where(mask, s, NEG_INF)
            m = jnp.max(s, axis=-1, keepdims=True)
            p = jnp.exp(s - m)
            l = jnp.sum(p, axis=-1, keepdims=True)
            outs.append(jnp.dot(p.astype(BF16), v, preferred_element_type=F32) / l)
            lses.append(m + jnp.log(l))
        o_ref[pl.ds(q0, ATTN_TQ), :] = jnp.where(head0, outs[0], outs[1]).astype(BF16)
        lse_ref[pl.ds(q0, ATTN_TQ), :] = jnp.where(head0, lses[0], lses[1])
        return carry

    lax.fori_loop(0, nq, body, 0, unroll=ATTN_UNROLL)


def _attn_branch(qkv, dilation, B, S):
    L = S // dilation
    pv = qkv
    acb = ATTN_WIDTH // LANES
    in_spec = lambda part: pl.BlockSpec((None, L, LANES),
                                        lambda b, r, h: (b, 0, (part * dilation + r) * acb + h))
    out_spec = pl.BlockSpec((None, L, LANES), lambda b, r, h: (b, 0, r * acb + h))
    o, lse = pl.pallas_call(
        functools.partial(_attn_kernel, L=L),
        grid=(B, dilation, acb),
        in_specs=[in_spec(0), in_spec(1), in_spec(2)],
        out_specs=[out_spec, out_spec],
        out_shape=[jax.ShapeDtypeStruct((B, L, dilation * ATTN_WIDTH), BF16),
                   jax.ShapeDtypeStruct((B, L, dilation * ATTN_WIDTH), F32)],
        compiler_params=_cparams(("parallel", "parallel", "parallel")),
        name=f"attn_d{dilation}",
    )(pv, pv, pv)
    return o.reshape(B * S, ATTN_WIDTH), lse.reshape(B * S, ATTN_WIDTH)


def _hgrn_chunk(q, kk, lf, v, state_t, tri, reverse):
    C = HGRN_CHUNK
    SUB = HGRN_SUB
    nsub = C // SUB
    b = jnp.dot(tri, lf, preferred_element_type=F32, precision=HIGHEST) * LOG2E
    col = lax.broadcasted_iota(I32, (SUB, C), 1)
    row = lax.broadcasted_iota(I32, (SUB, C), 0)
    score_rows = []
    for i in range(nsub):
        r0 = i * SUB
        bi = b[r0:r0 + SUB]
        qi = q[r0:r0 + SUB]
        ki = kk[r0:r0 + SUB]
        diag = jnp.zeros((SUB, C), F32)
        for s in range(SUB):
            e = jnp.exp2(bi - bi[s:s + 1])
            colv = jnp.sum(qi * e * ki[s:s + 1], axis=-1, keepdims=True)
            diag = jnp.where(col == r0 + s, colv, diag)
        if reverse:
            keep = (col - r0) >= row
        else:
            keep = (col - r0) <= row
        diag = jnp.where(jnp.logical_and(keep, jnp.logical_and(col >= r0, col < r0 + SUB)), diag, 0.0)
        if reverse:
            has_off = i < nsub - 1
            bref = b[r0 + SUB:r0 + SUB + 1] if has_off else None
            off_mask = col >= r0 + SUB
        else:
            has_off = i > 0
            bref = b[r0 - 1:r0] if has_off else None
            off_mask = col < r0
        if has_off:
            qs = (qi * jnp.exp2(bi - bref)).astype(BF16)
            ks = (kk * jnp.exp2(bref - b)).astype(BF16)
            off = lax.dot_general(qs, ks, (((1,), (1,)), ((), ())), preferred_element_type=F32)
            score_rows.append(jnp.where(off_mask, off, diag))
        else:
            score_rows.append(diag)
    scores = jnp.concatenate(score_rows, axis=0)
    b_edge = b[0:1] if reverse else b[C - 1:C]
    o = jnp.dot(scores.astype(BF16), v.astype(BF16), preferred_element_type=F32)
    qd = (q * jnp.exp2(b)).astype(BF16)
    o = o + lax.dot_general(qd, state_t.astype(BF16), (((1,), (1,)), ((), ())),
                            preferred_element_type=F32)
    kd = (kk * jnp.exp2(b_edge - b)).astype(BF16)
    upd = lax.dot_general(v.astype(BF16), kd, (((0,), (0,)), ((), ())), preferred_element_type=F32)
    state_t = state_t * jnp.exp2(b_edge) + upd
    return o, state_t


def _hgrn_kernel(q_ref, zf_ref, zb_ref, v_ref, g_ref, lbf_ref, lbb_ref, ng_ref, o_ref, acc_ref, *, S):
    C = HGRN_CHUNK
    n_chunks = S // C
    r = lax.broadcasted_iota(I32, (C, C), 0)
    c = lax.broadcasted_iota(I32, (C, C), 1)
    tri_f = (c <= r).astype(F32)
    tri_b = (c >= r).astype(F32)
    lbf = lbf_ref[...]
    lbb = lbb_ref[...]
    acc_ref[...] = jnp.zeros_like(acc_ref)

    def gates(z, lb):
        sg = jax.nn.sigmoid(z)
        return jnp.log(lb + (1.0 - lb) * sg), (1.0 - lb) * (1.0 - sg)

    def body(n, carry):
        st_f, st_b = carry
        rf = pl.multiple_of(n * C, C)
        rb = pl.multiple_of((n_chunks - 1 - n) * C, C)
        lf_f, kk_f = gates(zf_ref[pl.ds(rf, C), :].astype(F32), lbf)
        o_f, st_f = _hgrn_chunk(q_ref[pl.ds(rf, C), :].astype(F32), kk_f, lf_f,
                                v_ref[pl.ds(rf, C), :].astype(F32), st_f, tri_f, False)
        acc_ref[pl.ds(rf, C), :] += o_f
        lf_b, kk_b = gates(zb_ref[pl.ds(rb, C), :].astype(F32), lbb)
        o_b, st_b = _hgrn_chunk(q_ref[pl.ds(rb, C), :].astype(F32), kk_b, lf_b,
                                v_ref[pl.ds(rb, C), :].astype(F32), st_b, tri_b, True)
        acc_ref[pl.ds(rb, C), :] += o_b
        return st_f, st_b

    z0 = jnp.zeros((HGRN_HEAD_DIM, HGRN_HEAD_DIM), F32)
    lax.fori_loop(0, n_chunks, body, (z0, z0))

    ng = ng_ref[...]

    def norm_body(n, carry):
        r0 = pl.multiple_of(n * 512, 512)
        t = acc_ref[pl.ds(r0, 512), :]
        g = g_ref[pl.ds(r0, 512), :].astype(F32)
        y = t * lax.rsqrt(jnp.mean(t * t, axis=-1, keepdims=True) + RMS_EPS) * ng
        o_ref[pl.ds(r0, 512), :] = (y * (g * jax.nn.sigmoid(g))).astype(BF16)
        return carry

    lax.fori_loop(0, S // 512, norm_body, 0)


def _hgrn(proj, lb_fwd, lb_bwd, norm_g, B, S):
    P = proj.shape[1]
    pv = proj.reshape(B, S, P)
    base = 3 * ATTN_WIDTH // LANES
    nh = HGRN_HEADS
    in_spec = lambda k: pl.BlockSpec((None, S, LANES), lambda b, h: (b, 0, base + k * nh + h))
    vec_spec = pl.BlockSpec((None, 1, LANES), lambda b, h: (h, 0, 0))
    out = pl.pallas_call(
        functools.partial(_hgrn_kernel, S=S),
        grid=(B, nh),
        in_specs=[in_spec(0), in_spec(1), in_spec(2), in_spec(3), in_spec(4), vec_spec, vec_spec, vec_spec],
        out_specs=pl.BlockSpec((None, S, LANES), lambda b, h: (b, 0, h)),
        out_shape=jax.ShapeDtypeStruct((B, S, HGRN_WIDTH), BF16),
        scratch_shapes=[pltpu.VMEM((S, LANES), F32)],
        compiler_params=_cparams(("parallel", "parallel")),
        name="hgrn2",
    )(pv, pv, pv, pv, pv, lb_fwd.reshape(nh, 1, LANES), lb_bwd.reshape(nh, 1, LANES),
      norm_g.reshape(nh, 1, LANES))
    return out.reshape(B * S, HGRN_WIDTH)


def _layer_norm_rows(y, g, b):
    mu = jnp.mean(y, axis=-1, keepdims=True)
    d = y - mu
    var = jnp.mean(d * d, axis=-1, keepdims=True)
    return d * lax.rsqrt(var + LN_EPS) * g + b


def _mix_kernel(o1_ref, o2_ref, o3_ref, l1_ref, l2_ref, l3_ref, rec_ref, x_ref, grp_ref, ag_ref, w_ref,
                gate_ref, lng_ref, lnb_ref, sc_ref, sh_ref, x1_ref, u2_ref, urows_ref):
    l1, l2, l3 = l1_ref[...], l2_ref[...], l3_ref[...]
    m = jnp.maximum(jnp.maximum(l1, l2), l3)
    e1, e2, e3 = jnp.exp(l1 - m), jnp.exp(l2 - m), jnp.exp(l3 - m)
    attn = (e1 * o1_ref[...].astype(F32) + e2 * o2_ref[...].astype(F32)
            + e3 * o3_ref[...].astype(F32)) / (e1 + e2 + e3)
    ms = jnp.dot((attn * attn).astype(BF16), grp_ref[...], preferred_element_type=F32)
    normed = attn * lax.rsqrt(ms + RMS_EPS) * ag_ref[...]
    mixed = jnp.concatenate([normed.astype(BF16), rec_ref[...]], axis=-1)
    mix = jnp.dot(mixed, w_ref[...], preferred_element_type=F32)
    x1 = _layer_norm_rows(ALPHA * x_ref[...] + gate_ref[...] * mix, lng_ref[...], lnb_ref[...])
    x1_ref[...] = x1
    u2 = x1 * (1.0 + sc_ref[...]) + sh_ref[...]
    u2_ref[...] = u2
    for cb in range(ROW_SUB):
        urows_ref[pl.ds(cb, MIX_TM, stride=ROW_SUB), :] = u2[:, cb * LANES:(cb + 1) * LANES]


def _mix(o_branches, lse_branches, rec, x2, attn_norm_g, w_out_bf, gate1, ln_g, ln_b, scale2, shift2, S):
    T, D = x2.shape
    per_b = S // MIX_TM
    head = jnp.arange(ATTN_WIDTH) // ATTN_HEAD_DIM
    grp = jnp.where(head[:, None] == head[None, :], 1.0 / ATTN_HEAD_DIM, 0.0).astype(BF16)
    row = lambda w: pl.BlockSpec((MIX_TM, w), lambda i: (i, 0))
    const = lambda shape: pl.BlockSpec(shape, lambda i: (0,) * len(shape))
    per_batch = pl.BlockSpec((None, 1, D), lambda i: (i // per_b, 0, 0))
    return pl.pallas_call(
        _mix_kernel,
        grid=(T // MIX_TM,),
        in_specs=[row(ATTN_WIDTH)] * 6 + [row(HGRN_WIDTH), row(D), const((ATTN_WIDTH, ATTN_WIDTH)),
                  const((1, ATTN_WIDTH)), const((D, D)), per_batch, const((1, D)), const((1, D)),
                  per_batch, per_batch],
        out_specs=[row(D), row(D), pl.BlockSpec((MIX_TM * ROW_SUB, LANES), lambda i: (i, 0))],
        out_shape=[jax.ShapeDtypeStruct((T, D), F32)] * 2 + [jax.ShapeDtypeStruct((T * ROW_SUB, LANES), F32)],
        compiler_params=_cparams(("parallel",)),
        name="mix_out_ln1",
    )(*o_branches, *lse_branches, rec, x2, grp, attn_norm_g.reshape(1, -1), w_out_bf, gate1,
      ln_g.reshape(1, D), ln_b.reshape(1, D), scale2, shift2)


def _route_kernel(u_ref, w_ref, bias_ref, idx_ref, gw_ref, rank_ref, cnt_ref, run_ref):
    i = pl.program_id(0)

    @pl.when(i == 0)
    def _():
        run_ref[...] = jnp.zeros_like(run_ref)

    tm = ROUTE_TM
    logits = jnp.dot(u_ref[...], w_ref[...], preferred_element_type=F32, precision=HIGHEST)
    scores = jax.nn.sigmoid(logits)
    sel = scores + bias_ref[...]
    lane = lax.broadcasted_iota(I32, (tm, N_EXPERTS), 1)
    lane_f = lane.astype(F32)
    group = lane // GROUP_SIZE
    neg = jnp.float32(-jnp.inf)

    def first_argmax(vals):
        mx = jnp.max(vals, axis=-1, keepdims=True)
        idx = jnp.min(jnp.where(vals == mx, lane_f, float(N_EXPERTS)), axis=-1, keepdims=True)
        return mx, idx

    gscore = []
    for g in range(N_EXPERT_GROUPS):
        vals = jnp.where(group == g, sel, neg)
        m1, i1 = first_argmax(vals)
        m2 = jnp.max(jnp.where(lane_f == i1, neg, vals), axis=-1, keepdims=True)
        gscore.append(m1 + m2)
    keep_f = jnp.zeros((tm, N_EXPERTS), F32)
    for g in range(N_EXPERT_GROUPS):
        beaten = jnp.zeros((tm, 1), F32)
        for h in range(N_EXPERT_GROUPS):
            if h == g:
                continue
            ahead = (gscore[h] >= gscore[g]) if h < g else (gscore[h] > gscore[g])
            beaten = beaten + jnp.where(ahead, 1.0, 0.0)
        gkeep = jnp.where(beaten < TOPK_GROUPS, 1.0, 0.0)
        keep_f = jnp.where(group == g, gkeep, keep_f)
    vals = jnp.where(keep_f > 0.5, sel, neg)

    lane_o = lax.broadcasted_iota(I32, (tm, LANES), 1)
    idx_out = jnp.zeros((tm, LANES), F32)
    gw_out = jnp.zeros((tm, LANES), F32)
    chosen = jnp.zeros((tm, N_EXPERTS), F32)
    picks = []
    wsum = jnp.zeros((tm, 1), F32)
    for k in range(TOP_K):
        _, ik = first_argmax(vals)
        hit = lane_f == ik
        sk = jnp.sum(jnp.where(hit, scores, 0.0), axis=-1, keepdims=True)
        vals = jnp.where(hit, neg, vals)
        chosen = jnp.where(hit, 1.0, chosen)
        picks.append((ik, sk))
        wsum = wsum + sk
        idx_out = jnp.where(lane_o == k, ik, idx_out)
    for k, (ik, sk) in enumerate(picks):
        gw_out = jnp.where(lane_o == k, sk / wsum * ROUTED_SCALE, gw_out)

    r = lax.broadcasted_iota(I32, (tm, tm), 0)
    c = lax.broadcasted_iota(I32, (tm, tm), 1)
    strict_lower = jnp.where(c < r, 1.0, 0.0).astype(BF16)
    before = jnp.dot(strict_lower, chosen.astype(BF16), preferred_element_type=F32) + run_ref[...]
    rank_out = jnp.zeros((tm, LANES), F32)
    for k, (ik, sk) in enumerate(picks):
        rk = jnp.sum(jnp.where(lane_f == ik, before, 0.0), axis=-1, keepdims=True)
        rank_out = jnp.where(lane_o == k, rk, rank_out)
    run_ref[...] = run_ref[...] + jnp.sum(chosen, axis=0, keepdims=True)

    idx_ref[...] = idx_out.astype(I32)
    gw_ref[...] = gw_out
    rank_ref[...] = rank_out.astype(I32)
    cnt_ref[...] = run_ref[...]


def _route(u2, w_router, router_bias):
    T, D = u2.shape
    row = lambda w: pl.BlockSpec((ROUTE_TM, w), lambda i: (i, 0))
    idx, gw, rank, cnt = pl.pallas_call(
        _route_kernel,
        grid=(T // ROUTE_TM,),
        in_specs=[row(D), pl.BlockSpec((D, N_EXPERTS), lambda i: (0, 0)),
                  pl.BlockSpec((1, N_EXPERTS), lambda i: (0, 0))],
        out_specs=[row(LANES), row(LANES), row(LANES), pl.BlockSpec((1, N_EXPERTS), lambda i: (0, 0))],
        out_shape=[jax.ShapeDtypeStruct((T, LANES), I32), jax.ShapeDtypeStruct((T, LANES), F32),
                   jax.ShapeDtypeStruct((T, LANES), I32), jax.ShapeDtypeStruct((1, N_EXPERTS), F32)],
        scratch_shapes=[pltpu.VMEM((1, N_EXPERTS), F32)],
        compiler_params=_cparams(("arbitrary",)),
        name="router",
    )(u2, w_router, router_bias.reshape(1, N_EXPERTS))
    return idx, gw, rank, cnt[0].astype(I32)


def _slot_kernel(idx_ref, rank_ref, start_ref, dest_ref):
    tm = ROUTE_TM
    lane_e = lax.broadcasted_iota(I32, (tm, N_EXPERTS), 1)
    lane_o = lax.broadcasted_iota(I32, (tm, LANES), 1)
    idx = idx_ref[...]
    start = start_ref[...]
    base = jnp.zeros((tm, LANES), F32)
    for k in range(TOP_K):
        sk = jnp.sum(jnp.where(lane_e == idx[:, k:k + 1], start, 0.0), axis=-1, keepdims=True)
        base = jnp.where(lane_o == k, sk, base)
    dest_ref[...] = base.astype(I32) + rank_ref[...]


def _slots(idx, rank, start):
    T = idx.shape[0]
    row = pl.BlockSpec((ROUTE_TM, LANES), lambda i: (i, 0))
    dest = pl.pallas_call(
        _slot_kernel,
        grid=(T // ROUTE_TM,),
        in_specs=[row, row, pl.BlockSpec((1, N_EXPERTS), lambda i: (0, 0))],
        out_specs=row,
        out_shape=jax.ShapeDtypeStruct((T, LANES), I32),
        compiler_params=_cparams(("parallel",)),
        name="slots",
    )(idx, rank, start.astype(F32).reshape(1, N_EXPERTS))
    return dest[:, :TOP_K]


MOE_META_SLOTS = 8
MOE_BUFS = 3
Y_TILE = 8
Y_TILE_SHIFT = Y_TILE.bit_length() - 1


def _moe_kernel(bexp_ref, wnext_ref, nused_ref, meta_hbm, u_hbm, wg_hbm, wu_hbm, wd_hbm, y_hbm,
                meta_smem, xb0, xb1, xb2, yb0, yb1, yb2, wg_st, wu_st, wd_st, wg_bf, wu_bf, wd_bf,
                sem_meta, sem_g, sem_s, sem_w):
    i = pl.program_id(0)
    n_blocks = pl.num_programs(0)
    nused = nused_ref[0]
    last = nused - 1
    xbufs = (xb0, xb1, xb2)
    ybufs = (yb0, yb1, yb2)

    def slot(blk):
        return blk & (MOE_META_SLOTS - 1)

    def meta_copy(blk, sl):
        row = pl.ds(pl.multiple_of(blk * (2 * MOE_BLOCK), 2 * MOE_BLOCK), 2 * MOE_BLOCK)
        return pltpu.make_async_copy(meta_hbm.at[row], meta_smem.at[sl], sem_meta.at[sl])

    def weight_copies(e):
        return (pltpu.make_async_copy(wg_hbm.at[e], wg_st, sem_w.at[0]),
                pltpu.make_async_copy(wu_hbm.at[e], wu_st, sem_w.at[1]),
                pltpu.make_async_copy(wd_hbm.at[e], wd_st, sem_w.at[2]))

    def gather_rows(sl, xdst, sem):
        for j in range(MOE_BLOCK):
            row0 = pl.multiple_of(meta_smem[sl, j] * ROW_SUB, ROW_SUB)
            pltpu.make_async_copy(u_hbm.at[pl.ds(row0, ROW_SUB)], xdst.at[pl.ds(j * ROW_PITCH, ROW_SUB)], sem).start()

    def scatter_rows(sl, ysrc, sem):
        for j in range(MOE_BLOCK):
            dst = meta_smem[sl, MOE_BLOCK + j]
            pltpu.make_async_copy(ysrc.at[pl.ds(j, 1)],
                                  y_hbm.at[dst >> Y_TILE_SHIFT, pl.ds(dst & (Y_TILE - 1), 1)], sem).start()

    def wait_gather(sem):
        n = MOE_BLOCK * ROW_SUB
        pltpu.make_async_copy(u_hbm.at[pl.ds(0, n)], xb0.at[pl.ds(0, n)], sem).wait()

    def wait_scatter(sem):
        pltpu.make_async_copy(yb0, yb1, sem).wait()

    @pl.when(i >= nused)
    def _():
        yb0[...] = jnp.zeros_like(yb0)
        tiles = MOE_BLOCK // Y_TILE
        for a in range(tiles):
            pltpu.make_async_copy(yb0.at[pl.ds(a * Y_TILE, Y_TILE)], y_hbm.at[i * tiles + a], sem_s.at[0]).start()
        wait_scatter(sem_s.at[0])

    @pl.when(i < nused)
    def _():
        @pl.when(i == 0)
        def _():
            for blk, sl in ((0, 0), (n_blocks, slot(-1)), (jnp.minimum(1, last), 1)):
                cp = meta_copy(blk, sl)
                cp.start()
                cp.wait()
            meta_copy(jnp.minimum(2, last), 2).start()
            yb2[...] = jnp.zeros_like(yb2)
            gather_rows(0, xb0, sem_g.at[0])
            gather_rows(1, xb1, sem_g.at[1])
            for cp in weight_copies(bexp_ref[0]):
                cp.start(priority=1)

        meta_copy(jnp.minimum(i + 2, last), slot(i + 2)).wait()
        meta_copy(jnp.minimum(i + 3, last), slot(i + 3)).start()

        @pl.when((i == 0) | (bexp_ref[i] != bexp_ref[jnp.maximum(i - 1, 0)]))
        def _():
            for cp in weight_copies(bexp_ref[i]):
                cp.wait()
            wg_bf[...] = wg_st[...].astype(BF16)
            wu_bf[...] = wu_st[...].astype(BF16)
            wd_bf[...] = wd_st[...].astype(BF16)
            nxt_e = wnext_ref[i]

            @pl.when(nxt_e >= 0)
            def _():
                for cp in weight_copies(nxt_e):
                    cp.start(priority=1)

    def step(r):
        r_prev = (r + MOE_BUFS - 1) % MOE_BUFS
        xcur, ycur = xbufs[r], ybufs[r]
        wait_gather(sem_g.at[r])
        gather_rows(slot(i + 2), xbufs[r_prev], sem_g.at[r_prev])
        scatter_rows(slot(i - 1), ybufs[r_prev], sem_s.at[r_prev])
        x = jnp.concatenate([xcur[pl.ds(cb, MOE_BLOCK, stride=ROW_PITCH), :].astype(BF16)
                             for cb in range(ROW_SUB)], axis=1)
        hg = jnp.dot(x, wg_bf[...], preferred_element_type=F32)
        hu = jnp.dot(x, wu_bf[...], preferred_element_type=F32)
        act = (hg * jax.nn.sigmoid(hg) * hu).astype(BF16)

        @pl.when(i >= 2)
        def _():
            wait_scatter(sem_s.at[r])

        ycur[...] = jnp.dot(act, wd_bf[...], preferred_element_type=F32)

        @pl.when(i == last)
        def _():
            scatter_rows(slot(i), ycur, sem_s.at[r])

    for r in range(MOE_BUFS):
        @pl.when((i < nused) & (i % MOE_BUFS == r))
        def _(r=r):
            step(r)

    @pl.when(i == last)
    def _():
        wait_scatter(sem_s.at[i % MOE_BUFS])
        wait_scatter(sem_s.at[(i + 2) % MOE_BUFS])

        @pl.when(i >= 1)
        def _():
            wait_scatter(sem_s.at[(i + 1) % MOE_BUFS])

        wait_gather(sem_g.at[(i + 1) % MOE_BUFS])
        wait_gather(sem_g.at[(i + 2) % MOE_BUFS])
        meta_copy(jnp.minimum(i + 3, last), slot(i + 3)).wait()


def _moe(u_rows, bexp, wnext, nused, meta, w_gate, w_up, w_down, n_pad):
    D = ROW_SUB * LANES
    n_blocks = n_pad // MOE_BLOCK
    E = EXPERT_DIM
    grid_spec = pltpu.PrefetchScalarGridSpec(
        num_scalar_prefetch=3,
        grid=(n_blocks,),
        in_specs=[pl.BlockSpec(memory_space=pl.ANY)] * 5,
        out_specs=pl.BlockSpec(memory_space=pl.ANY),
        scratch_shapes=[pltpu.SMEM((MOE_META_SLOTS, 2 * MOE_BLOCK), I32)]
                       + [pltpu.VMEM((MOE_BLOCK * ROW_PITCH, LANES), F32)] * MOE_BUFS
                       + [pltpu.VMEM((MOE_BLOCK, D), F32)] * MOE_BUFS + [
                        pltpu.VMEM((D, E), F32),
                        pltpu.VMEM((D, E), F32),
                        pltpu.VMEM((E, D), F32),
                        pltpu.VMEM((D, E), BF16),
                        pltpu.VMEM((D, E), BF16),
                        pltpu.VMEM((E, D), BF16),
                        pltpu.SemaphoreType.DMA((MOE_META_SLOTS,)),
                        pltpu.SemaphoreType.DMA((MOE_BUFS,)),
                        pltpu.SemaphoreType.DMA((MOE_BUFS,)),
                        pltpu.SemaphoreType.DMA((3,))])
    return pl.pallas_call(
        _moe_kernel,
        grid_spec=grid_spec,
        out_shape=jax.ShapeDtypeStruct(((n_pad + MOE_BLOCK) // Y_TILE, Y_TILE, D), F32),
        compiler_params=_cparams(("arbitrary",)),
        name="moe_experts",
    )(bexp, wnext, nused, meta, u_rows, w_gate, w_up, w_down)


def _dispatch_plan(idx, rank, counts, T):
    n_assign = T * TOP_K
    n_blocks = -(-(n_assign + N_EXPERTS * (MOE_BLOCK - 1)) // MOE_BLOCK)
    n_pad = n_blocks * MOE_BLOCK
    padded = (counts + MOE_BLOCK - 1) // MOE_BLOCK * MOE_BLOCK
    padded_end = jnp.cumsum(padded)
    start = padded_end - padded
    dest = _slots(idx, rank, start)
    src = jnp.full((n_pad,), -1, I32).at[dest.reshape(-1)].set(jnp.arange(n_assign, dtype=I32))
    is_pad = src < 0
    pad_rank = jnp.cumsum(is_pad.astype(I32)) - 1
    tok = jnp.where(is_pad, 0, src // TOP_K)
    dst = jnp.where(is_pad, n_assign + pad_rank, src)
    tok = jnp.concatenate([tok, jnp.zeros((MOE_BLOCK,), I32)]).reshape(n_blocks + 1, MOE_BLOCK)
    dst = jnp.concatenate([dst, n_pad + jnp.arange(MOE_BLOCK, dtype=I32)]).reshape(n_blocks + 1, MOE_BLOCK)
    meta = jnp.concatenate([tok, dst], axis=1).reshape(-1)
    block_row0 = jnp.arange(n_blocks, dtype=I32) * MOE_BLOCK
    bexp = jnp.minimum(jnp.sum((padded_end[None, :] <= block_row0[:, None]).astype(I32), axis=1),
                       N_EXPERTS - 1).astype(I32)
    nused = (padded_end[-1] // MOE_BLOCK).astype(I32).reshape(1)
    eid = jnp.arange(N_EXPERTS, dtype=I32)
    later = (eid[None, :] > eid[:, None]) & (counts[None, :] > 0)
    next_e = jnp.min(jnp.where(later, eid[None, :], N_EXPERTS), axis=1)
    onehot = (bexp[:, None] == eid[None, :]).astype(I32)
    wnext = jnp.sum(onehot * next_e[None, :], axis=1)
    wnext = jnp.where(wnext < N_EXPERTS, wnext, -1).astype(I32)
    return bexp, wnext, nused, meta.astype(I32), n_pad


def _final_kernel(u_ref, y_ref, gw_ref, x1_ref, wg_ref, wu_ref, wd_ref, gate_ref, lng_ref, lnb_ref, o_ref):
    tm = FINAL_TM
    nrow = tm * TOP_K
    u = u_ref[...].astype(BF16)
    hg = jnp.dot(u, wg_ref[...], preferred_element_type=F32)
    hu = jnp.dot(u, wu_ref[...], preferred_element_type=F32)
    act = (hg * jax.nn.sigmoid(hg) * hu).astype(BF16)
    ffn = jnp.dot(act, wd_ref[...], preferred_element_type=F32)
    expand = jnp.where((lax.broadcasted_iota(I32, (nrow, tm), 0) // TOP_K) == lax.broadcasted_iota(I32, (nrow, tm), 1),
                       1.0, 0.0)
    gw_rows = jnp.dot(expand, gw_ref[...], preferred_element_type=F32, precision=HIGHEST)
    pick = lax.broadcasted_iota(I32, (nrow, LANES), 1) == (lax.broadcasted_iota(I32, (nrow, LANES), 0) % TOP_K)
    wcol = jnp.sum(jnp.where(pick, gw_rows, 0.0), axis=-1, keepdims=True)
    yw = (y_ref[...] * wcol).astype(BF16)
    fold = jnp.where((lax.broadcasted_iota(I32, (tm, nrow), 1) // TOP_K) == lax.broadcasted_iota(I32, (tm, nrow), 0),
                     1.0, 0.0).astype(BF16)
    ffn = ffn + jnp.dot(fold, yw, preferred_element_type=F32)
    o_ref[...] = _layer_norm_rows(ALPHA * x1_ref[...] + gate_ref[...] * ffn, lng_ref[...], lnb_ref[...])


def _final(u2, y_flat, gw, x1, ws_gate_bf, ws_up_bf, ws_down_bf, gate2, ln_g, ln_b, S):
    T, D = u2.shape
    E = ws_gate_bf.shape[1]
    per_b = S // FINAL_TM
    assert Y_TILE == TOP_K
    row = lambda w: pl.BlockSpec((FINAL_TM, w), lambda i: (i, 0))
    const = lambda shape: pl.BlockSpec(shape, lambda i: (0,) * len(shape))
    return pl.pallas_call(
        _final_kernel,
        grid=(T // FINAL_TM,),
        in_specs=[row(D), pl.BlockSpec((FINAL_TM * TOP_K, D), lambda i: (i, 0)), row(LANES), row(D),
                  const((D, E)), const((D, E)), const((E, D)),
                  pl.BlockSpec((None, 1, D), lambda i: (i // per_b, 0, 0)), const((1, D)), const((1, D))],
        out_specs=row(D),
        out_shape=jax.ShapeDtypeStruct((T, D), F32),
        compiler_params=_cparams(("parallel",)),
        name="shared_combine_ln2",
    )(u2, y_flat.reshape(-1, D), gw, x1, ws_gate_bf, ws_up_bf, ws_down_bf, gate2, ln_g.reshape(1, D), ln_b.reshape(1, D))


def kernel(x, c, positions, w_ada, b_ada, w_in, lb_logits, attn_norm_g, hgrn_norm_g, w_out, ln1_g, ln1_b,
           w_router, router_bias, expert_w_gate, expert_w_up, expert_w_down, shared_w_gate, shared_w_up,
           shared_w_down, ln2_g, ln2_b):
    B, S, D = x.shape
    T = B * S
    layer = 0
    lower_bounds = jnp.cumsum(jax.nn.softmax(lb_logits.astype(F32), axis=1), axis=1)

    mod = _ada_mod(c, w_ada[layer], b_ada[layer])
    shift1, scale1, gate1, shift2, scale2, gate2 = [m.reshape(B, 1, D) for m in jnp.split(mod, 6, axis=-1)]
    cos_t, sin_t = _rope_tables(positions)
    x2 = x.reshape(T, D)

    proj, qkv4, qkv16 = _in_proj(x2, scale1, shift1, w_in[layer].astype(BF16), cos_t, sin_t, B, S)
    branches = [_attn_branch(qkv, d, B, S)
                for qkv, d in zip((proj.reshape(B, S, -1), qkv4, qkv16), DILATIONS)]
    rec = _hgrn(proj, lower_bounds[0, layer], lower_bounds[1, layer], hgrn_norm_g[layer], B, S)
    x1, u2, u_rows = _mix([o for o, _ in branches], [l for _, l in branches], rec, x2, attn_norm_g[layer],
                  w_out[layer].astype(BF16), gate1, ln1_g[layer], ln1_b[layer], scale2, shift2, S)

    idx, gw, rank, counts = _route(u2, w_router[layer], router_bias[layer])
    bexp, wnext, nused, meta, n_pad = _dispatch_plan(idx, rank, counts, T)
    y_flat = _moe(u_rows, bexp, wnext, nused, meta, expert_w_gate[layer], expert_w_up[layer], expert_w_down[layer], n_pad)
    out = _final(u2, y_flat, gw, x1, shared_w_gate[layer].astype(BF16), shared_w_up[layer].astype(BF16),
                 shared_w_down[layer].astype(BF16), gate2, ln2_g[layer], ln2_b[layer], S)
    return out.reshape(B, S, D)
```

```python
import functools

import jax
import jax.numpy as jnp
from jax import lax
from jax.experimental import pallas as pl
from jax.experimental.pallas import tpu as pltpu

F32 = jnp.float32
BF16 = jnp.bfloat16
I32 = jnp.int32
HIGHEST = lax.Precision.HIGHEST

D_MODEL = 2048
ATTN_HEADS = 16
ATTN_HEAD_DIM = 64
ATTN_WIDTH = ATTN_HEADS * ATTN_HEAD_DIM
HGRN_HEADS = 8
HGRN_HEAD_DIM = 128
HGRN_WIDTH = HGRN_HEADS * HGRN_HEAD_DIM
IN_PROJ_WIDTH = 3 * ATTN_WIDTH + 5 * HGRN_WIDTH
DILATIONS = (1, 4, 16)
ATTN_HALF = 64
ROPE_THETA = 10000.0
N_EXPERTS = 256
N_EXPERT_GROUPS = 8
GROUP_SIZE = N_EXPERTS // N_EXPERT_GROUPS
TOPK_GROUPS = 4
TOP_K = 8
EXPERT_DIM = 512
ROUTED_SCALE = 2.5
DEPTH = 1
ALPHA = (2 * DEPTH) ** 0.25
LN_EPS = 1e-5
RMS_EPS = 1e-6
NEG_INF = -1e30
LOG2E = 1.4426950408889634

LANES = 128
VMEM_LIMIT = 56 * 1024 * 1024

ADA_TN = 1024
ROPE_TM = 2048
INPROJ_TM = 512
INPROJ_TN = 1024
ATTN_TQ = 128
ATTN_WK = 256
ATTN_UNROLL = 4
HGRN_CHUNK = 64
HGRN_SUB = 16
HGRN_STATE_UNROLL = 4
HGRN_INTRA_UNROLL = 4
MIX_TM = 256
ROUTE_TM = 256
MOE_BLOCK = 128
ROW_SUB = D_MODEL // LANES
ROW_PITCH = 24
FINAL_TM = 128


def _cparams(sem):
    return pltpu.CompilerParams(dimension_semantics=sem, vmem_limit_bytes=VMEM_LIMIT)


def _ada_kernel(c_ref, w_ref, b_ref, o_ref):
    c = c_ref[...]
    sc = c * jax.nn.sigmoid(c)
    o_ref[...] = jnp.dot(sc, w_ref[...], preferred_element_type=F32, precision=HIGHEST) + b_ref[...]


def _ada_mod(c, w_ada, b_ada):
    B, D = c.shape
    N = w_ada.shape[1]
    c8 = jnp.zeros((8, D), F32).at[:B].set(c)
    out = pl.pallas_call(
        _ada_kernel,
        grid=(N // ADA_TN,),
        in_specs=[pl.BlockSpec((8, D), lambda j: (0, 0)),
                  pl.BlockSpec((D, ADA_TN), lambda j: (0, j)),
                  pl.BlockSpec((1, ADA_TN), lambda j: (0, j))],
        out_specs=pl.BlockSpec((8, ADA_TN), lambda j: (0, j)),
        out_shape=jax.ShapeDtypeStruct((8, N), F32),
        compiler_params=_cparams(("parallel",)),
        name="ada_mod",
    )(c8, w_ada, b_ada.reshape(1, N))
    return out[:B]


def _rope_kernel(pos_ref, invf_ref, sign_ref, cos_ref, sin_ref):
    ang = pos_ref[...].astype(F32) * invf_ref[...]
    cos_ref[...] = jnp.cos(ang)
    sin_ref[...] = jnp.sin(ang) * sign_ref[...]


def _rope_tables(positions):
    T = positions.size
    half = ATTN_HEAD_DIM // 2
    inv_freq = ROPE_THETA ** (-jnp.arange(half, dtype=F32) / half)
    lane = jnp.arange(LANES)
    invf = inv_freq[lane % half].reshape(1, LANES)
    sign = jnp.where((lane % ATTN_HEAD_DIM) < half, -1.0, 1.0).astype(F32).reshape(1, LANES)
    return pl.pallas_call(
        _rope_kernel,
        grid=(T // ROPE_TM,),
        in_specs=[pl.BlockSpec((ROPE_TM, 1), lambda i: (i, 0)),
                  pl.BlockSpec((1, LANES), lambda i: (0, 0)),
                  pl.BlockSpec((1, LANES), lambda i: (0, 0))],
        out_specs=[pl.BlockSpec((ROPE_TM, LANES), lambda i: (i, 0)),
                   pl.BlockSpec((ROPE_TM, LANES), lambda i: (i, 0))],
        out_shape=[jax.ShapeDtypeStruct((T, LANES), F32)] * 2,
        compiler_params=_cparams(("parallel",)),
        name="rope_tables",
    )(positions.reshape(T, 1), invf, sign)


def _inproj_kernel(x_ref, sc_ref, sh_ref, w_ref, cos_ref, sin_ref, o_ref, o4_ref, o16_ref, stage_ref):
    j = pl.program_id(1)
    u = (x_ref[...] * (1.0 + sc_ref[...]) + sh_ref[...]).astype(BF16)
    acc = jnp.dot(u, w_ref[...], preferred_element_type=F32)

    @pl.when(j < 2)
    def _():
        qscale = jnp.where(j == 0, ATTN_HEAD_DIM ** -0.5, 1.0).astype(F32)
        cos = cos_ref[...] * qscale
        sin = sin_ref[...] * qscale
        lane = lax.broadcasted_iota(I32, cos.shape, 1)
        first = (lane % ATTN_HEAD_DIM) < (ATTN_HEAD_DIM // 2)
        for cb in range(INPROJ_TN // LANES):
            a = acc[:, cb * LANES:(cb + 1) * LANES]
            partner = jnp.where(first, pltpu.roll(a, LANES - ATTN_HEAD_DIM // 2, 1),
                                pltpu.roll(a, ATTN_HEAD_DIM // 2, 1))
            stage_ref[cb] = a * cos + partner * sin

    @pl.when(j == 2)
    def _():
        for cb in range(INPROJ_TN // LANES):
            stage_ref[cb] = acc[:, cb * LANES:(cb + 1) * LANES]

    @pl.when(j < 3)
    def _():
        for cb in range(INPROJ_TN // LANES):
            o_ref[:, cb * LANES:(cb + 1) * LANES] = stage_ref[cb].astype(BF16)
            for d, od_ref in ((DILATIONS[1], o4_ref), (DILATIONS[2], o16_ref)):
                rows = INPROJ_TM // d
                for r in range(d):
                    c0 = r * INPROJ_TN + cb * LANES
                    od_ref[:, c0:c0 + LANES] = stage_ref[cb, pl.ds(r, rows, stride=d), :].astype(BF16)

    @pl.when(j >= 3)
    def _():
        o_ref[...] = acc.astype(BF16)


def _in_proj(x2, scale1, shift1, w_in_bf, cos_t, sin_t, B, S):
    T, D = x2.shape
    P = w_in_bf.shape[1]
    per_b = S // INPROJ_TM
    assert INPROJ_TN == ATTN_WIDTH
    d4, d16 = DILATIONS[1], DILATIONS[2]
    strided_spec = lambda d: pl.BlockSpec((None, INPROJ_TM // d, d * INPROJ_TN),
                                          lambda i, j: (i // per_b, i % per_b, jnp.minimum(j, 2)))
    strided_shape = lambda d: jax.ShapeDtypeStruct((B, S // d, 3 * d * ATTN_WIDTH), BF16)
    return pl.pallas_call(
        _inproj_kernel,
        grid=(T // INPROJ_TM, P // INPROJ_TN),
        in_specs=[pl.BlockSpec((INPROJ_TM, D), lambda i, j: (i, 0)),
                  pl.BlockSpec((None, 1, D), lambda i, j: (i // per_b, 0, 0)),
                  pl.BlockSpec((None, 1, D), lambda i, j: (i // per_b, 0, 0)),
                  pl.BlockSpec((D, INPROJ_TN), lambda i, j: (0, j)),
                  pl.BlockSpec((INPROJ_TM, LANES), lambda i, j: (i, 0)),
                  pl.BlockSpec((INPROJ_TM, LANES), lambda i, j: (i, 0))],
        out_specs=[pl.BlockSpec((INPROJ_TM, INPROJ_TN), lambda i, j: (i, j)), strided_spec(d4), strided_spec(d16)],
        out_shape=[jax.ShapeDtypeStruct((T, P), BF16), strided_shape(d4), strided_shape(d16)],
        scratch_shapes=[pltpu.VMEM((INPROJ_TN // LANES, INPROJ_TM, LANES), F32)],
        compiler_params=_cparams(("parallel", "arbitrary")),
        name="in_proj",
    )(x2, scale1, shift1, w_in_bf, cos_t, sin_t)


def _attn_kernel(q_ref, k_ref, v_ref, o_ref, lse_ref, *, L):
    nq = L // ATTN_TQ
    lane = lax.broadcasted_iota(I32, (1, LANES), 1)
    head0 = lane < ATTN_HEAD_DIM
    rel = (lax.broadcasted_iota(I32, (ATTN_TQ, ATTN_WK), 1)
           - lax.broadcasted_iota(I32, (ATTN_TQ, ATTN_WK), 0))

    def body(i, carry):
        q0 = pl.multiple_of(i * ATTN_TQ, ATTN_TQ)
        ks = pl.multiple_of(jnp.clip(i * ATTN_TQ - ATTN_HALF, 0, L - ATTN_WK), ATTN_HALF)
        q = q_ref[pl.ds(q0, ATTN_TQ), :]
        k = k_ref[pl.ds(ks, ATTN_WK), :]
        v = v_ref[pl.ds(ks, ATTN_WK), :]
        mask = jnp.abs(rel + (ks - q0)) <= ATTN_HALF
        outs, lses = [], []
        for hmask in (head0, jnp.logical_not(head0)):
            qh = jnp.where(hmask, q, jnp.zeros_like(q))
            s = lax.dot_general(qh, k, (((1,), (1,)), ((), ())), preferred_element_type=F32)
            s = jnp.where(mask, s, NEG_INF)
            m = jnp.max(s, axis=-1, keepdims=True)
            p = jnp.exp(s - m)
            l = jnp.sum(p, axis=-1, keepdims=True)
            outs.append(jnp.dot(p.astype(BF16), v, preferred_element_type=F32) / l)
            lses.append(m + jnp.log(l))
        o_ref[pl.ds(q0, ATTN_TQ), :] = jnp.where(head0, outs[0], outs[1]).astype(BF16)
        lse_ref[pl.ds(q0, ATTN_TQ), :] = jnp.where(head0, lses[0], lses[1])
        return carry

    lax.fori_loop(0, nq, body, 0, unroll=ATTN_UNROLL)


def _attn_branch(qkv, dilation, B, S):
    L = S // dilation
    pv = qkv
    acb = ATTN_WIDTH // LANES
    in_spec = lambda part: pl.BlockSpec((None, L, LANES),
                                        lambda b, r, h: (b, 0, (part * dilation + r) * acb + h))
    out_spec = pl.BlockSpec((None, L, LANES), lambda b, r, h: (b, 0, r * acb + h))
    o, lse = pl.pallas_call(
        functools.partial(_attn_kernel, L=L),
        grid=(B, dilation, acb),
        in_specs=[in_spec(0), in_spec(1), in_spec(2)],
        out_specs=[out_spec, out_spec],
        out_shape=[jax.ShapeDtypeStruct((B, L, dilation * ATTN_WIDTH), BF16),
                   jax.ShapeDtypeStruct((B, L, dilation * ATTN_WIDTH), F32)],
        compiler_params=_cparams(("parallel", "parallel", "parallel")),
        name=f"attn_d{dilation}",
    )(pv, pv, pv)
    return o.reshape(B * S, ATTN_WIDTH), lse.reshape(B * S, ATTN_WIDTH)


def _hgrn_chunk(q, kk, lf, v_bf, tri, reverse):
    C = HGRN_CHUNK
    SUB = HGRN_SUB
    nsub = C // SUB
    b = jnp.dot(tri, lf, preferred_element_type=F32, precision=HIGHEST) * LOG2E
    col = lax.broadcasted_iota(I32, (SUB, C), 1)
    row = lax.broadcasted_iota(I32, (SUB, C), 0)
    score_rows = []
    for i in range(nsub):
        r0 = i * SUB
        bi = b[r0:r0 + SUB]
        qi = q[r0:r0 + SUB]
        ki = kk[r0:r0 + SUB]
        diag = jnp.zeros((SUB, C), F32)
        for s in range(SUB):
            e = jnp.exp2(bi - bi[s:s + 1])
            colv = jnp.sum(qi * e * ki[s:s + 1], axis=-1, keepdims=True)
            diag = jnp.where(col == r0 + s, colv, diag)
        if reverse:
            keep = (col - r0) >= row
        else:
            keep = (col - r0) <= row
        diag = jnp.where(jnp.logical_and(keep, jnp.logical_and(col >= r0, col < r0 + SUB)), diag, 0.0)
        if reverse:
            has_off = i < nsub - 1
            bref = b[r0 + SUB:r0 + SUB + 1] if has_off else None
            off_mask = col >= r0 + SUB
        else:
            has_off = i > 0
            bref = b[r0 - 1:r0] if has_off else None
            off_mask = col < r0
        if has_off:
            qs = (qi * jnp.exp2(bi - bref)).astype(BF16)
            ks = (kk * jnp.exp2(bref - b)).astype(BF16)
            off = lax.dot_general(qs, ks, (((1,), (1,)), ((), ())), preferred_element_type=F32)
            score_rows.append(jnp.where(off_mask, off, diag))
        else:
            score_rows.append(diag)
    scores = jnp.concatenate(score_rows, axis=0)
    b_edge = b[0:1] if reverse else b[C - 1:C]
    o = jnp.dot(scores.astype(BF16), v_bf, preferred_element_type=F32)
    qd = (q * jnp.exp2(b)).astype(BF16)
    kd = (kk * jnp.exp2(b_edge - b)).astype(BF16)
    return o, qd, kd, jnp.exp2(b_edge)


def _hgrn_kernel(q_ref, zf_ref, zb_ref, v_ref, g_ref, lbf_ref, lbb_ref, ng_ref, o_ref,
                 acc_ref, qd_ref, kd_ref, dec_ref, *, S):
    C = HGRN_CHUNK
    n_chunks = S // C
    r = lax.broadcasted_iota(I32, (C, C), 0)
    c = lax.broadcasted_iota(I32, (C, C), 1)
    tri_f = (c <= r).astype(F32)
    tri_b = (c >= r).astype(F32)
    lbf = lbf_ref[...]
    lbb = lbb_ref[...]

    def gates(z, lb):
        sg = jax.nn.sigmoid(z)
        return jnp.log(lb + (1.0 - lb) * sg), (1.0 - lb) * (1.0 - sg)

    def intra_body(n, carry):
        r0 = pl.multiple_of(n * C, C)
        q = q_ref[pl.ds(r0, C), :].astype(F32)
        v_bf = v_ref[pl.ds(r0, C), :]
        total = None
        for d, (z_ref, lb, tri) in enumerate(((zf_ref, lbf, tri_f), (zb_ref, lbb, tri_b))):
            lf, kk = gates(z_ref[pl.ds(r0, C), :].astype(F32), lb)
            o, qd, kd, dec = _hgrn_chunk(q, kk, lf, v_bf, tri, d == 1)
            qd_ref[d, pl.ds(r0, C), :] = qd
            kd_ref[d, pl.ds(r0, C), :] = kd
            dec_ref[d, pl.ds(n, 1), :] = dec
            total = o if total is None else total + o
        acc_ref[pl.ds(r0, C), :] = total
        return carry

    lax.fori_loop(0, n_chunks, intra_body, 0, unroll=HGRN_INTRA_UNROLL)

    def state_body(n, carry):
        states = list(carry)
        for d, blk in enumerate((n, n_chunks - 1 - n)):
            r0 = pl.multiple_of(blk * C, C)
            st = states[d]
            o = lax.dot_general(qd_ref[d, pl.ds(r0, C), :], st.astype(BF16), (((1,), (1,)), ((), ())),
                                preferred_element_type=F32)
            acc_ref[pl.ds(r0, C), :] += o
            upd = lax.dot_general(v_ref[pl.ds(r0, C), :], kd_ref[d, pl.ds(r0, C), :], (((0,), (0,)), ((), ())),
                                  preferred_element_type=F32)
            states[d] = st * dec_ref[d, pl.ds(blk, 1), :] + upd
        return tuple(states)

    z0 = jnp.zeros((HGRN_HEAD_DIM, HGRN_HEAD_DIM), F32)
    lax.fori_loop(0, n_chunks, state_body, (z0, z0), unroll=HGRN_STATE_UNROLL)

    ng = ng_ref[...]

    def norm_body(n, carry):
        r0 = pl.multiple_of(n * 512, 512)
        t = acc_ref[pl.ds(r0, 512), :]
        g = g_ref[pl.ds(r0, 512), :].astype(F32)
        y = t * lax.rsqrt(jnp.mean(t * t, axis=-1, keepdims=True) + RMS_EPS) * ng
        o_ref[pl.ds(r0, 512), :] = (y * (g * jax.nn.sigmoid(g))).astype(BF16)
        return carry

    lax.fori_loop(0, S // 512, norm_body, 0)


def _hgrn(proj, lb_fwd, lb_bwd, norm_g, B, S):
    P = proj.shape[1]
    pv = proj.reshape(B, S, P)
    base = 3 * ATTN_WIDTH // LANES
    nh = HGRN_HEADS
    in_spec = lambda k: pl.BlockSpec((None, S, LANES), lambda b, h: (b, 0, base + k * nh + h))
    vec_spec = pl.BlockSpec((None, 1, LANES), lambda b, h: (h, 0, 0))
    out = pl.pallas_call(
        functools.partial(_hgrn_kernel, S=S),
        grid=(B, nh),
        in_specs=[in_spec(0), in_spec(1), in_spec(2), in_spec(3), in_spec(4), vec_spec, vec_spec, vec_spec],
        out_specs=pl.BlockSpec((None, S, LANES), lambda b, h: (b, 0, h)),
        out_shape=jax.ShapeDtypeStruct((B, S, HGRN_WIDTH), BF16),
        scratch_shapes=[pltpu.VMEM((S, LANES), F32),
                        pltpu.VMEM((2, S, LANES), BF16),
                        pltpu.VMEM((2, S, LANES), BF16),
                        pltpu.VMEM((2, S // HGRN_CHUNK, LANES), F32)],
        compiler_params=_cparams(("parallel", "parallel")),
        name="hgrn2",
    )(pv, pv, pv, pv, pv, lb_fwd.reshape(nh, 1, LANES), lb_bwd.reshape(nh, 1, LANES),
      norm_g.reshape(nh, 1, LANES))
    return out.reshape(B * S, HGRN_WIDTH)


def _layer_norm_rows(y, g, b):
    mu = jnp.mean(y, axis=-1, keepdims=True)
    d = y - mu
    var = jnp.mean(d * d, axis=-1, keepdims=True)
    return d * lax.rsqrt(var + LN_EPS) * g + b


def _mix_kernel(o1_ref, o2_ref, o3_ref, l1_ref, l2_ref, l3_ref, rec_ref, x_ref, grp_ref, ag_ref, w_ref,
                gate_ref, lng_ref, lnb_ref, sc_ref, sh_ref, x1_ref, u2_ref, urows_ref):
    l1, l2, l3 = l1_ref[...], l2_ref[...], l3_ref[...]
    m = jnp.maximum(jnp.maximum(l1, l2), l3)
    e1, e2, e3 = jnp.exp(l1 - m), jnp.exp(l2 - m), jnp.exp(l3 - m)
    attn = (e1 * o1_ref[...].astype(F32) + e2 * o2_ref[...].astype(F32)
            + e3 * o3_ref[...].astype(F32)) / (e1 + e2 + e3)
    ms = jnp.dot((attn * attn).astype(BF16), grp_ref[...], preferred_element_type=F32)
    normed = attn * lax.rsqrt(ms + RMS_EPS) * ag_ref[...]
    mixed = jnp.concatenate([normed.astype(BF16), rec_ref[...]], axis=-1)
    mix = jnp.dot(mixed, w_ref[...], preferred_element_type=F32)
    x1 = _layer_norm_rows(ALPHA * x_ref[...] + gate_ref[...] * mix, lng_ref[...], lnb_ref[...])
    x1_ref[...] = x1
    u2 = x1 * (1.0 + sc_ref[...]) + sh_ref[...]
    u2_ref[...] = u2
    for cb in range(ROW_SUB):
        urows_ref[pl.ds(cb, MIX_TM, stride=ROW_SUB), :] = u2[:, cb * LANES:(cb + 1) * LANES]


def _mix(o_branches, lse_branches, rec, x2, attn_norm_g, w_out_bf, gate1, ln_g, ln_b, scale2, shift2, S):
    T, D = x2.shape
    per_b = S // MIX_TM
    head = jnp.arange(ATTN_WIDTH) // ATTN_HEAD_DIM
    grp = jnp.where(head[:, None] == head[None, :], 1.0 / ATTN_HEAD_DIM, 0.0).astype(BF16)
    row = lambda w: pl.BlockSpec((MIX_TM, w), lambda i: (i, 0))
    const = lambda shape: pl.BlockSpec(shape, lambda i: (0,) * len(shape))
    per_batch = pl.BlockSpec((None, 1, D), lambda i: (i // per_b, 0, 0))
    return pl.pallas_call(
        _mix_kernel,
        grid=(T // MIX_TM,),
        in_specs=[row(ATTN_WIDTH)] * 6 + [row(HGRN_WIDTH), row(D), const((ATTN_WIDTH, ATTN_WIDTH)),
                  const((1, ATTN_WIDTH)), const((D, D)), per_batch, const((1, D)), const((1, D)),
                  per_batch, per_batch],
        out_specs=[row(D), row(D), pl.BlockSpec((MIX_TM * ROW_SUB, LANES), lambda i: (i, 0))],
        out_shape=[jax.ShapeDtypeStruct((T, D), F32)] * 2 + [jax.ShapeDtypeStruct((T * ROW_SUB, LANES), F32)],
        compiler_params=_cparams(("parallel",)),
        name="mix_out_ln1",
    )(*o_branches, *lse_branches, rec, x2, grp, attn_norm_g.reshape(1, -1), w_out_bf, gate1,
      ln_g.reshape(1, D), ln_b.reshape(1, D), scale2, shift2)


def _route_kernel(u_ref, w_ref, bias_ref, idx_ref, gw_ref, rank_ref, cnt_ref, run_ref):
    i = pl.program_id(0)

    @pl.when(i == 0)
    def _():
        run_ref[...] = jnp.zeros_like(run_ref)

    tm = ROUTE_TM
    logits = jnp.dot(u_ref[...], w_ref[...], preferred_element_type=F32, precision=HIGHEST)
    scores = jax.nn.sigmoid(logits)
    sel = scores + bias_ref[...]
    lane = lax.broadcasted_iota(I32, (tm, N_EXPERTS), 1)
    lane_f = lane.astype(F32)
    group = lane // GROUP_SIZE
    neg = jnp.float32(-jnp.inf)

    def first_argmax(vals):
        mx = jnp.max(vals, axis=-1, keepdims=True)
        idx = jnp.min(jnp.where(vals == mx, lane_f, float(N_EXPERTS)), axis=-1, keepdims=True)
        return mx, idx

    gscore = []
    for g in range(N_EXPERT_GROUPS):
        vals = jnp.where(group == g, sel, neg)
        m1, i1 = first_argmax(vals)
        m2 = jnp.max(jnp.where(lane_f == i1, neg, vals), axis=-1, keepdims=True)
        gscore.append(m1 + m2)
    keep_f = jnp.zeros((tm, N_EXPERTS), F32)
    for g in range(N_EXPERT_GROUPS):
        beaten = jnp.zeros((tm, 1), F32)
        for h in range(N_EXPERT_GROUPS):
            if h == g:
                continue
            ahead = (gscore[h] >= gscore[g]) if h < g else (gscore[h] > gscore[g])
            beaten = beaten + jnp.where(ahead, 1.0, 0.0)
        gkeep = jnp.where(beaten < TOPK_GROUPS, 1.0, 0.0)
        keep_f = jnp.where(group == g, gkeep, keep_f)
    vals = jnp.where(keep_f > 0.5, sel, neg)

    lane_o = lax.broadcasted_iota(I32, (tm, LANES), 1)
    idx_out = jnp.zeros((tm, LANES), F32)
    gw_out = jnp.zeros((tm, LANES), F32)
    chosen = jnp.zeros((tm, N_EXPERTS), F32)
    picks = []
    wsum = jnp.zeros((tm, 1), F32)
    for k in range(TOP_K):
        _, ik = first_argmax(vals)
        hit = lane_f == ik
        sk = jnp.sum(jnp.where(hit, scores, 0.0), axis=-1, keepdims=True)
        vals = jnp.where(hit, neg, vals)
        chosen = jnp.where(hit, 1.0, chosen)
        picks.append((ik, sk))
        wsum = wsum + sk
        idx_out = jnp.where(lane_o == k, ik, idx_out)
    for k, (ik, sk) in enumerate(picks):
        gw_out = jnp.where(lane_o == k, sk / wsum * ROUTED_SCALE, gw_out)

    r = lax.broadcasted_iota(I32, (tm, tm), 0)
    c = lax.broadcasted_iota(I32, (tm, tm), 1)
    strict_lower = jnp.where(c < r, 1.0, 0.0).astype(BF16)
    before = jnp.dot(strict_lower, chosen.astype(BF16), preferred_element_type=F32) + run_ref[...]
    rank_out = jnp.zeros((tm, LANES), F32)
    for k, (ik, sk) in enumerate(picks):
        rk = jnp.sum(jnp.where(lane_f == ik, before, 0.0), axis=-1, keepdims=True)
        rank_out = jnp.where(lane_o == k, rk, rank_out)
    run_ref[...] = run_ref[...] + jnp.sum(chosen, axis=0, keepdims=True)

    idx_ref[...] = idx_out.astype(I32)
    gw_ref[...] = gw_out
    rank_ref[...] = rank_out.astype(I32)
    cnt_ref[...] = run_ref[...]


def _route(u2, w_router, router_bias):
    T, D = u2.shape
    row = lambda w: pl.BlockSpec((ROUTE_TM, w), lambda i: (i, 0))
    idx, gw, rank, cnt = pl.pallas_call(
        _route_kernel,
        grid=(T // ROUTE_TM,),
        in_specs=[row(D), pl.BlockSpec((D, N_EXPERTS), lambda i: (0, 0)),
                  pl.BlockSpec((1, N_EXPERTS), lambda i: (0, 0))],
        out_specs=[row(LANES), row(LANES), row(LANES), pl.BlockSpec((1, N_EXPERTS), lambda i: (0, 0))],
        out_shape=[jax.ShapeDtypeStruct((T, LANES), I32), jax.ShapeDtypeStruct((T, LANES), F32),
                   jax.ShapeDtypeStruct((T, LANES), I32), jax.ShapeDtypeStruct((1, N_EXPERTS), F32)],
        scratch_shapes=[pltpu.VMEM((1, N_EXPERTS), F32)],
        compiler_params=_cparams(("arbitrary",)),
        name="router",
    )(u2, w_router, router_bias.reshape(1, N_EXPERTS))
    return idx, gw, rank, cnt[0].astype(I32)


def _slot_kernel(idx_ref, rank_ref, start_ref, dest_ref):
    tm = ROUTE_TM
    lane_e = lax.broadcasted_iota(I32, (tm, N_EXPERTS), 1)
    lane_o = lax.broadcasted_iota(I32, (tm, LANES), 1)
    idx = idx_ref[...]
    start = start_ref[...]
    base = jnp.zeros((tm, LANES), F32)
    for k in range(TOP_K):
        sk = jnp.sum(jnp.where(lane_e == idx[:, k:k + 1], start, 0.0), axis=-1, keepdims=True)
        base = jnp.where(lane_o == k, sk, base)
    dest_ref[...] = base.astype(I32) + rank_ref[...]


def _slots(idx, rank, start):
    T = idx.shape[0]
    row = pl.BlockSpec((ROUTE_TM, LANES), lambda i: (i, 0))
    dest = pl.pallas_call(
        _slot_kernel,
        grid=(T // ROUTE_TM,),
        in_specs=[row, row, pl.BlockSpec((1, N_EXPERTS), lambda i: (0, 0))],
        out_specs=row,
        out_shape=jax.ShapeDtypeStruct((T, LANES), I32),
        compiler_params=_cparams(("parallel",)),
        name="slots",
    )(idx, rank, start.astype(F32).reshape(1, N_EXPERTS))
    return dest[:, :TOP_K]


MOE_META_SLOTS = 8
MOE_BUFS = 3
Y_TILE = 8
Y_TILE_SHIFT = Y_TILE.bit_length() - 1


def _moe_kernel(bexp_ref, wnext_ref, nused_ref, meta_hbm, u_hbm, wg_hbm, wu_hbm, wd_hbm, y_hbm,
                meta_smem, xb0, xb1, xb2, yb0, yb1, yb2, wg_st, wu_st, wd_st, wg_bf, wu_bf, wd_bf,
                sem_meta, sem_g, sem_s, sem_w):
    i = pl.program_id(0)
    n_blocks = pl.num_programs(0)
    nused = nused_ref[0]
    last = nused - 1
    xbufs = (xb0, xb1, xb2)
    ybufs = (yb0, yb1, yb2)

    def slot(blk):
        return blk & (MOE_META_SLOTS - 1)

    def meta_copy(blk, sl):
        row = pl.ds(pl.multiple_of(blk * (2 * MOE_BLOCK), 2 * MOE_BLOCK), 2 * MOE_BLOCK)
        return pltpu.make_async_copy(meta_hbm.at[row], meta_smem.at[sl], sem_meta.at[sl])

    def weight_copies(e):
        return (pltpu.make_async_copy(wg_hbm.at[e], wg_st, sem_w.at[0]),
                pltpu.make_async_copy(wu_hbm.at[e], wu_st, sem_w.at[1]),
                pltpu.make_async_copy(wd_hbm.at[e], wd_st, sem_w.at[2]))

    def gather_rows(sl, xdst, sem):
        for j in range(MOE_BLOCK):
            row0 = pl.multiple_of(meta_smem[sl, j] * ROW_SUB, ROW_SUB)
            pltpu.make_async_copy(u_hbm.at[pl.ds(row0, ROW_SUB)], xdst.at[pl.ds(j * ROW_PITCH, ROW_SUB)], sem).start()

    def scatter_rows(sl, ysrc, sem):
        for j in range(MOE_BLOCK):
            dst = meta_smem[sl, MOE_BLOCK + j]
            pltpu.make_async_copy(ysrc.at[pl.ds(j, 1)],
                                  y_hbm.at[dst >> Y_TILE_SHIFT, pl.ds(dst & (Y_TILE - 1), 1)], sem).start()

    def wait_gather(sem):
        n = MOE_BLOCK * ROW_SUB
        pltpu.make_async_copy(u_hbm.at[pl.ds(0, n)], xb0.at[pl.ds(0, n)], sem).wait()

    def wait_scatter(sem):
        pltpu.make_async_copy(yb0, yb1, sem).wait()

    @pl.when(i >= nused)
    def _():
        @pl.when(i == nused)
        def _():
            yb0[...] = jnp.zeros_like(yb0)

        @pl.when(i > nused)
        def _():
            wait_scatter(sem_s.at[0])

        tiles = MOE_BLOCK // Y_TILE
        for a in range(tiles):
            pltpu.make_async_copy(yb0.at[pl.ds(a * Y_TILE, Y_TILE)], y_hbm.at[i * tiles + a], sem_s.at[0]).start()

        @pl.when(i == n_blocks - 1)
        def _():
            wait_scatter(sem_s.at[0])

    @pl.when(i < nused)
    def _():
        @pl.when(i == 0)
        def _():
            for blk, sl in ((0, 0), (n_blocks, slot(-1)), (jnp.minimum(1, last), 1)):
                cp = meta_copy(blk, sl)
                cp.start()
                cp.wait()
            meta_copy(jnp.minimum(2, last), 2).start()
            yb2[...] = jnp.zeros_like(yb2)
            gather_rows(0, xb0, sem_g.at[0])
            gather_rows(1, xb1, sem_g.at[1])
            for cp in weight_copies(bexp_ref[0]):
                cp.start(priority=1)

        meta_copy(jnp.minimum(i + 2, last), slot(i + 2)).wait()
        meta_copy(jnp.minimum(i + 3, last), slot(i + 3)).start()

        @pl.when((i == 0) | (bexp_ref[i] != bexp_ref[jnp.maximum(i - 1, 0)]))
        def _():
            for cp in weight_copies(bexp_ref[i]):
                cp.wait()
            wg_bf[...] = wg_st[...].astype(BF16)
            wu_bf[...] = wu_st[...].astype(BF16)
            wd_bf[...] = wd_st[...].astype(BF16)
            nxt_e = wnext_ref[i]

            @pl.when(nxt_e >= 0)
            def _():
                for cp in weight_copies(nxt_e):
                    cp.start(priority=1)

    def step(r):
        r_prev = (r + MOE_BUFS - 1) % MOE_BUFS
        xcur, ycur = xbufs[r], ybufs[r]
        wait_gather(sem_g.at[r])
        gather_rows(slot(i + 2), xbufs[r_prev], sem_g.at[r_prev])
        scatter_rows(slot(i - 1), ybufs[r_prev], sem_s.at[r_prev])
        x = jnp.concatenate([xcur[pl.ds(cb, MOE_BLOCK, stride=ROW_PITCH), :].astype(BF16)
                             for cb in range(ROW_SUB)], axis=1)
        hg = jnp.dot(x, wg_bf[...], preferred_element_type=F32)
        hu = jnp.dot(x, wu_bf[...], preferred_element_type=F32)
        act = (hg * jax.nn.sigmoid(hg) * hu).astype(BF16)

        @pl.when(i >= 2)
        def _():
            wait_scatter(sem_s.at[r])

        ycur[...] = jnp.dot(act, wd_bf[...], preferred_element_type=F32)

        @pl.when(i == last)
        def _():
            scatter_rows(slot(i), ycur, sem_s.at[r])

    for r in range(MOE_BUFS):
        @pl.when((i < nused) & (i % MOE_BUFS == r))
        def _(r=r):
            step(r)

    @pl.when(i == last)
    def _():
        wait_scatter(sem_s.at[i % MOE_BUFS])
        wait_scatter(sem_s.at[(i + 2) % MOE_BUFS])

        @pl.when(i >= 1)
        def _():
            wait_scatter(sem_s.at[(i + 1) % MOE_BUFS])

        wait_gather(sem_g.at[(i + 1) % MOE_BUFS])
        wait_gather(sem_g.at[(i + 2) % MOE_BUFS])
        meta_copy(jnp.minimum(i + 3, last), slot(i + 3)).wait()


def _moe(u_rows, bexp, wnext, nused, meta, w_gate, w_up, w_down, n_pad):
    D = ROW_SUB * LANES
    n_blocks = n_pad // MOE_BLOCK
    E = EXPERT_DIM
    grid_spec = pltpu.PrefetchScalarGridSpec(
        num_scalar_prefetch=3,
        grid=(n_blocks,),
        in_specs=[pl.BlockSpec(memory_space=pl.ANY)] * 5,
        out_specs=pl.BlockSpec(memory_space=pl.ANY),
        scratch_shapes=[pltpu.SMEM((MOE_META_SLOTS, 2 * MOE_BLOCK), I32)]
                       + [pltpu.VMEM((MOE_BLOCK * ROW_PITCH, LANES), F32)] * MOE_BUFS
                       + [pltpu.VMEM((MOE_BLOCK, D), F32)] * MOE_BUFS + [
                        pltpu.VMEM((D, E), F32),
                        pltpu.VMEM((D, E), F32),
                        pltpu.VMEM((E, D), F32),
                        pltpu.VMEM((D, E), BF16),
                        pltpu.VMEM((D, E), BF16),
                        pltpu.VMEM((E, D), BF16),
                        pltpu.SemaphoreType.DMA((MOE_META_SLOTS,)),
                        pltpu.SemaphoreType.DMA((MOE_BUFS,)),
                        pltpu.SemaphoreType.DMA((MOE_BUFS,)),
                        pltpu.SemaphoreType.DMA((3,))])
    return pl.pallas_call(
        _moe_kernel,
        grid_spec=grid_spec,
        out_shape=jax.ShapeDtypeStruct(((n_pad + MOE_BLOCK) // Y_TILE, Y_TILE, D), F32),
        compiler_params=_cparams(("arbitrary",)),
        name="moe_experts",
    )(bexp, wnext, nused, meta, u_rows, w_gate, w_up, w_down)


def _dispatch_plan(idx, rank, counts, T):
    n_assign = T * TOP_K
    n_blocks = -(-(n_assign + N_EXPERTS * (MOE_BLOCK - 1)) // MOE_BLOCK)
    n_pad = n_blocks * MOE_BLOCK
    padded = (counts + MOE_BLOCK - 1) // MOE_BLOCK * MOE_BLOCK
    padded_end = jnp.cumsum(padded)
    start = padded_end - padded
    dest = _slots(idx, rank, start)
    src = jnp.full((n_pad,), -1, I32).at[dest.reshape(-1)].set(jnp.arange(n_assign, dtype=I32))
    is_pad = src < 0
    pad_rank = jnp.cumsum(is_pad.astype(I32)) - 1
    tok = jnp.where(is_pad, 0, src // TOP_K)
    dst = jnp.where(is_pad, n_assign + pad_rank, src)
    tok = jnp.concatenate([tok, jnp.zeros((MOE_BLOCK,), I32)]).reshape(n_blocks + 1, MOE_BLOCK)
    dst = jnp.concatenate([dst, n_pad + jnp.arange(MOE_BLOCK, dtype=I32)]).reshape(n_blocks + 1, MOE_BLOCK)
    meta = jnp.concatenate([tok, dst], axis=1).reshape(-1)
    block_row0 = jnp.arange(n_blocks, dtype=I32) * MOE_BLOCK
    bexp = jnp.minimum(jnp.sum((padded_end[None, :] <= block_row0[:, None]).astype(I32), axis=1),
                       N_EXPERTS - 1).astype(I32)
    nused = (padded_end[-1] // MOE_BLOCK).astype(I32).reshape(1)
    eid = jnp.arange(N_EXPERTS, dtype=I32)
    later = (eid[None, :] > eid[:, None]) & (counts[None, :] > 0)
    next_e = jnp.min(jnp.where(later, eid[None, :], N_EXPERTS), axis=1)
    onehot = (bexp[:, None] == eid[None, :]).astype(I32)
    wnext = jnp.sum(onehot * next_e[None, :], axis=1)
    wnext = jnp.where(wnext < N_EXPERTS, wnext, -1).astype(I32)
    return bexp, wnext, nused, meta.astype(I32), n_pad


def _final_kernel(u_ref, y_ref, gw_ref, x1_ref, wg_ref, wu_ref, wd_ref, gate_ref, lng_ref, lnb_ref, o_ref):
    tm = FINAL_TM
    nrow = tm * TOP_K
    u = u_ref[...].astype(BF16)
    hg = jnp.dot(u, wg_ref[...], preferred_element_type=F32)
    hu = jnp.dot(u, wu_ref[...], preferred_element_type=F32)
    act = (hg * jax.nn.sigmoid(hg) * hu).astype(BF16)
    ffn = jnp.dot(act, wd_ref[...], preferred_element_type=F32)
    expand = jnp.where((lax.broadcasted_iota(I32, (nrow, tm), 0) // TOP_K) == lax.broadcasted_iota(I32, (nrow, tm), 1),
                       1.0, 0.0)
    gw_rows = jnp.dot(expand, gw_ref[...], preferred_element_type=F32, precision=HIGHEST)
    pick = lax.broadcasted_iota(I32, (nrow, LANES), 1) == (lax.broadcasted_iota(I32, (nrow, LANES), 0) % TOP_K)
    wcol = jnp.sum(jnp.where(pick, gw_rows, 0.0), axis=-1, keepdims=True)
    yw = (y_ref[...] * wcol).astype(BF16)
    fold = jnp.where((lax.broadcasted_iota(I32, (tm, nrow), 1) // TOP_K) == lax.broadcasted_iota(I32, (tm, nrow), 0),
                     1.0, 0.0).astype(BF16)
    ffn = ffn + jnp.dot(fold, yw, preferred_element_type=F32)
    o_ref[...] = _layer_norm_rows(ALPHA * x1_ref[...] + gate_ref[...] * ffn, lng_ref[...], lnb_ref[...])


def _final(u2, y_flat, gw, x1, ws_gate_bf, ws_up_bf, ws_down_bf, gate2, ln_g, ln_b, S):
    T, D = u2.shape
    E = ws_gate_bf.shape[1]
    per_b = S // FINAL_TM
    assert Y_TILE == TOP_K
    row = lambda w: pl.BlockSpec((FINAL_TM, w), lambda i: (i, 0))
    const = lambda shape: pl.BlockSpec(shape, lambda i: (0,) * len(shape))
    return pl.pallas_call(
        _final_kernel,
        grid=(T // FINAL_TM,),
        in_specs=[row(D), pl.BlockSpec((FINAL_TM * TOP_K, D), lambda i: (i, 0)), row(LANES), row(D),
                  const((D, E)), const((D, E)), const((E, D)),
                  pl.BlockSpec((None, 1, D), lambda i: (i // per_b, 0, 0)), const((1, D)), const((1, D))],
        out_specs=row(D),
        out_shape=jax.ShapeDtypeStruct((T, D), F32),
        compiler_params=_cparams(("parallel",)),
        name="shared_combine_ln2",
    )(u2, y_flat.reshape(-1, D), gw, x1, ws_gate_bf, ws_up_bf, ws_down_bf, gate2, ln_g.reshape(1, D), ln_b.reshape(1, D))


def kernel(x, c, positions, w_ada, b_ada, w_in, lb_logits, attn_norm_g, hgrn_norm_g, w_out, ln1_g, ln1_b,
           w_router, router_bias, expert_w_gate, expert_w_up, expert_w_down, shared_w_gate, shared_w_up,
           shared_w_down, ln2_g, ln2_b):
    B, S, D = x.shape
    T = B * S
    layer = 0
    lower_bounds = jnp.cumsum(jax.nn.softmax(lb_logits.astype(F32), axis=1), axis=1)

    mod = _ada_mod(c, w_ada[layer], b_ada[layer])
    shift1, scale1, gate1, shift2, scale2, gate2 = [m.reshape(B, 1, D) for m in jnp.split(mod, 6, axis=-1)]
    cos_t, sin_t = _rope_tables(positions)
    x2 = x.reshape(T, D)

    proj, qkv4, qkv16 = _in_proj(x2, scale1, shift1, w_in[layer].astype(BF16), cos_t, sin_t, B, S)
    branches = [_attn_branch(qkv, d, B, S)
                for qkv, d in zip((proj.reshape(B, S, -1), qkv4, qkv16), DILATIONS)]
    rec = _hgrn(proj, lower_bounds[0, layer], lower_bounds[1, layer], hgrn_norm_g[layer], B, S)
    x1, u2, u_rows = _mix([o for o, _ in branches], [l for _, l in branches], rec, x2, attn_norm_g[layer],
                  w_out[layer].astype(BF16), gate1, ln1_g[layer], ln1_b[layer], scale2, shift2, S)

    idx, gw, rank, counts = _route(u2, w_router[layer], router_bias[layer])
    bexp, wnext, nused, meta, n_pad = _dispatch_plan(idx, rank, counts, T)
    y_flat = _moe(u_rows, bexp, wnext, nused, meta, expert_w_gate[layer], expert_w_up[layer], expert_w_down[layer], n_pad)
    out = _final(u2, y_flat, gw, x1, shared_w_gate[layer].astype(BF16), shared_w_up[layer].astype(BF16),
                 shared_w_down[layer].astype(BF16), gate2, ln2_g[layer], ln2_b[layer], S)
    return out.reshape(B, S, D)
```

```python
import functools

import jax
import jax.numpy as jnp
from jax import lax
from jax.experimental import pallas as pl
from jax.experimental.pallas import tpu as pltpu

F32 = jnp.float32
BF16 = jnp.bfloat16
I32 = jnp.int32
HIGHEST = lax.Precision.HIGHEST

D_MODEL = 2048
ATTN_HEADS = 16
ATTN_HEAD_DIM = 64
ATTN_WIDTH = ATTN_HEADS * ATTN_HEAD_DIM
HGRN_HEADS = 8
HGRN_HEAD_DIM = 128
HGRN_WIDTH = HGRN_HEADS * HGRN_HEAD_DIM
IN_PROJ_WIDTH = 3 * ATTN_WIDTH + 5 * HGRN_WIDTH
DILATIONS = (1, 4, 16)
ATTN_HALF = 64
ROPE_THETA = 10000.0
N_EXPERTS = 256
N_EXPERT_GROUPS = 8
GROUP_SIZE = N_EXPERTS // N_EXPERT_GROUPS
TOPK_GROUPS = 4
TOP_K = 8
EXPERT_DIM = 512
ROUTED_SCALE = 2.5
DEPTH = 1
ALPHA = (2 * DEPTH) ** 0.25
LN_EPS = 1e-5
RMS_EPS = 1e-6
NEG_INF = -1e30
LOG2E = 1.4426950408889634

LANES = 128
VMEM_LIMIT = 56 * 1024 * 1024

ADA_TN = 1024
ROPE_TM = 2048
INPROJ_TM = 512
INPROJ_TN = 1024
ATTN_TQ = 128
ATTN_WK = 256
ATTN_UNROLL = 4
HGRN_CHUNK = 64
HGRN_SUB = 16
HGRN_STATE_UNROLL = 4
HGRN_INTRA_UNROLL = 4
MIX_TM = 256
ROUTE_TM = 256
MOE_BLOCK = 128
ROW_SUB = D_MODEL // LANES
ROW_PITCH = 24
FINAL_TM = 128


def _cparams(sem):
    return pltpu.CompilerParams(dimension_semantics=sem, vmem_limit_bytes=VMEM_LIMIT)


def _ada_kernel(c_ref, w_ref, b_ref, o_ref):
    c = c_ref[...]
    sc = c * jax.nn.sigmoid(c)
    o_ref[...] = jnp.dot(sc, w_ref[...], preferred_element_type=F32, precision=HIGHEST) + b_ref[...]


def _ada_mod(c, w_ada, b_ada):
    B, D = c.shape
    N = w_ada.shape[1]
    c8 = jnp.zeros((8, D), F32).at[:B].set(c)
    out = pl.pallas_call(
        _ada_kernel,
        grid=(N // ADA_TN,),
        in_specs=[pl.BlockSpec((8, D), lambda j: (0, 0)),
                  pl.BlockSpec((D, ADA_TN), lambda j: (0, j)),
                  pl.BlockSpec((1, ADA_TN), lambda j: (0, j))],
        out_specs=pl.BlockSpec((8, ADA_TN), lambda j: (0, j)),
        out_shape=jax.ShapeDtypeStruct((8, N), F32),
        compiler_params=_cparams(("parallel",)),
        name="ada_mod",
    )(c8, w_ada, b_ada.reshape(1, N))
    return out[:B]


def _rope_kernel(pos_ref, invf_ref, sign_ref, cos_ref, sin_ref):
    ang = pos_ref[...].astype(F32) * invf_ref[...]
    cos_ref[...] = jnp.cos(ang)
    sin_ref[...] = jnp.sin(ang) * sign_ref[...]


def _rope_tables(positions):
    T = positions.size
    half = ATTN_HEAD_DIM // 2
    inv_freq = ROPE_THETA ** (-jnp.arange(half, dtype=F32) / half)
    lane = jnp.arange(LANES)
    invf = inv_freq[lane % half].reshape(1, LANES)
    sign = jnp.where((lane % ATTN_HEAD_DIM) < half, -1.0, 1.0).astype(F32).reshape(1, LANES)
    return pl.pallas_call(
        _rope_kernel,
        grid=(T // ROPE_TM,),
        in_specs=[pl.BlockSpec((ROPE_TM, 1), lambda i: (i, 0)),
                  pl.BlockSpec((1, LANES), lambda i: (0, 0)),
                  pl.BlockSpec((1, LANES), lambda i: (0, 0))],
        out_specs=[pl.BlockSpec((ROPE_TM, LANES), lambda i: (i, 0)),
                   pl.BlockSpec((ROPE_TM, LANES), lambda i: (i, 0))],
        out_shape=[jax.ShapeDtypeStruct((T, LANES), F32)] * 2,
        compiler_params=_cparams(("parallel",)),
        name="rope_tables",
    )(positions.reshape(T, 1), invf, sign)


def _inproj_kernel(x_ref, sc_ref, sh_ref, w_ref, cos_ref, sin_ref, o_ref, o4_ref, o16_ref, stage_ref):
    j = pl.program_id(1)
    u = (x_ref[...] * (1.0 + sc_ref[...]) + sh_ref[...]).astype(BF16)
    acc = jnp.dot(u, w_ref[...], preferred_element_type=F32)

    @pl.when(j < 2)
    def _():
        qscale = jnp.where(j == 0, ATTN_HEAD_DIM ** -0.5, 1.0).astype(F32)
        cos = cos_ref[...] * qscale
        sin = sin_ref[...] * qscale
        lane = lax.broadcasted_iota(I32, cos.shape, 1)
        first = (lane % ATTN_HEAD_DIM) < (ATTN_HEAD_DIM // 2)
        for cb in range(INPROJ_TN // LANES):
            a = acc[:, cb * LANES:(cb + 1) * LANES]
            partner = jnp.where(first, pltpu.roll(a, LANES - ATTN_HEAD_DIM // 2, 1),
                                pltpu.roll(a, ATTN_HEAD_DIM // 2, 1))
            stage_ref[cb] = a * cos + partner * sin

    @pl.when(j == 2)
    def _():
        for cb in range(INPROJ_TN // LANES):
            stage_ref[cb] = acc[:, cb * LANES:(cb + 1) * LANES]

    @pl.when(j < 3)
    def _():
        for cb in range(INPROJ_TN // LANES):
            o_ref[:, cb * LANES:(cb + 1) * LANES] = stage_ref[cb].astype(BF16)
            for d, od_ref in ((DILATIONS[1], o4_ref), (DILATIONS[2], o16_ref)):
                rows = INPROJ_TM // d
                for r in range(d):
                    c0 = r * INPROJ_TN + cb * LANES
                    od_ref[:, c0:c0 + LANES] = stage_ref[cb, pl.ds(r, rows, stride=d), :].astype(BF16)

    @pl.when(j >= 3)
    def _():
        o_ref[...] = acc.astype(BF16)


def _in_proj(x2, scale1, shift1, w_in_bf, cos_t, sin_t, B, S):
    T, D = x2.shape
    P = w_in_bf.shape[1]
    per_b = S // INPROJ_TM
    assert INPROJ_TN == ATTN_WIDTH
    d4, d16 = DILATIONS[1], DILATIONS[2]
    strided_spec = lambda d: pl.BlockSpec((None, INPROJ_TM // d, d * INPROJ_TN),
                                          lambda i, j: (i // per_b, i % per_b, jnp.minimum(j, 2)))
    strided_shape = lambda d: jax.ShapeDtypeStruct((B, S // d, 3 * d * ATTN_WIDTH), BF16)
    return pl.pallas_call(
        _inproj_kernel,
        grid=(T // INPROJ_TM, P // INPROJ_TN),
        in_specs=[pl.BlockSpec((INPROJ_TM, D), lambda i, j: (i, 0)),
                  pl.BlockSpec((None, 1, D), lambda i, j: (i // per_b, 0, 0)),
                  pl.BlockSpec((None, 1, D), lambda i, j: (i // per_b, 0, 0)),
                  pl.BlockSpec((D, INPROJ_TN), lambda i, j: (0, j)),
                  pl.BlockSpec((INPROJ_TM, LANES), lambda i, j: (i, 0)),
                  pl.BlockSpec((INPROJ_TM, LANES), lambda i, j: (i, 0))],
        out_specs=[pl.BlockSpec((INPROJ_TM, INPROJ_TN), lambda i, j: (i, j)), strided_spec(d4), strided_spec(d16)],
        out_shape=[jax.ShapeDtypeStruct((T, P), BF16), strided_shape(d4), strided_shape(d16)],
        scratch_shapes=[pltpu.VMEM((INPROJ_TN // LANES, INPROJ_TM, LANES), F32)],
        compiler_params=_cparams(("parallel", "arbitrary")),
        name="in_proj",
    )(x2, scale1, shift1, w_in_bf, cos_t, sin_t)


def _attn_kernel(q_ref, k_ref, v_ref, o_ref, lse_ref, *, L, dilation):
    nq = L // ATTN_TQ
    res = pl.program_id(2)
    lane = lax.broadcasted_iota(I32, (1, LANES), 1)
    head0 = lane < ATTN_HEAD_DIM
    rel = (lax.broadcasted_iota(I32, (ATTN_TQ, ATTN_WK), 1)
           - lax.broadcasted_iota(I32, (ATTN_TQ, ATTN_WK), 0))

    def body(i, carry):
        q0 = pl.multiple_of(i * ATTN_TQ, ATTN_TQ)
        ks = pl.multiple_of(jnp.clip(i * ATTN_TQ - ATTN_HALF, 0, L - ATTN_WK), ATTN_HALF)
        q = q_ref[pl.ds(q0, ATTN_TQ), :]
        k = k_ref[pl.ds(ks, ATTN_WK), :]
        v = v_ref[pl.ds(ks, ATTN_WK), :]
        mask = jnp.abs(rel + (ks - q0)) <= ATTN_HALF
        outs, lses = [], []
        for hmask in (head0, jnp.logical_not(head0)):
            qh = jnp.where(hmask, q, jnp.zeros_like(q))
            s = lax.dot_general(qh, k, (((1,), (1,)), ((), ())), preferred_element_type=F32)
            s = jnp.where(mask, s, NEG_INF)
            m = jnp.max(s, axis=-1, keepdims=True)
            p = jnp.exp(s - m)
            l = jnp.sum(p, axis=-1, keepdims=True)
            outs.append(jnp.dot(p.astype(BF16), v, preferred_element_type=F32) / l)
            lses.append(m + jnp.log(l))
        if dilation == 1:
            rows = pl.ds(q0, ATTN_TQ)
        else:
            rows = pl.ds(q0 * dilation + res, ATTN_TQ, stride=dilation)
        o_ref[rows, :] = jnp.where(head0, outs[0], outs[1]).astype(o_ref.dtype)
        lse_ref[rows, :] = jnp.where(head0, lses[0], lses[1])
        return carry

    lax.fori_loop(0, nq, body, 0, unroll=ATTN_UNROLL)


def _attn_branch(qkv, dilation, B, S):
    L = S // dilation
    acb = ATTN_WIDTH // LANES
    in_spec = lambda part: pl.BlockSpec((None, L, LANES),
                                        lambda b, h, r: (b, 0, (part * dilation + r) * acb + h))
    out_spec = pl.BlockSpec((None, S, LANES), lambda b, h, r: (b, 0, h))
    o_dtype = BF16 if dilation == 1 else F32
    o, lse = pl.pallas_call(
        functools.partial(_attn_kernel, L=L, dilation=dilation),
        grid=(B, acb, dilation),
        in_specs=[in_spec(0), in_spec(1), in_spec(2)],
        out_specs=[out_spec, out_spec],
        out_shape=[jax.ShapeDtypeStruct((B, S, ATTN_WIDTH), o_dtype),
                   jax.ShapeDtypeStruct((B, S, ATTN_WIDTH), F32)],
        compiler_params=_cparams(("parallel", "parallel", "arbitrary")),
        name=f"attn_d{dilation}",
    )(qkv, qkv, qkv)
    return o.reshape(B * S, ATTN_WIDTH), lse.reshape(B * S, ATTN_WIDTH)


def _hgrn_chunk(q, kk, lf, v_bf, tri, reverse):
    C = HGRN_CHUNK
    SUB = HGRN_SUB
    nsub = C // SUB
    b = jnp.dot(tri, lf, preferred_element_type=F32, precision=HIGHEST) * LOG2E
    col = lax.broadcasted_iota(I32, (SUB, C), 1)
    row = lax.broadcasted_iota(I32, (SUB, C), 0)
    score_rows = []
    for i in range(nsub):
        r0 = i * SUB
        bi = b[r0:r0 + SUB]
        qi = q[r0:r0 + SUB]
        ki = kk[r0:r0 + SUB]
        diag = jnp.zeros((SUB, C), F32)
        for s in range(SUB):
            e = jnp.exp2(bi - bi[s:s + 1])
            colv = jnp.sum(qi * e * ki[s:s + 1], axis=-1, keepdims=True)
            diag = jnp.where(col == r0 + s, colv, diag)
        if reverse:
            keep = (col - r0) >= row
        else:
            keep = (col - r0) <= row
        diag = jnp.where(jnp.logical_and(keep, jnp.logical_and(col >= r0, col < r0 + SUB)), diag, 0.0)
        if reverse:
            has_off = i < nsub - 1
            bref = b[r0 + SUB:r0 + SUB + 1] if has_off else None
            off_mask = col >= r0 + SUB
        else:
            has_off = i > 0
            bref = b[r0 - 1:r0] if has_off else None
            off_mask = col < r0
        if has_off:
            qs = (qi * jnp.exp2(bi - bref)).astype(BF16)
            ks = (kk * jnp.exp2(bref - b)).astype(BF16)
            off = lax.dot_general(qs, ks, (((1,), (1,)), ((), ())), preferred_element_type=F32)
            score_rows.append(jnp.where(off_mask, off, diag))
        else:
            score_rows.append(diag)
    scores = jnp.concatenate(score_rows, axis=0)
    b_edge = b[0:1] if reverse else b[C - 1:C]
    o = jnp.dot(scores.astype(BF16), v_bf, preferred_element_type=F32)
    qd = (q * jnp.exp2(b)).astype(BF16)
    kd = (kk * jnp.exp2(b_edge - b)).astype(BF16)
    return o, qd, kd, jnp.exp2(b_edge)


def _hgrn_kernel(q_ref, zf_ref, zb_ref, v_ref, g_ref, lbf_ref, lbb_ref, ng_ref, o_ref,
                 acc_ref, qd_ref, kd_ref, dec_ref, *, S):
    C = HGRN_CHUNK
    n_chunks = S // C
    r = lax.broadcasted_iota(I32, (C, C), 0)
    c = lax.broadcasted_iota(I32, (C, C), 1)
    tri_f = (c <= r).astype(F32)
    tri_b = (c >= r).astype(F32)
    lbf = lbf_ref[...]
    lbb = lbb_ref[...]

    def gates(z, lb):
        sg = jax.nn.sigmoid(z)
        return jnp.log(lb + (1.0 - lb) * sg), (1.0 - lb) * (1.0 - sg)

    def intra_body(n, carry):
        r0 = pl.multiple_of(n * C, C)
        q = q_ref[pl.ds(r0, C), :].astype(F32)
        v_bf = v_ref[pl.ds(r0, C), :]
        total = None
        for d, (z_ref, lb, tri) in enumerate(((zf_ref, lbf, tri_f), (zb_ref, lbb, tri_b))):
            lf, kk = gates(z_ref[pl.ds(r0, C), :].astype(F32), lb)
            o, qd, kd, dec = _hgrn_chunk(q, kk, lf, v_bf, tri, d == 1)
            qd_ref[d, pl.ds(r0, C), :] = qd
            kd_ref[d, pl.ds(r0, C), :] = kd
            dec_ref[d, pl.ds(n, 1), :] = dec
            total = o if total is None else total + o
        acc_ref[pl.ds(r0, C), :] = total
        return carry

    lax.fori_loop(0, n_chunks, intra_body, 0, unroll=HGRN_INTRA_UNROLL)

    def state_body(n, carry):
        states = list(carry)
        for d, blk in enumerate((n, n_chunks - 1 - n)):
            r0 = pl.multiple_of(blk * C, C)
            st = states[d]
            o = lax.dot_general(qd_ref[d, pl.ds(r0, C), :], st.astype(BF16), (((1,), (1,)), ((), ())),
                                preferred_element_type=F32)
            acc_ref[pl.ds(r0, C), :] += o
            upd = lax.dot_general(v_ref[pl.ds(r0, C), :], kd_ref[d, pl.ds(r0, C), :], (((0,), (0,)), ((), ())),
                                  preferred_element_type=F32)
            states[d] = st * dec_ref[d, pl.ds(blk, 1), :] + upd
        return tuple(states)

    z0 = jnp.zeros((HGRN_HEAD_DIM, HGRN_HEAD_DIM), F32)
    lax.fori_loop(0, n_chunks, state_body, (z0, z0), unroll=HGRN_STATE_UNROLL)

    ng = ng_ref[...]

    def norm_body(n, carry):
        r0 = pl.multiple_of(n * 512, 512)
        t = acc_ref[pl.ds(r0, 512), :]
        g = g_ref[pl.ds(r0, 512), :].astype(F32)
        y = t * lax.rsqrt(jnp.mean(t * t, axis=-1, keepdims=True) + RMS_EPS) * ng
        o_ref[pl.ds(r0, 512), :] = (y * (g * jax.nn.sigmoid(g))).astype(BF16)
        return carry

    lax.fori_loop(0, S // 512, norm_body, 0)


def _hgrn(proj, lb_fwd, lb_bwd, norm_g, B, S):
    P = proj.shape[1]
    pv = proj.reshape(B, S, P)
    base = 3 * ATTN_WIDTH // LANES
    nh = HGRN_HEADS
    in_spec = lambda k: pl.BlockSpec((None, S, LANES), lambda b, h: (b, 0, base + k * nh + h))
    vec_spec = pl.BlockSpec((None, 1, LANES), lambda b, h: (h, 0, 0))
    out = pl.pallas_call(
        functools.partial(_hgrn_kernel, S=S),
        grid=(B, nh),
        in_specs=[in_spec(0), in_spec(1), in_spec(2), in_spec(3), in_spec(4), vec_spec, vec_spec, vec_spec],
        out_specs=pl.BlockSpec((None, S, LANES), lambda b, h: (b, 0, h)),
        out_shape=jax.ShapeDtypeStruct((B, S, HGRN_WIDTH), BF16),
        scratch_shapes=[pltpu.VMEM((S, LANES), F32),
                        pltpu.VMEM((2, S, LANES), BF16),
                        pltpu.VMEM((2, S, LANES), BF16),
                        pltpu.VMEM((2, S // HGRN_CHUNK, LANES), F32)],
        compiler_params=_cparams(("parallel", "parallel")),
        name="hgrn2",
    )(pv, pv, pv, pv, pv, lb_fwd.reshape(nh, 1, LANES), lb_bwd.reshape(nh, 1, LANES),
      norm_g.reshape(nh, 1, LANES))
    return out.reshape(B * S, HGRN_WIDTH)


def _layer_norm_rows(y, g, b):
    mu = jnp.mean(y, axis=-1, keepdims=True)
    d = y - mu
    var = jnp.mean(d * d, axis=-1, keepdims=True)
    return d * lax.rsqrt(var + LN_EPS) * g + b


def _mix_kernel(o1_ref, o2_ref, o3_ref, l1_ref, l2_ref, l3_ref, rec_ref, x_ref, grp_ref, ag_ref, w_ref,
                gate_ref, lng_ref, lnb_ref, sc_ref, sh_ref, x1_ref, u2_ref, urows_ref):
    l1, l2, l3 = l1_ref[...], l2_ref[...], l3_ref[...]
    m = jnp.maximum(jnp.maximum(l1, l2), l3)
    e1, e2, e3 = jnp.exp(l1 - m), jnp.exp(l2 - m), jnp.exp(l3 - m)
    attn = (e1 * o1_ref[...].astype(F32) + e2 * o2_ref[...].astype(F32)
            + e3 * o3_ref[...].astype(F32)) / (e1 + e2 + e3)
    ms = jnp.dot((attn * attn).astype(BF16), grp_ref[...], preferred_element_type=F32)
    normed = attn * lax.rsqrt(ms + RMS_EPS) * ag_ref[...]
    mixed = jnp.concatenate([normed.astype(BF16), rec_ref[...]], axis=-1)
    mix = jnp.dot(mixed, w_ref[...], preferred_element_type=F32)
    x1 = _layer_norm_rows(ALPHA * x_ref[...] + gate_ref[...] * mix, lng_ref[...], lnb_ref[...])
    x1_ref[...] = x1
    u2 = x1 * (1.0 + sc_ref[...]) + sh_ref[...]
    u2_ref[...] = u2
    for cb in range(ROW_SUB):
        urows_ref[pl.ds(cb, MIX_TM, stride=ROW_SUB), :] = u2[:, cb * LANES:(cb + 1) * LANES]


def _mix(o_branches, lse_branches, rec, x2, attn_norm_g, w_out_bf, gate1, ln_g, ln_b, scale2, shift2, S):
    T, D = x2.shape
    per_b = S // MIX_TM
    head = jnp.arange(ATTN_WIDTH) // ATTN_HEAD_DIM
    grp = jnp.where(head[:, None] == head[None, :], 1.0 / ATTN_HEAD_DIM, 0.0).astype(BF16)
    row = lambda w: pl.BlockSpec((MIX_TM, w), lambda i: (i, 0))
    const = lambda shape: pl.BlockSpec(shape, lambda i: (0,) * len(shape))
    per_batch = pl.BlockSpec((None, 1, D), lambda i: (i // per_b, 0, 0))
    return pl.pallas_call(
        _mix_kernel,
        grid=(T // MIX_TM,),
        in_specs=[row(ATTN_WIDTH)] * 6 + [row(HGRN_WIDTH), row(D), const((ATTN_WIDTH, ATTN_WIDTH)),
                  const((1, ATTN_WIDTH)), const((D, D)), per_batch, const((1, D)), const((1, D)),
                  per_batch, per_batch],
        out_specs=[row(D), row(D), pl.BlockSpec((MIX_TM * ROW_SUB, LANES), lambda i: (i, 0))],
        out_shape=[jax.ShapeDtypeStruct((T, D), F32)] * 2 + [jax.ShapeDtypeStruct((T * ROW_SUB, LANES), F32)],
        compiler_params=_cparams(("parallel",)),
        name="mix_out_ln1",
    )(*o_branches, *lse_branches, rec, x2, grp, attn_norm_g.reshape(1, -1), w_out_bf, gate1,
      ln_g.reshape(1, D), ln_b.reshape(1, D), scale2, shift2)


def _route_kernel(u_ref, w_ref, bias_ref, idx_ref, gw_ref, rank_ref, cnt_ref, run_ref):
    i = pl.program_id(0)

    @pl.when(i == 0)
    def _():
        run_ref[...] = jnp.zeros_like(run_ref)

    tm = ROUTE_TM
    logits = jnp.dot(u_ref[...], w_ref[...], preferred_element_type=F32, precision=HIGHEST)
    scores = jax.nn.sigmoid(logits)
    sel = scores + bias_ref[...]
    lane = lax.broadcasted_iota(I32, (tm, N_EXPERTS), 1)
    lane_f = lane.astype(F32)
    group = lane // GROUP_SIZE
    neg = jnp.float32(-jnp.inf)

    def first_argmax(vals):
        mx = jnp.max(vals, axis=-1, keepdims=True)
        idx = jnp.min(jnp.where(vals == mx, lane_f, float(N_EXPERTS)), axis=-1, keepdims=True)
        return mx, idx

    gscore = []
    for g in range(N_EXPERT_GROUPS):
        vals = jnp.where(group == g, sel, neg)
        m1, i1 = first_argmax(vals)
        m2 = jnp.max(jnp.where(lane_f == i1, neg, vals), axis=-1, keepdims=True)
        gscore.append(m1 + m2)
    keep_f = jnp.zeros((tm, N_EXPERTS), F32)
    for g in range(N_EXPERT_GROUPS):
        beaten = jnp.zeros((tm, 1), F32)
        for h in range(N_EXPERT_GROUPS):
            if h == g:
                continue
            ahead = (gscore[h] >= gscore[g]) if h < g else (gscore[h] > gscore[g])
            beaten = beaten + jnp.where(ahead, 1.0, 0.0)
        gkeep = jnp.where(beaten < TOPK_GROUPS, 1.0, 0.0)
        keep_f = jnp.where(group == g, gkeep, keep_f)
    vals = jnp.where(keep_f > 0.5, sel, neg)

    lane_o = lax.broadcasted_iota(I32, (tm, LANES), 1)
    idx_out = jnp.zeros((tm, LANES), F32)
    gw_out = jnp.zeros((tm, LANES), F32)
    chosen = jnp.zeros((tm, N_EXPERTS), F32)
    picks = []
    wsum = jnp.zeros((tm, 1), F32)
    for k in range(TOP_K):
        _, ik = first_argmax(vals)
        hit = lane_f == ik
        sk = jnp.sum(jnp.where(hit, scores, 0.0), axis=-1, keepdims=True)
        vals = jnp.where(hit, neg, vals)
        chosen = jnp.where(hit, 1.0, chosen)
        picks.append((ik, sk))
        wsum = wsum + sk
        idx_out = jnp.where(lane_o == k, ik, idx_out)
    for k, (ik, sk) in enumerate(picks):
        gw_out = jnp.where(lane_o == k, sk / wsum * ROUTED_SCALE, gw_out)

    r = lax.broadcasted_iota(I32, (tm, tm), 0)
    c = lax.broadcasted_iota(I32, (tm, tm), 1)
    strict_lower = jnp.where(c < r, 1.0, 0.0).astype(BF16)
    before = jnp.dot(strict_lower, chosen.astype(BF16), preferred_element_type=F32) + run_ref[...]
    rank_out = jnp.zeros((tm, LANES), F32)
    for k, (ik, sk) in enumerate(picks):
        rk = jnp.sum(jnp.where(lane_f == ik, before, 0.0), axis=-1, keepdims=True)
        rank_out = jnp.where(lane_o == k, rk, rank_out)
    run_ref[...] = run_ref[...] + jnp.sum(chosen, axis=0, keepdims=True)

    idx_ref[...] = idx_out.astype(I32)
    gw_ref[...] = gw_out
    rank_ref[...] = rank_out.astype(I32)
    cnt_ref[...] = run_ref[...]


def _route(u2, w_router, router_bias):
    T, D = u2.shape
    row = lambda w: pl.BlockSpec((ROUTE_TM, w), lambda i: (i, 0))
    idx, gw, rank, cnt = pl.pallas_call(
        _route_kernel,
        grid=(T // ROUTE_TM,),
        in_specs=[row(D), pl.BlockSpec((D, N_EXPERTS), lambda i: (0, 0)),
                  pl.BlockSpec((1, N_EXPERTS), lambda i: (0, 0))],
        out_specs=[row(LANES), row(LANES), row(LANES), pl.BlockSpec((1, N_EXPERTS), lambda i: (0, 0))],
        out_shape=[jax.ShapeDtypeStruct((T, LANES), I32), jax.ShapeDtypeStruct((T, LANES), F32),
                   jax.ShapeDtypeStruct((T, LANES), I32), jax.ShapeDtypeStruct((1, N_EXPERTS), F32)],
        scratch_shapes=[pltpu.VMEM((1, N_EXPERTS), F32)],
        compiler_params=_cparams(("arbitrary",)),
        name="router",
    )(u2, w_router, router_bias.reshape(1, N_EXPERTS))
    return idx, gw, rank, cnt[0].astype(I32)


def _slot_kernel(idx_ref, rank_ref, start_ref, dest_ref):
    tm = ROUTE_TM
    lane_e = lax.broadcasted_iota(I32, (tm, N_EXPERTS), 1)
    lane_o = lax.broadcasted_iota(I32, (tm, LANES), 1)
    idx = idx_ref[...]
    start = start_ref[...]
    base = jnp.zeros((tm, LANES), F32)
    for k in range(TOP_K):
        sk = jnp.sum(jnp.where(lane_e == idx[:, k:k + 1], start, 0.0), axis=-1, keepdims=True)
        base = jnp.where(lane_o == k, sk, base)
    dest_ref[...] = base.astype(I32) + rank_ref[...]


def _slots(idx, rank, start):
    T = idx.shape[0]
    row = pl.BlockSpec((ROUTE_TM, LANES), lambda i: (i, 0))
    dest = pl.pallas_call(
        _slot_kernel,
        grid=(T // ROUTE_TM,),
        in_specs=[row, row, pl.BlockSpec((1, N_EXPERTS), lambda i: (0, 0))],
        out_specs=row,
        out_shape=jax.ShapeDtypeStruct((T, LANES), I32),
        compiler_params=_cparams(("parallel",)),
        name="slots",
    )(idx, rank, start.astype(F32).reshape(1, N_EXPERTS))
    return dest[:, :TOP_K]


MOE_META_SLOTS = 8
MOE_BUFS = 3
Y_TILE = 8
Y_TILE_SHIFT = Y_TILE.bit_length() - 1


def _moe_kernel(bexp_ref, wnext_ref, nused_ref, meta_hbm, u_hbm, wg_hbm, wu_hbm, wd_hbm, y_hbm,
                meta_smem, xb0, xb1, xb2, yb0, yb1, yb2, wg_st, wu_st, wd_st, wg_bf, wu_bf, wd_bf,
                sem_meta, sem_g, sem_s, sem_w):
    i = pl.program_id(0)
    n_blocks = pl.num_programs(0)
    nused = nused_ref[0]
    last = nused - 1
    xbufs = (xb0, xb1, xb2)
    ybufs = (yb0, yb1, yb2)

    def slot(blk):
        return blk & (MOE_META_SLOTS - 1)

    def meta_copy(blk, sl):
        row = pl.ds(pl.multiple_of(blk * (2 * MOE_BLOCK), 2 * MOE_BLOCK), 2 * MOE_BLOCK)
        return pltpu.make_async_copy(meta_hbm.at[row], meta_smem.at[sl], sem_meta.at[sl])

    def weight_copies(e):
        return (pltpu.make_async_copy(wg_hbm.at[e], wg_st, sem_w.at[0]),
                pltpu.make_async_copy(wu_hbm.at[e], wu_st, sem_w.at[1]),
                pltpu.make_async_copy(wd_hbm.at[e], wd_st, sem_w.at[2]))

    def gather_rows(sl, xdst, sem):
        for j in range(MOE_BLOCK):
            row0 = pl.multiple_of(meta_smem[sl, j] * ROW_SUB, ROW_SUB)
            pltpu.make_async_copy(u_hbm.at[pl.ds(row0, ROW_SUB)], xdst.at[pl.ds(j * ROW_PITCH, ROW_SUB)], sem).start()

    def scatter_rows(sl, ysrc, sem):
        for j in range(MOE_BLOCK):
            dst = meta_smem[sl, MOE_BLOCK + j]
            pltpu.make_async_copy(ysrc.at[pl.ds(j, 1)],
                                  y_hbm.at[dst >> Y_TILE_SHIFT, pl.ds(dst & (Y_TILE - 1), 1)],
                                  sem).start(priority=1)

    def wait_gather(sem):
        n = MOE_BLOCK * ROW_SUB
        pltpu.make_async_copy(u_hbm.at[pl.ds(0, n)], xb0.at[pl.ds(0, n)], sem).wait()

    def wait_scatter(sem):
        pltpu.make_async_copy(yb0, yb1, sem).wait()

    @pl.when(i >= nused)
    def _():
        @pl.when(i == nused)
        def _():
            yb0[...] = jnp.zeros_like(yb0)

        @pl.when(i > nused)
        def _():
            wait_scatter(sem_s.at[0])

        tiles = MOE_BLOCK // Y_TILE
        for a in range(tiles):
            pltpu.make_async_copy(yb0.at[pl.ds(a * Y_TILE, Y_TILE)], y_hbm.at[i * tiles + a], sem_s.at[0]).start()

        @pl.when(i == n_blocks - 1)
        def _():
            wait_scatter(sem_s.at[0])

    @pl.when(i < nused)
    def _():
        @pl.when(i == 0)
        def _():
            for blk, sl in ((0, 0), (n_blocks, slot(-1)), (jnp.minimum(1, last), 1)):
                cp = meta_copy(blk, sl)
                cp.start()
                cp.wait()
            meta_copy(jnp.minimum(2, last), 2).start()
            yb2[...] = jnp.zeros_like(yb2)
            gather_rows(0, xb0, sem_g.at[0])
            gather_rows(1, xb1, sem_g.at[1])
            for cp in weight_copies(bexp_ref[0]):
                cp.start(priority=1)

        meta_copy(jnp.minimum(i + 2, last), slot(i + 2)).wait()
        meta_copy(jnp.minimum(i + 3, last), slot(i + 3)).start()

        @pl.when((i == 0) | (bexp_ref[i] != bexp_ref[jnp.maximum(i - 1, 0)]))
        def _():
            for cp in weight_copies(bexp_ref[i]):
                cp.wait()
            wg_bf[...] = wg_st[...].astype(BF16)
            wu_bf[...] = wu_st[...].astype(BF16)
            wd_bf[...] = wd_st[...].astype(BF16)
            nxt_e = wnext_ref[i]

            @pl.when(nxt_e >= 0)
            def _():
                for cp in weight_copies(nxt_e):
                    cp.start(priority=1)

    def step(r):
        r_prev = (r + MOE_BUFS - 1) % MOE_BUFS
        xcur, ycur = xbufs[r], ybufs[r]
        wait_gather(sem_g.at[r])
        gather_rows(slot(i + 2), xbufs[r_prev], sem_g.at[r_prev])
        scatter_rows(slot(i - 1), ybufs[r_prev], sem_s.at[r_prev])
        x = jnp.concatenate([xcur[pl.ds(cb, MOE_BLOCK, stride=ROW_PITCH), :].astype(BF16)
                             for cb in range(ROW_SUB)], axis=1)
        hg = jnp.dot(x, wg_bf[...], preferred_element_type=F32)
        hu = jnp.dot(x, wu_bf[...], preferred_element_type=F32)
        act = (hg * jax.nn.sigmoid(hg) * hu).astype(BF16)

        @pl.when(i >= 2)
        def _():
            wait_scatter(sem_s.at[r])

        ycur[...] = jnp.dot(act, wd_bf[...], preferred_element_type=F32)

        @pl.when(i == last)
        def _():
            scatter_rows(slot(i), ycur, sem_s.at[r])

    for r in range(MOE_BUFS):
        @pl.when((i < nused) & (i % MOE_BUFS == r))
        def _(r=r):
            step(r)

    @pl.when(i == last)
    def _():
        wait_scatter(sem_s.at[i % MOE_BUFS])
        wait_scatter(sem_s.at[(i + 2) % MOE_BUFS])

        @pl.when(i >= 1)
        def _():
            wait_scatter(sem_s.at[(i + 1) % MOE_BUFS])

        wait_gather(sem_g.at[(i + 1) % MOE_BUFS])
        wait_gather(sem_g.at[(i + 2) % MOE_BUFS])
        meta_copy(jnp.minimum(i + 3, last), slot(i + 3)).wait()


def _moe(u_rows, bexp, wnext, nused, meta, w_gate, w_up, w_down, n_pad):
    D = ROW_SUB * LANES
    n_blocks = n_pad // MOE_BLOCK
    E = EXPERT_DIM
    grid_spec = pltpu.PrefetchScalarGridSpec(
        num_scalar_prefetch=3,
        grid=(n_blocks,),
        in_specs=[pl.BlockSpec(memory_space=pl.ANY)] * 5,
        out_specs=pl.BlockSpec(memory_space=pl.ANY),
        scratch_shapes=[pltpu.SMEM((MOE_META_SLOTS, 2 * MOE_BLOCK), I32)]
                       + [pltpu.VMEM((MOE_BLOCK * ROW_PITCH, LANES), F32)] * MOE_BUFS
                       + [pltpu.VMEM((MOE_BLOCK, D), F32)] * MOE_BUFS + [
                        pltpu.VMEM((D, E), F32),
                        pltpu.VMEM((D, E), F32),
                        pltpu.VMEM((E, D), F32),
                        pltpu.VMEM((D, E), BF16),
                        pltpu.VMEM((D, E), BF16),
                        pltpu.VMEM((E, D), BF16),
                        pltpu.SemaphoreType.DMA((MOE_META_SLOTS,)),
                        pltpu.SemaphoreType.DMA((MOE_BUFS,)),
                        pltpu.SemaphoreType.DMA((MOE_BUFS,)),
                        pltpu.SemaphoreType.DMA((3,))])
    return pl.pallas_call(
        _moe_kernel,
        grid_spec=grid_spec,
        out_shape=jax.ShapeDtypeStruct(((n_pad + MOE_BLOCK) // Y_TILE, Y_TILE, D), F32),
        compiler_params=_cparams(("arbitrary",)),
        name="moe_experts",
    )(bexp, wnext, nused, meta, u_rows, w_gate, w_up, w_down)


def _dispatch_plan(idx, rank, counts, T):
    n_assign = T * TOP_K
    n_blocks = -(-(n_assign + N_EXPERTS * (MOE_BLOCK - 1)) // MOE_BLOCK)
    n_pad = n_blocks * MOE_BLOCK
    padded = (counts + MOE_BLOCK - 1) // MOE_BLOCK * MOE_BLOCK
    padded_end = jnp.cumsum(padded)
    start = padded_end - padded
    dest = _slots(idx, rank, start)
    src = jnp.full((n_pad,), -1, I32).at[dest.reshape(-1)].set(jnp.arange(n_assign, dtype=I32))
    is_pad = src < 0
    pad_rank = jnp.cumsum(is_pad.astype(I32)) - 1
    tok = jnp.where(is_pad, 0, src // TOP_K)
    dst = jnp.where(is_pad, n_assign + pad_rank, src)
    tok = jnp.concatenate([tok, jnp.zeros((MOE_BLOCK,), I32)]).reshape(n_blocks + 1, MOE_BLOCK)
    dst = jnp.concatenate([dst, n_pad + jnp.arange(MOE_BLOCK, dtype=I32)]).reshape(n_blocks + 1, MOE_BLOCK)
    meta = jnp.concatenate([tok, dst], axis=1).reshape(-1)
    block_row0 = jnp.arange(n_blocks, dtype=I32) * MOE_BLOCK
    bexp = jnp.minimum(jnp.sum((padded_end[None, :] <= block_row0[:, None]).astype(I32), axis=1),
                       N_EXPERTS - 1).astype(I32)
    nused = (padded_end[-1] // MOE_BLOCK).astype(I32).reshape(1)
    eid = jnp.arange(N_EXPERTS, dtype=I32)
    later = (eid[None, :] > eid[:, None]) & (counts[None, :] > 0)
    next_e = jnp.min(jnp.where(later, eid[None, :], N_EXPERTS), axis=1)
    onehot = (bexp[:, None] == eid[None, :]).astype(I32)
    wnext = jnp.sum(onehot * next_e[None, :], axis=1)
    wnext = jnp.where(wnext < N_EXPERTS, wnext, -1).astype(I32)
    return bexp, wnext, nused, meta.astype(I32), n_pad


def _final_kernel(u_ref, y_ref, gw_ref, x1_ref, wg_ref, wu_ref, wd_ref, gate_ref, lng_ref, lnb_ref, o_ref):
    tm = FINAL_TM
    nrow = tm * TOP_K
    u = u_ref[...].astype(BF16)
    hg = jnp.dot(u, wg_ref[...], preferred_element_type=F32)
    hu = jnp.dot(u, wu_ref[...], preferred_element_type=F32)
    act = (hg * jax.nn.sigmoid(hg) * hu).astype(BF16)
    ffn = jnp.dot(act, wd_ref[...], preferred_element_type=F32)
    expand = jnp.where((lax.broadcasted_iota(I32, (nrow, tm), 0) // TOP_K) == lax.broadcasted_iota(I32, (nrow, tm), 1),
                       1.0, 0.0)
    gw_rows = jnp.dot(expand, gw_ref[...], preferred_element_type=F32, precision=HIGHEST)
    pick = lax.broadcasted_iota(I32, (nrow, LANES), 1) == (lax.broadcasted_iota(I32, (nrow, LANES), 0) % TOP_K)
    wcol = jnp.sum(jnp.where(pick, gw_rows, 0.0), axis=-1, keepdims=True)
    yw = (y_ref[...] * wcol).astype(BF16)
    fold = jnp.where((lax.broadcasted_iota(I32, (tm, nrow), 1) // TOP_K) == lax.broadcasted_iota(I32, (tm, nrow), 0),
                     1.0, 0.0).astype(BF16)
    ffn = ffn + jnp.dot(fold, yw, preferred_element_type=F32)
    o_ref[...] = _layer_norm_rows(ALPHA * x1_ref[...] + gate_ref[...] * ffn, lng_ref[...], lnb_ref[...])


def _final(u2, y_flat, gw, x1, ws_gate_bf, ws_up_bf, ws_down_bf, gate2, ln_g, ln_b, S):
    T, D = u2.shape
    E = ws_gate_bf.shape[1]
    per_b = S // FINAL_TM
    assert Y_TILE == TOP_K
    row = lambda w: pl.BlockSpec((FINAL_TM, w), lambda i: (i, 0))
    const = lambda shape: pl.BlockSpec(shape, lambda i: (0,) * len(shape))
    return pl.pallas_call(
        _final_kernel,
        grid=(T // FINAL_TM,),
        in_specs=[row(D), pl.BlockSpec((FINAL_TM * TOP_K, D), lambda i: (i, 0)), row(LANES), row(D),
                  const((D, E)), const((D, E)), const((E, D)),
                  pl.BlockSpec((None, 1, D), lambda i: (i // per_b, 0, 0)), const((1, D)), const((1, D))],
        out_specs=row(D),
        out_shape=jax.ShapeDtypeStruct((T, D), F32),
        compiler_params=_cparams(("parallel",)),
        name="shared_combine_ln2",
    )(u2, y_flat.reshape(-1, D), gw, x1, ws_gate_bf, ws_up_bf, ws_down_bf, gate2, ln_g.reshape(1, D), ln_b.reshape(1, D))


def kernel(x, c, positions, w_ada, b_ada, w_in, lb_logits, attn_norm_g, hgrn_norm_g, w_out, ln1_g, ln1_b,
           w_router, router_bias, expert_w_gate, expert_w_up, expert_w_down, shared_w_gate, shared_w_up,
           shared_w_down, ln2_g, ln2_b):
    B, S, D = x.shape
    T = B * S
    layer = 0
    lower_bounds = jnp.cumsum(jax.nn.softmax(lb_logits.astype(F32), axis=1), axis=1)

    mod = _ada_mod(c, w_ada[layer], b_ada[layer])
    shift1, scale1, gate1, shift2, scale2, gate2 = [m.reshape(B, 1, D) for m in jnp.split(mod, 6, axis=-1)]
    cos_t, sin_t = _rope_tables(positions)
    x2 = x.reshape(T, D)

    proj, qkv4, qkv16 = _in_proj(x2, scale1, shift1, w_in[layer].astype(BF16), cos_t, sin_t, B, S)
    branches = [_attn_branch(qkv, d, B, S)
                for qkv, d in zip((proj.reshape(B, S, -1), qkv4, qkv16), DILATIONS)]
    rec = _hgrn(proj, lower_bounds[0, layer], lower_bounds[1, layer], hgrn_norm_g[layer], B, S)
    x1, u2, u_rows = _mix([o for o, _ in branches], [l for _, l in branches], rec, x2, attn_norm_g[layer],
                  w_out[layer].astype(BF16), gate1, ln1_g[layer], ln1_b[layer], scale2, shift2, S)

    idx, gw, rank, counts = _route(u2, w_router[layer], router_bias[layer])
    bexp, wnext, nused, meta, n_pad = _dispatch_plan(idx, rank, counts, T)
    y_flat = _moe(u_rows, bexp, wnext, nused, meta, expert_w_gate[layer], expert_w_up[layer], expert_w_down[layer], n_pad)
    out = _final(u2, y_flat, gw, x1, shared_w_gate[layer].astype(BF16), shared_w_up[layer].astype(BF16),
                 shared_w_down[layer].astype(BF16), gate2, ln2_g[layer], ln2_b[layer], S)
    return out.reshape(B, S, D)
```

```python
import functools

import jax
import jax.numpy as jnp
from jax import lax
from jax.experimental import pallas as pl
from jax.experimental.pallas import tpu as pltpu

F32 = jnp.float32
BF16 = jnp.bfloat16
I32 = jnp.int32
HIGHEST = lax.Precision.HIGHEST

D_MODEL = 2048
ATTN_HEADS = 16
ATTN_HEAD_DIM = 64
ATTN_WIDTH = ATTN_HEADS * ATTN_HEAD_DIM
HGRN_HEADS = 8
HGRN_HEAD_DIM = 128
HGRN_WIDTH = HGRN_HEADS * HGRN_HEAD_DIM
IN_PROJ_WIDTH = 3 * ATTN_WIDTH + 5 * HGRN_WIDTH
DILATIONS = (1, 4, 16)
ATTN_HALF = 64
ROPE_THETA = 10000.0
N_EXPERTS = 256
N_EXPERT_GROUPS = 8
GROUP_SIZE = N_EXPERTS // N_EXPERT_GROUPS
TOPK_GROUPS = 4
TOP_K = 8
EXPERT_DIM = 512
ROUTED_SCALE = 2.5
DEPTH = 1
ALPHA = (2 * DEPTH) ** 0.25
LN_EPS = 1e-5
RMS_EPS = 1e-6
NEG_INF = -1e30
LOG2E = 1.4426950408889634

LANES = 128
VMEM_LIMIT = 56 * 1024 * 1024

ADA_TN = 1024
ROPE_TM = 2048
INPROJ_TM = 512
INPROJ_TN = 1024
ATTN_TQ = 128
ATTN_WK = 256
ATTN_UNROLL = 4
HGRN_CHUNK = 64
HGRN_SUB = 16
HGRN_STATE_UNROLL = 4
HGRN_INTRA_UNROLL = 4
MIX_TM = 256
ROUTE_TM = 256
MOE_BLOCK = 128
ROW_SUB = D_MODEL // LANES
ROW_PITCH = 24
FINAL_TM = 128


def _cparams(sem):
    return pltpu.CompilerParams(dimension_semantics=sem, vmem_limit_bytes=VMEM_LIMIT)


def _ada_kernel(c_ref, w_ref, b_ref, o_ref):
    c = c_ref[...]
    sc = c * jax.nn.sigmoid(c)
    o_ref[...] = jnp.dot(sc, w_ref[...], preferred_element_type=F32, precision=HIGHEST) + b_ref[...]


def _ada_mod(c, w_ada, b_ada):
    B, D = c.shape
    N = w_ada.shape[1]
    c8 = jnp.zeros((8, D), F32).at[:B].set(c)
    out = pl.pallas_call(
        _ada_kernel,
        grid=(N // ADA_TN,),
        in_specs=[pl.BlockSpec((8, D), lambda j: (0, 0)),
                  pl.BlockSpec((D, ADA_TN), lambda j: (0, j)),
                  pl.BlockSpec((1, ADA_TN), lambda j: (0, j))],
        out_specs=pl.BlockSpec((8, ADA_TN), lambda j: (0, j)),
        out_shape=jax.ShapeDtypeStruct((8, N), F32),
        compiler_params=_cparams(("parallel",)),
        name="ada_mod",
    )(c8, w_ada, b_ada.reshape(1, N))
    return out[:B]


def _rope_kernel(pos_ref, invf_ref, sign_ref, cos_ref, sin_ref):
    ang = pos_ref[...].astype(F32) * invf_ref[...]
    cos_ref[...] = jnp.cos(ang)
    sin_ref[...] = jnp.sin(ang) * sign_ref[...]


def _rope_tables(positions):
    T = positions.size
    half = ATTN_HEAD_DIM // 2
    inv_freq = ROPE_THETA ** (-jnp.arange(half, dtype=F32) / half)
    lane = jnp.arange(LANES)
    invf = inv_freq[lane % half].reshape(1, LANES)
    sign = jnp.where((lane % ATTN_HEAD_DIM) < half, -1.0, 1.0).astype(F32).reshape(1, LANES)
    return pl.pallas_call(
        _rope_kernel,
        grid=(T // ROPE_TM,),
        in_specs=[pl.BlockSpec((ROPE_TM, 1), lambda i: (i, 0)),
                  pl.BlockSpec((1, LANES), lambda i: (0, 0)),
                  pl.BlockSpec((1, LANES), lambda i: (0, 0))],
        out_specs=[pl.BlockSpec((ROPE_TM, LANES), lambda i: (i, 0)),
                   pl.BlockSpec((ROPE_TM, LANES), lambda i: (i, 0))],
        out_shape=[jax.ShapeDtypeStruct((T, LANES), F32)] * 2,
        compiler_params=_cparams(("parallel",)),
        name="rope_tables",
    )(positions.reshape(T, 1), invf, sign)


def _inproj_kernel(x_ref, sc_ref, sh_ref, w_ref, cos_ref, sin_ref, o_ref, o4_ref, o16_ref, stage_ref):
    j = pl.program_id(1)
    u = (x_ref[...] * (1.0 + sc_ref[...]) + sh_ref[...]).astype(BF16)
    acc = jnp.dot(u, w_ref[...], preferred_element_type=F32)

    @pl.when(j < 2)
    def _():
        qscale = jnp.where(j == 0, ATTN_HEAD_DIM ** -0.5, 1.0).astype(F32)
        cos = cos_ref[...] * qscale
        sin = sin_ref[...] * qscale
        lane = lax.broadcasted_iota(I32, cos.shape, 1)
        first = (lane % ATTN_HEAD_DIM) < (ATTN_HEAD_DIM // 2)
        for cb in range(INPROJ_TN // LANES):
            a = acc[:, cb * LANES:(cb + 1) * LANES]
            partner = jnp.where(first, pltpu.roll(a, LANES - ATTN_HEAD_DIM // 2, 1),
                                pltpu.roll(a, ATTN_HEAD_DIM // 2, 1))
            stage_ref[cb] = a * cos + partner * sin

    @pl.when(j == 2)
    def _():
        for cb in range(INPROJ_TN // LANES):
            stage_ref[cb] = acc[:, cb * LANES:(cb + 1) * LANES]

    @pl.when(j < 3)
    def _():
        for cb in range(INPROJ_TN // LANES):
            o_ref[:, cb * LANES:(cb + 1) * LANES] = stage_ref[cb].astype(BF16)
            for d, od_ref in ((DILATIONS[1], o4_ref), (DILATIONS[2], o16_ref)):
                rows = INPROJ_TM // d
                for r in range(d):
                    c0 = r * INPROJ_TN + cb * LANES
                    od_ref[:, c0:c0 + LANES] = stage_ref[cb, pl.ds(r, rows, stride=d), :].astype(BF16)

    @pl.when(j >= 3)
    def _():
        o_ref[...] = acc.astype(BF16)


def _in_proj(x2, scale1, shift1, w_in_bf, cos_t, sin_t, B, S):
    T, D = x2.shape
    P = w_in_bf.shape[1]
    per_b = S // INPROJ_TM
    assert INPROJ_TN == ATTN_WIDTH
    d4, d16 = DILATIONS[1], DILATIONS[2]
    strided_spec = lambda d: pl.BlockSpec((None, INPROJ_TM // d, d * INPROJ_TN),
                                          lambda i, j: (i // per_b, i % per_b, jnp.minimum(j, 2)))
    strided_shape = lambda d: jax.ShapeDtypeStruct((B, S // d, 3 * d * ATTN_WIDTH), BF16)
    return pl.pallas_call(
        _inproj_kernel,
        grid=(T // INPROJ_TM, P // INPROJ_TN),
        in_specs=[pl.BlockSpec((INPROJ_TM, D), lambda i, j: (i, 0)),
                  pl.BlockSpec((None, 1, D), lambda i, j: (i // per_b, 0, 0)),
                  pl.BlockSpec((None, 1, D), lambda i, j: (i // per_b, 0, 0)),
                  pl.BlockSpec((D, INPROJ_TN), lambda i, j: (0, j)),
                  pl.BlockSpec((INPROJ_TM, LANES), lambda i, j: (i, 0)),
                  pl.BlockSpec((INPROJ_TM, LANES), lambda i, j: (i, 0))],
        out_specs=[pl.BlockSpec((INPROJ_TM, INPROJ_TN), lambda i, j: (i, j)), strided_spec(d4), strided_spec(d16)],
        out_shape=[jax.ShapeDtypeStruct((T, P), BF16), strided_shape(d4), strided_shape(d16)],
        scratch_shapes=[pltpu.VMEM((INPROJ_TN // LANES, INPROJ_TM, LANES), F32)],
        compiler_params=_cparams(("parallel", "arbitrary")),
        name="in_proj",
    )(x2, scale1, shift1, w_in_bf, cos_t, sin_t)


def _attn_kernel(q_ref, k_ref, v_ref, o_ref, lse_ref, *, L, dilation):
    nq = L // ATTN_TQ
    res = pl.program_id(2)
    lane = lax.broadcasted_iota(I32, (1, LANES), 1)
    head0 = lane < ATTN_HEAD_DIM
    rel = (lax.broadcasted_iota(I32, (ATTN_TQ, ATTN_WK), 1)
           - lax.broadcasted_iota(I32, (ATTN_TQ, ATTN_WK), 0))

    def body(i, carry):
        q0 = pl.multiple_of(i * ATTN_TQ, ATTN_TQ)
        ks = pl.multiple_of(jnp.clip(i * ATTN_TQ - ATTN_HALF, 0, L - ATTN_WK), ATTN_HALF)
        q = q_ref[pl.ds(q0, ATTN_TQ), :]
        k = k_ref[pl.ds(ks, ATTN_WK), :]
        v = v_ref[pl.ds(ks, ATTN_WK), :]
        mask = jnp.abs(rel + (ks - q0)) <= ATTN_HALF
        outs, lses = [], []
        for hmask in (head0, jnp.logical_not(head0)):
            qh = jnp.where(hmask, q, jnp.zeros_like(q))
            s = lax.dot_general(qh, k, (((1,), (1,)), ((), ())), preferred_element_type=F32)
            s = jnp.where(mask, s, NEG_INF)
            m = jnp.max(s, axis=-1, keepdims=True)
            p = jnp.exp(s - m)
            l = jnp.sum(p, axis=-1, keepdims=True)
            outs.append(jnp.dot(p.astype(BF16), v, preferred_element_type=F32) / l)
            lses.append(m + jnp.log(l))
        if dilation == 1:
            rows = pl.ds(q0, ATTN_TQ)
        else:
            rows = pl.ds(q0 * dilation + res, ATTN_TQ, stride=dilation)
        o_ref[rows, :] = jnp.where(head0, outs[0], outs[1]).astype(o_ref.dtype)
        lse_ref[rows, :] = jnp.where(head0, lses[0], lses[1])
        return carry

    lax.fori_loop(0, nq, body, 0, unroll=ATTN_UNROLL)


def _attn_branch(qkv, dilation, B, S):
    L = S // dilation
    acb = ATTN_WIDTH // LANES
    in_spec = lambda part: pl.BlockSpec((None, L, LANES),
                                        lambda b, h, r: (b, 0, (part * dilation + r) * acb + h))
    out_spec = pl.BlockSpec((None, S, LANES), lambda b, h, r: (b, 0, h))
    o_dtype = BF16 if dilation == 1 else F32
    o, lse = pl.pallas_call(
        functools.partial(_attn_kernel, L=L, dilation=dilation),
        grid=(B, acb, dilation),
        in_specs=[in_spec(0), in_spec(1), in_spec(2)],
        out_specs=[out_spec, out_spec],
        out_shape=[jax.ShapeDtypeStruct((B, S, ATTN_WIDTH), o_dtype),
                   jax.ShapeDtypeStruct((B, S, ATTN_WIDTH), F32)],
        compiler_params=_cparams(("parallel", "parallel", "arbitrary")),
        name=f"attn_d{dilation}",
    )(qkv, qkv, qkv)
    return o.reshape(B * S, ATTN_WIDTH), lse.reshape(B * S, ATTN_WIDTH)


def _hgrn_chunk(q, kk, lf, v_bf, tri, reverse):
    C = HGRN_CHUNK
    SUB = HGRN_SUB
    nsub = C // SUB
    b = jnp.dot(tri, lf, preferred_element_type=F32, precision=HIGHEST) * LOG2E
    col = lax.broadcasted_iota(I32, (SUB, C), 1)
    row = lax.broadcasted_iota(I32, (SUB, C), 0)
    score_rows = []
    for i in range(nsub):
        r0 = i * SUB
        bi = b[r0:r0 + SUB]
        qi = q[r0:r0 + SUB]
        ki = kk[r0:r0 + SUB]
        diag = jnp.zeros((SUB, C), F32)
        for s in range(SUB):
            e = jnp.exp2(bi - bi[s:s + 1])
            colv = jnp.sum(qi * e * ki[s:s + 1], axis=-1, keepdims=True)
            diag = jnp.where(col == r0 + s, colv, diag)
        if reverse:
            keep = (col - r0) >= row
        else:
            keep = (col - r0) <= row
        diag = jnp.where(jnp.logical_and(keep, jnp.logical_and(col >= r0, col < r0 + SUB)), diag, 0.0)
        if reverse:
            has_off = i < nsub - 1
            bref = b[r0 + SUB:r0 + SUB + 1] if has_off else None
            off_mask = col >= r0 + SUB
        else:
            has_off = i > 0
            bref = b[r0 - 1:r0] if has_off else None
            off_mask = col < r0
        if has_off:
            qs = (qi * jnp.exp2(bi - bref)).astype(BF16)
            ks = (kk * jnp.exp2(bref - b)).astype(BF16)
            off = lax.dot_general(qs, ks, (((1,), (1,)), ((), ())), preferred_element_type=F32)
            score_rows.append(jnp.where(off_mask, off, diag))
        else:
            score_rows.append(diag)
    scores = jnp.concatenate(score_rows, axis=0)
    b_edge = b[0:1] if reverse else b[C - 1:C]
    o = jnp.dot(scores.astype(BF16), v_bf, preferred_element_type=F32)
    qd = (q * jnp.exp2(b)).astype(BF16)
    kd = (kk * jnp.exp2(b_edge - b)).astype(BF16)
    return o, qd, kd, jnp.exp2(b_edge)


def _hgrn_kernel(q_ref, zf_ref, zb_ref, v_ref, g_ref, lbf_ref, lbb_ref, ng_ref, o_ref,
                 acc_ref, qd_ref, kd_ref, dec_ref, *, S):
    C = HGRN_CHUNK
    n_chunks = S // C
    r = lax.broadcasted_iota(I32, (C, C), 0)
    c = lax.broadcasted_iota(I32, (C, C), 1)
    tri_f = (c <= r).astype(F32)
    tri_b = (c >= r).astype(F32)
    lbf = lbf_ref[...]
    lbb = lbb_ref[...]

    def gates(z, lb):
        sg = jax.nn.sigmoid(z)
        return jnp.log(lb + (1.0 - lb) * sg), (1.0 - lb) * (1.0 - sg)

    def intra_body(n, carry):
        r0 = pl.multiple_of(n * C, C)
        q = q_ref[pl.ds(r0, C), :].astype(F32)
        v_bf = v_ref[pl.ds(r0, C), :]
        total = None
        for d, (z_ref, lb, tri) in enumerate(((zf_ref, lbf, tri_f), (zb_ref, lbb, tri_b))):
            lf, kk = gates(z_ref[pl.ds(r0, C), :].astype(F32), lb)
            o, qd, kd, dec = _hgrn_chunk(q, kk, lf, v_bf, tri, d == 1)
            qd_ref[d, pl.ds(r0, C), :] = qd
            kd_ref[d, pl.ds(r0, C), :] = kd
            dec_ref[d, pl.ds(n, 1), :] = dec
            total = o if total is None else total + o
        acc_ref[pl.ds(r0, C), :] = total
        return carry

    lax.fori_loop(0, n_chunks, intra_body, 0, unroll=HGRN_INTRA_UNROLL)

    def state_body(n, carry):
        states = list(carry)
        for d, blk in enumerate((n, n_chunks - 1 - n)):
            r0 = pl.multiple_of(blk * C, C)
            st = states[d]
            o = lax.dot_general(qd_ref[d, pl.ds(r0, C), :], st.astype(BF16), (((1,), (1,)), ((), ())),
                                preferred_element_type=F32)
            acc_ref[pl.ds(r0, C), :] += o
            upd = lax.dot_general(v_ref[pl.ds(r0, C), :], kd_ref[d, pl.ds(r0, C), :], (((0,), (0,)), ((), ())),
                                  preferred_element_type=F32)
            states[d] = st * dec_ref[d, pl.ds(blk, 1), :] + upd
        return tuple(states)

    z0 = jnp.zeros((HGRN_HEAD_DIM, HGRN_HEAD_DIM), F32)
    lax.fori_loop(0, n_chunks, state_body, (z0, z0), unroll=HGRN_STATE_UNROLL)

    ng = ng_ref[...]

    def norm_body(n, carry):
        r0 = pl.multiple_of(n * 512, 512)
        t = acc_ref[pl.ds(r0, 512), :]
        g = g_ref[pl.ds(r0, 512), :].astype(F32)
        y = t * lax.rsqrt(jnp.mean(t * t, axis=-1, keepdims=True) + RMS_EPS) * ng
        o_ref[pl.ds(r0, 512), :] = (y * (g * jax.nn.sigmoid(g))).astype(BF16)
        return carry

    lax.fori_loop(0, S // 512, norm_body, 0)


def _hgrn(proj, lb_fwd, lb_bwd, norm_g, B, S):
    P = proj.shape[1]
    pv = proj.reshape(B, S, P)
    base = 3 * ATTN_WIDTH // LANES
    nh = HGRN_HEADS
    in_spec = lambda k: pl.BlockSpec((None, S, LANES), lambda b, h: (b, 0, base + k * nh + h))
    vec_spec = pl.BlockSpec((None, 1, LANES), lambda b, h: (h, 0, 0))
    out = pl.pallas_call(
        functools.partial(_hgrn_kernel, S=S),
        grid=(B, nh),
        in_specs=[in_spec(0), in_spec(1), in_spec(2), in_spec(3), in_spec(4), vec_spec, vec_spec, vec_spec],
        out_specs=pl.BlockSpec((None, S, LANES), lambda b, h: (b, 0, h)),
        out_shape=jax.ShapeDtypeStruct((B, S, HGRN_WIDTH), BF16),
        scratch_shapes=[pltpu.VMEM((S, LANES), F32),
                        pltpu.VMEM((2, S, LANES), BF16),
                        pltpu.VMEM((2, S, LANES), BF16),
                        pltpu.VMEM((2, S // HGRN_CHUNK, LANES), F32)],
        compiler_params=_cparams(("parallel", "parallel")),
        name="hgrn2",
    )(pv, pv, pv, pv, pv, lb_fwd.reshape(nh, 1, LANES), lb_bwd.reshape(nh, 1, LANES),
      norm_g.reshape(nh, 1, LANES))
    return out.reshape(B * S, HGRN_WIDTH)


def _layer_norm_rows(y, g, b):
    mu = jnp.mean(y, axis=-1, keepdims=True)
    d = y - mu
    var = jnp.mean(d * d, axis=-1, keepdims=True)
    return d * lax.rsqrt(var + LN_EPS) * g + b


def _mix_kernel(o1_ref, o2_ref, o3_ref, l1_ref, l2_ref, l3_ref, rec_ref, x_ref, grp_ref, ag_ref, w_ref,
                gate_ref, lng_ref, lnb_ref, sc_ref, sh_ref, x1_ref, u2_ref, urows_ref):
    l1, l2, l3 = l1_ref[...], l2_ref[...], l3_ref[...]
    m = jnp.maximum(jnp.maximum(l1, l2), l3)
    e1, e2, e3 = jnp.exp(l1 - m), jnp.exp(l2 - m), jnp.exp(l3 - m)
    attn = (e1 * o1_ref[...].astype(F32) + e2 * o2_ref[...].astype(F32)
            + e3 * o3_ref[...].astype(F32)) / (e1 + e2 + e3)
    ms = jnp.dot((attn * attn).astype(BF16), grp_ref[...], preferred_element_type=F32)
    normed = attn * lax.rsqrt(ms + RMS_EPS) * ag_ref[...]
    mixed = jnp.concatenate([normed.astype(BF16), rec_ref[...]], axis=-1)
    mix = jnp.dot(mixed, w_ref[...], preferred_element_type=F32)
    x1 = _layer_norm_rows(ALPHA * x_ref[...] + gate_ref[...] * mix, lng_ref[...], lnb_ref[...])
    x1_ref[...] = x1
    u2 = x1 * (1.0 + sc_ref[...]) + sh_ref[...]
    u2_ref[...] = u2
    for cb in range(ROW_SUB):
        urows_ref[pl.ds(cb, MIX_TM, stride=ROW_SUB), :] = u2[:, cb * LANES:(cb + 1) * LANES]


def _mix(o_branches, lse_branches, rec, x2, attn_norm_g, w_out_bf, gate1, ln_g, ln_b, scale2, shift2, S):
    T, D = x2.shape
    per_b = S // MIX_TM
    head = jnp.arange(ATTN_WIDTH) // ATTN_HEAD_DIM
    grp = jnp.where(head[:, None] == head[None, :], 1.0 / ATTN_HEAD_DIM, 0.0).astype(BF16)
    row = lambda w: pl.BlockSpec((MIX_TM, w), lambda i: (i, 0))
    const = lambda shape: pl.BlockSpec(shape, lambda i: (0,) * len(shape))
    per_batch = pl.BlockSpec((None, 1, D), lambda i: (i // per_b, 0, 0))
    return pl.pallas_call(
        _mix_kernel,
        grid=(T // MIX_TM,),
        in_specs=[row(ATTN_WIDTH)] * 6 + [row(HGRN_WIDTH), row(D), const((ATTN_WIDTH, ATTN_WIDTH)),
                  const((1, ATTN_WIDTH)), const((D, D)), per_batch, const((1, D)), const((1, D)),
                  per_batch, per_batch],
        out_specs=[row(D), row(D), pl.BlockSpec((MIX_TM * ROW_SUB, LANES), lambda i: (i, 0))],
        out_shape=[jax.ShapeDtypeStruct((T, D), F32)] * 2 + [jax.ShapeDtypeStruct((T * ROW_SUB, LANES), F32)],
        compiler_params=_cparams(("parallel",)),
        name="mix_out_ln1",
    )(*o_branches, *lse_branches, rec, x2, grp, attn_norm_g.reshape(1, -1), w_out_bf, gate1,
      ln_g.reshape(1, D), ln_b.reshape(1, D), scale2, shift2)


def _route_kernel(u_ref, w_ref, bias_ref, idx_ref, gw_ref, rank_ref, cnt_ref, run_ref):
    i = pl.program_id(0)

    @pl.when(i == 0)
    def _():
        run_ref[...] = jnp.zeros_like(run_ref)

    tm = ROUTE_TM
    logits = jnp.dot(u_ref[...], w_ref[...], preferred_element_type=F32, precision=HIGHEST)
    scores = jax.nn.sigmoid(logits)
    sel = scores + bias_ref[...]
    lane = lax.broadcasted_iota(I32, (tm, N_EXPERTS), 1)
    lane_f = lane.astype(F32)
    group = lane // GROUP_SIZE
    neg = jnp.float32(-jnp.inf)

    def first_argmax(vals):
        mx = jnp.max(vals, axis=-1, keepdims=True)
        idx = jnp.min(jnp.where(vals == mx, lane_f, float(N_EXPERTS)), axis=-1, keepdims=True)
        return mx, idx

    gscore = []
    for g in range(N_EXPERT_GROUPS):
        vals = jnp.where(group == g, sel, neg)
        m1, i1 = first_argmax(vals)
        m2 = jnp.max(jnp.where(lane_f == i1, neg, vals), axis=-1, keepdims=True)
        gscore.append(m1 + m2)
    keep_f = jnp.zeros((tm, N_EXPERTS), F32)
    for g in range(N_EXPERT_GROUPS):
        beaten = jnp.zeros((tm, 1), F32)
        for h in range(N_EXPERT_GROUPS):
            if h == g:
                continue
            ahead = (gscore[h] >= gscore[g]) if h < g else (gscore[h] > gscore[g])
            beaten = beaten + jnp.where(ahead, 1.0, 0.0)
        gkeep = jnp.where(beaten < TOPK_GROUPS, 1.0, 0.0)
        keep_f = jnp.where(group == g, gkeep, keep_f)
    vals = jnp.where(keep_f > 0.5, sel, neg)

    lane_o = lax.broadcasted_iota(I32, (tm, LANES), 1)
    idx_out = jnp.zeros((tm, LANES), F32)
    gw_out = jnp.zeros((tm, LANES), F32)
    chosen = jnp.zeros((tm, N_EXPERTS), F32)
    picks = []
    wsum = jnp.zeros((tm, 1), F32)
    for k in range(TOP_K):
        _, ik = first_argmax(vals)
        hit = lane_f == ik
        sk = jnp.sum(jnp.where(hit, scores, 0.0), axis=-1, keepdims=True)
        vals = jnp.where(hit, neg, vals)
        chosen = jnp.where(hit, 1.0, chosen)
        picks.append((ik, sk))
        wsum = wsum + sk
        idx_out = jnp.where(lane_o == k, ik, idx_out)
    for k, (ik, sk) in enumerate(picks):
        gw_out = jnp.where(lane_o == k, sk / wsum * ROUTED_SCALE, gw_out)

    r = lax.broadcasted_iota(I32, (tm, tm), 0)
    c = lax.broadcasted_iota(I32, (tm, tm), 1)
    strict_lower = jnp.where(c < r, 1.0, 0.0).astype(BF16)
    before = jnp.dot(strict_lower, chosen.astype(BF16), preferred_element_type=F32) + run_ref[...]
    rank_out = jnp.zeros((tm, LANES), F32)
    for k, (ik, sk) in enumerate(picks):
        rk = jnp.sum(jnp.where(lane_f == ik, before, 0.0), axis=-1, keepdims=True)
        rank_out = jnp.where(lane_o == k, rk, rank_out)
    run_ref[...] = run_ref[...] + jnp.sum(chosen, axis=0, keepdims=True)

    idx_ref[...] = idx_out.astype(I32)
    gw_ref[...] = gw_out
    rank_ref[...] = rank_out.astype(I32)
    cnt_ref[...] = run_ref[...]


def _route(u2, w_router, router_bias):
    T, D = u2.shape
    row = lambda w: pl.BlockSpec((ROUTE_TM, w), lambda i: (i, 0))
    idx, gw, rank, cnt = pl.pallas_call(
        _route_kernel,
        grid=(T // ROUTE_TM,),
        in_specs=[row(D), pl.BlockSpec((D, N_EXPERTS), lambda i: (0, 0)),
                  pl.BlockSpec((1, N_EXPERTS), lambda i: (0, 0))],
        out_specs=[row(LANES), row(LANES), row(LANES), pl.BlockSpec((1, N_EXPERTS), lambda i: (0, 0))],
        out_shape=[jax.ShapeDtypeStruct((T, LANES), I32), jax.ShapeDtypeStruct((T, LANES), F32),
                   jax.ShapeDtypeStruct((T, LANES), I32), jax.ShapeDtypeStruct((1, N_EXPERTS), F32)],
        scratch_shapes=[pltpu.VMEM((1, N_EXPERTS), F32)],
        compiler_params=_cparams(("arbitrary",)),
        name="router",
    )(u2, w_router, router_bias.reshape(1, N_EXPERTS))
    return idx, gw, rank, cnt[0].astype(I32)


def _slot_kernel(idx_ref, rank_ref, start_ref, dest_ref):
    tm = ROUTE_TM
    lane_e = lax.broadcasted_iota(I32, (tm, N_EXPERTS), 1)
    lane_o = lax.broadcasted_iota(I32, (tm, LANES), 1)
    idx = idx_ref[...]
    start = start_ref[...]
    base = jnp.zeros((tm, LANES), F32)
    for k in range(TOP_K):
        sk = jnp.sum(jnp.where(lane_e == idx[:, k:k + 1], start, 0.0), axis=-1, keepdims=True)
        base = jnp.where(lane_o == k, sk, base)
    dest_ref[...] = base.astype(I32) + rank_ref[...]


def _slots(idx, rank, start):
    T = idx.shape[0]
    row = pl.BlockSpec((ROUTE_TM, LANES), lambda i: (i, 0))
    dest = pl.pallas_call(
        _slot_kernel,
        grid=(T // ROUTE_TM,),
        in_specs=[row, row, pl.BlockSpec((1, N_EXPERTS), lambda i: (0, 0))],
        out_specs=row,
        out_shape=jax.ShapeDtypeStruct((T, LANES), I32),
        compiler_params=_cparams(("parallel",)),
        name="slots",
    )(idx, rank, start.astype(F32).reshape(1, N_EXPERTS))
    return dest[:, :TOP_K]


MOE_META_SLOTS = 8
MOE_BUFS = 3
Y_TILE = 8
Y_TILE_SHIFT = Y_TILE.bit_length() - 1


def _moe_kernel(bexp_ref, wnext_ref, nused_ref, meta_hbm, u_hbm, wg_hbm, wu_hbm, wd_hbm, y_hbm,
                meta_smem, xb0, xb1, xb2, yb0, yb1, yb2, wg_st, wu_st, wd_st, wg_bf, wu_bf, wd_bf,
                sem_meta, sem_g, sem_s, sem_w):
    i = pl.program_id(0)
    n_blocks = pl.num_programs(0)
    nused = nused_ref[0]
    last = nused - 1
    xbufs = (xb0, xb1, xb2)
    ybufs = (yb0, yb1, yb2)

    def slot(blk):
        return blk & (MOE_META_SLOTS - 1)

    def meta_copy(blk, sl):
        row = pl.ds(pl.multiple_of(blk * (2 * MOE_BLOCK), 2 * MOE_BLOCK), 2 * MOE_BLOCK)
        return pltpu.make_async_copy(meta_hbm.at[row], meta_smem.at[sl], sem_meta.at[sl])

    def weight_copies(e):
        return (pltpu.make_async_copy(wg_hbm.at[e], wg_st, sem_w.at[0]),
                pltpu.make_async_copy(wu_hbm.at[e], wu_st, sem_w.at[1]),
                pltpu.make_async_copy(wd_hbm.at[e], wd_st, sem_w.at[2]))

    def gather_rows(sl, xdst, sem):
        for j in range(MOE_BLOCK):
            row0 = pl.multiple_of(meta_smem[sl, j] * ROW_SUB, ROW_SUB)
            pltpu.make_async_copy(u_hbm.at[pl.ds(row0, ROW_SUB)], xdst.at[pl.ds(j * ROW_PITCH, ROW_SUB)], sem).start()

    def scatter_rows(sl, ysrc, sem):
        for j in range(MOE_BLOCK):
            dst = meta_smem[sl, MOE_BLOCK + j]
            row0 = pl.multiple_of(dst * ROW_SUB, ROW_SUB)
            pltpu.make_async_copy(ysrc.at[pl.ds(j * ROW_PITCH, ROW_SUB)], y_hbm.at[pl.ds(row0, ROW_SUB)],
                                  sem).start(priority=1)

    def wait_gather(sem):
        n = MOE_BLOCK * ROW_SUB
        pltpu.make_async_copy(u_hbm.at[pl.ds(0, n)], xb0.at[pl.ds(0, n)], sem).wait()

    def wait_scatter(sem):
        n = MOE_BLOCK * ROW_SUB
        pltpu.make_async_copy(yb0.at[pl.ds(0, n)], yb1.at[pl.ds(0, n)], sem).wait()

    @pl.when(i >= nused)
    def _():
        @pl.when(i == nused)
        def _():
            yb0[...] = jnp.zeros_like(yb0)

        @pl.when(i > nused)
        def _():
            wait_scatter(sem_s.at[0])

        n = MOE_BLOCK * ROW_SUB
        pltpu.make_async_copy(yb0.at[pl.ds(0, n)], y_hbm.at[pl.ds(pl.multiple_of(i * n, n), n)], sem_s.at[0]).start()

        @pl.when(i == n_blocks - 1)
        def _():
            wait_scatter(sem_s.at[0])

    @pl.when(i < nused)
    def _():
        @pl.when(i == 0)
        def _():
            for blk, sl in ((0, 0), (n_blocks, slot(-1)), (jnp.minimum(1, last), 1)):
                cp = meta_copy(blk, sl)
                cp.start()
                cp.wait()
            meta_copy(jnp.minimum(2, last), 2).start()
            yb2[...] = jnp.zeros_like(yb2)
            gather_rows(0, xb0, sem_g.at[0])
            gather_rows(1, xb1, sem_g.at[1])
            for cp in weight_copies(bexp_ref[0]):
                cp.start(priority=1)

        meta_copy(jnp.minimum(i + 2, last), slot(i + 2)).wait()
        meta_copy(jnp.minimum(i + 3, last), slot(i + 3)).start()

        @pl.when((i == 0) | (bexp_ref[i] != bexp_ref[jnp.maximum(i - 1, 0)]))
        def _():
            for cp in weight_copies(bexp_ref[i]):
                cp.wait()
            wg_bf[...] = wg_st[...].astype(BF16)
            wu_bf[...] = wu_st[...].astype(BF16)
            wd_bf[...] = wd_st[...].astype(BF16)
            nxt_e = wnext_ref[i]

            @pl.when(nxt_e >= 0)
            def _():
                for cp in weight_copies(nxt_e):
                    cp.start(priority=1)

    def step(r):
        r_prev = (r + MOE_BUFS - 1) % MOE_BUFS
        xcur, ycur = xbufs[r], ybufs[r]
        wait_gather(sem_g.at[r])
        gather_rows(slot(i + 2), xbufs[r_prev], sem_g.at[r_prev])
        scatter_rows(slot(i - 1), ybufs[r_prev], sem_s.at[r_prev])
        x = jnp.concatenate([xcur[pl.ds(cb, MOE_BLOCK, stride=ROW_PITCH), :].astype(BF16)
                             for cb in range(ROW_SUB)], axis=1)
        hg = jnp.dot(x, wg_bf[...], preferred_element_type=F32)
        hu = jnp.dot(x, wu_bf[...], preferred_element_type=F32)
        act = (hg * jax.nn.sigmoid(hg) * hu).astype(BF16)

        @pl.when(i >= 2)
        def _():
            wait_scatter(sem_s.at[r])

        y = jnp.dot(act, wd_bf[...], preferred_element_type=F32)
        for cb in range(ROW_SUB):
            ycur[pl.ds(cb, MOE_BLOCK, stride=ROW_PITCH), :] = y[:, cb * LANES:(cb + 1) * LANES]

        @pl.when(i == last)
        def _():
            scatter_rows(slot(i), ycur, sem_s.at[r])

    for r in range(MOE_BUFS):
        @pl.when((i < nused) & (i % MOE_BUFS == r))
        def _(r=r):
            step(r)

    @pl.when(i == last)
    def _():
        wait_scatter(sem_s.at[i % MOE_BUFS])
        wait_scatter(sem_s.at[(i + 2) % MOE_BUFS])

        @pl.when(i >= 1)
        def _():
            wait_scatter(sem_s.at[(i + 1) % MOE_BUFS])

        wait_gather(sem_g.at[(i + 1) % MOE_BUFS])
        wait_gather(sem_g.at[(i + 2) % MOE_BUFS])
        meta_copy(jnp.minimum(i + 3, last), slot(i + 3)).wait()


def _moe(u_rows, bexp, wnext, nused, meta, w_gate, w_up, w_down, n_pad):
    D = ROW_SUB * LANES
    n_blocks = n_pad // MOE_BLOCK
    E = EXPERT_DIM
    grid_spec = pltpu.PrefetchScalarGridSpec(
        num_scalar_prefetch=3,
        grid=(n_blocks,),
        in_specs=[pl.BlockSpec(memory_space=pl.ANY)] * 5,
        out_specs=pl.BlockSpec(memory_space=pl.ANY),
        scratch_shapes=[pltpu.SMEM((MOE_META_SLOTS, 2 * MOE_BLOCK), I32)]
                       + [pltpu.VMEM((MOE_BLOCK * ROW_PITCH, LANES), F32)] * MOE_BUFS
                       + [pltpu.VMEM((MOE_BLOCK * ROW_PITCH, LANES), F32)] * MOE_BUFS + [
                        pltpu.VMEM((D, E), F32),
                        pltpu.VMEM((D, E), F32),
                        pltpu.VMEM((E, D), F32),
                        pltpu.VMEM((D, E), BF16),
                        pltpu.VMEM((D, E), BF16),
                        pltpu.VMEM((E, D), BF16),
                        pltpu.SemaphoreType.DMA((MOE_META_SLOTS,)),
                        pltpu.SemaphoreType.DMA((MOE_BUFS,)),
                        pltpu.SemaphoreType.DMA((MOE_BUFS,)),
                        pltpu.SemaphoreType.DMA((3,))])
    return pl.pallas_call(
        _moe_kernel,
        grid_spec=grid_spec,
        out_shape=jax.ShapeDtypeStruct(((n_pad + MOE_BLOCK) * ROW_SUB, LANES), F32),
        compiler_params=_cparams(("arbitrary",)),
        name="moe_experts",
    )(bexp, wnext, nused, meta, u_rows, w_gate, w_up, w_down)


def _dispatch_plan(idx, rank, counts, T):
    n_assign = T * TOP_K
    n_blocks = -(-(n_assign + N_EXPERTS * (MOE_BLOCK - 1)) // MOE_BLOCK)
    n_pad = n_blocks * MOE_BLOCK
    padded = (counts + MOE_BLOCK - 1) // MOE_BLOCK * MOE_BLOCK
    padded_end = jnp.cumsum(padded)
    start = padded_end - padded
    dest = _slots(idx, rank, start)
    src = jnp.full((n_pad,), -1, I32).at[dest.reshape(-1)].set(jnp.arange(n_assign, dtype=I32))
    is_pad = src < 0
    pad_rank = jnp.cumsum(is_pad.astype(I32)) - 1
    tok = jnp.where(is_pad, 0, src // TOP_K)
    dst = jnp.where(is_pad, n_assign + pad_rank, src)
    tok = jnp.concatenate([tok, jnp.zeros((MOE_BLOCK,), I32)]).reshape(n_blocks + 1, MOE_BLOCK)
    dst = jnp.concatenate([dst, n_pad + jnp.arange(MOE_BLOCK, dtype=I32)]).reshape(n_blocks + 1, MOE_BLOCK)
    meta = jnp.concatenate([tok, dst], axis=1).reshape(-1)
    block_row0 = jnp.arange(n_blocks, dtype=I32) * MOE_BLOCK
    bexp = jnp.minimum(jnp.sum((padded_end[None, :] <= block_row0[:, None]).astype(I32), axis=1),
                       N_EXPERTS - 1).astype(I32)
    nused = (padded_end[-1] // MOE_BLOCK).astype(I32).reshape(1)
    eid = jnp.arange(N_EXPERTS, dtype=I32)
    later = (eid[None, :] > eid[:, None]) & (counts[None, :] > 0)
    next_e = jnp.min(jnp.where(later, eid[None, :], N_EXPERTS), axis=1)
    onehot = (bexp[:, None] == eid[None, :]).astype(I32)
    wnext = jnp.sum(onehot * next_e[None, :], axis=1)
    wnext = jnp.where(wnext < N_EXPERTS, wnext, -1).astype(I32)
    return bexp, wnext, nused, meta.astype(I32), n_pad


def _final_kernel(u_ref, y_ref, gw_ref, x1_ref, wg_ref, wu_ref, wd_ref, gate_ref, lng_ref, lnb_ref, o_ref):
    tm = FINAL_TM
    nrow = tm * TOP_K
    u = u_ref[...].astype(BF16)
    hg = jnp.dot(u, wg_ref[...], preferred_element_type=F32)
    hu = jnp.dot(u, wu_ref[...], preferred_element_type=F32)
    act = (hg * jax.nn.sigmoid(hg) * hu).astype(BF16)
    ffn = jnp.dot(act, wd_ref[...], preferred_element_type=F32)
    expand = jnp.where((lax.broadcasted_iota(I32, (nrow, tm), 0) // TOP_K) == lax.broadcasted_iota(I32, (nrow, tm), 1),
                       1.0, 0.0)
    gw_rows = jnp.dot(expand, gw_ref[...], preferred_element_type=F32, precision=HIGHEST)
    pick = lax.broadcasted_iota(I32, (nrow, LANES), 1) == (lax.broadcasted_iota(I32, (nrow, LANES), 0) % TOP_K)
    wcol = jnp.sum(jnp.where(pick, gw_rows, 0.0), axis=-1, keepdims=True)
    yrows = jnp.concatenate([y_ref[pl.ds(cb, nrow, stride=ROW_SUB), :] for cb in range(ROW_SUB)], axis=1)
    yw = (yrows * wcol).astype(BF16)
    fold = jnp.where((lax.broadcasted_iota(I32, (tm, nrow), 1) // TOP_K) == lax.broadcasted_iota(I32, (tm, nrow), 0),
                     1.0, 0.0).astype(BF16)
    ffn = ffn + jnp.dot(fold, yw, preferred_element_type=F32)
    o_ref[...] = _layer_norm_rows(ALPHA * x1_ref[...] + gate_ref[...] * ffn, lng_ref[...], lnb_ref[...])


def _final(u2, y_flat, gw, x1, ws_gate_bf, ws_up_bf, ws_down_bf, gate2, ln_g, ln_b, S):
    T, D = u2.shape
    E = ws_gate_bf.shape[1]
    per_b = S // FINAL_TM
    assert Y_TILE == TOP_K
    row = lambda w: pl.BlockSpec((FINAL_TM, w), lambda i: (i, 0))
    const = lambda shape: pl.BlockSpec(shape, lambda i: (0,) * len(shape))
    return pl.pallas_call(
        _final_kernel,
        grid=(T // FINAL_TM,),
        in_specs=[row(D), pl.BlockSpec((FINAL_TM * TOP_K * ROW_SUB, LANES), lambda i: (i, 0)), row(LANES), row(D),
                  const((D, E)), const((D, E)), const((E, D)),
                  pl.BlockSpec((None, 1, D), lambda i: (i // per_b, 0, 0)), const((1, D)), const((1, D))],
        out_specs=row(D),
        out_shape=jax.ShapeDtypeStruct((T, D), F32),
        compiler_params=_cparams(("parallel",)),
        name="shared_combine_ln2",
    )(u2, y_flat, gw, x1, ws_gate_bf, ws_up_bf, ws_down_bf, gate2, ln_g.reshape(1, D), ln_b.reshape(1, D))


def kernel(x, c, positions, w_ada, b_ada, w_in, lb_logits, attn_norm_g, hgrn_norm_g, w_out, ln1_g, ln1_b,
           w_router, router_bias, expert_w_gate, expert_w_up, expert_w_down, shared_w_gate, shared_w_up,
           shared_w_down, ln2_g, ln2_b):
    B, S, D = x.shape
    T = B * S
    layer = 0
    lower_bounds = jnp.cumsum(jax.nn.softmax(lb_logits.astype(F32), axis=1), axis=1)

    mod = _ada_mod(c, w_ada[layer], b_ada[layer])
    shift1, scale1, gate1, shift2, scale2, gate2 = [m.reshape(B, 1, D) for m in jnp.split(mod, 6, axis=-1)]
    cos_t, sin_t = _rope_tables(positions)
    x2 = x.reshape(T, D)

    proj, qkv4, qkv16 = _in_proj(x2, scale1, shift1, w_in[layer].astype(BF16), cos_t, sin_t, B, S)
    branches = [_attn_branch(qkv, d, B, S)
                for qkv, d in zip((proj.reshape(B, S, -1), qkv4, qkv16), DILATIONS)]
    rec = _hgrn(proj, lower_bounds[0, layer], lower_bounds[1, layer], hgrn_norm_g[layer], B, S)
    x1, u2, u_rows = _mix([o for o, _ in branches], [l for _, l in branches], rec, x2, attn_norm_g[layer],
                  w_out[layer].astype(BF16), gate1, ln1_g[layer], ln1_b[layer], scale2, shift2, S)

    idx, gw, rank, counts = _route(u2, w_router[layer], router_bias[layer])
    bexp, wnext, nused, meta, n_pad = _dispatch_plan(idx, rank, counts, T)
    y_flat = _moe(u_rows, bexp, wnext, nused, meta, expert_w_gate[layer], expert_w_up[layer], expert_w_down[layer], n_pad)
    out = _final(u2, y_flat, gw, x1, shared_w_gate[layer].astype(BF16), shared_w_up[layer].astype(BF16),
                 shared_w_down[layer].astype(BF16), gate2, ln2_g[layer], ln2_b[layer], S)
    return out.reshape(B, S, D)
```

```python
import functools

import jax
import jax.numpy as jnp
from jax import lax
from jax.experimental import pallas as pl
from jax.experimental.pallas import tpu as pltpu

F32 = jnp.float32
BF16 = jnp.bfloat16
I32 = jnp.int32
HIGHEST = lax.Precision.HIGHEST

D_MODEL = 2048
ATTN_HEADS = 16
ATTN_HEAD_DIM = 64
ATTN_WIDTH = ATTN_HEADS * ATTN_HEAD_DIM
HGRN_HEADS = 8
HGRN_HEAD_DIM = 128
HGRN_WIDTH = HGRN_HEADS * HGRN_HEAD_DIM
IN_PROJ_WIDTH = 3 * ATTN_WIDTH + 5 * HGRN_WIDTH
DILATIONS = (1, 4, 16)
ATTN_HALF = 64
ROPE_THETA = 10000.0
N_EXPERTS = 256
N_EXPERT_GROUPS = 8
GROUP_SIZE = N_EXPERTS // N_EXPERT_GROUPS
TOPK_GROUPS = 4
TOP_K = 8
EXPERT_DIM = 512
ROUTED_SCALE = 2.5
DEPTH = 1
ALPHA = (2 * DEPTH) ** 0.25
LN_EPS = 1e-5
RMS_EPS = 1e-6
NEG_INF = -1e30
LOG2E = 1.4426950408889634

LANES = 128
VMEM_LIMIT = 56 * 1024 * 1024

ADA_TN = 1024
ROPE_TM = 2048
INPROJ_TM = 512
INPROJ_TN = 1024
ATTN_TQ = 128
ATTN_WK = 256
ATTN_UNROLL = 8
HGRN_CHUNK = 64
HGRN_SUB = 16
HGRN_STATE_UNROLL = 4
HGRN_INTRA_UNROLL = 4
MIX_TM = 256
ROUTE_TM = 256
ROUTE_SPLIT = 2
MOE_BLOCK = 128
ROW_SUB = D_MODEL // LANES
ROW_PITCH = 24
FINAL_TM = 128


def _cparams(sem):
    return pltpu.CompilerParams(dimension_semantics=sem, vmem_limit_bytes=VMEM_LIMIT)


def _ada_kernel(c_ref, w_ref, b_ref, o_ref):
    c = c_ref[...]
    sc = c * jax.nn.sigmoid(c)
    o_ref[...] = jnp.dot(sc, w_ref[...], preferred_element_type=F32, precision=HIGHEST) + b_ref[...]


def _ada_mod(c, w_ada, b_ada):
    B, D = c.shape
    N = w_ada.shape[1]
    c8 = jnp.zeros((8, D), F32).at[:B].set(c)
    out = pl.pallas_call(
        _ada_kernel,
        grid=(N // ADA_TN,),
        in_specs=[pl.BlockSpec((8, D), lambda j: (0, 0)),
                  pl.BlockSpec((D, ADA_TN), lambda j: (0, j)),
                  pl.BlockSpec((1, ADA_TN), lambda j: (0, j))],
        out_specs=pl.BlockSpec((8, ADA_TN), lambda j: (0, j)),
        out_shape=jax.ShapeDtypeStruct((8, N), F32),
        compiler_params=_cparams(("parallel",)),
        name="ada_mod",
    )(c8, w_ada, b_ada.reshape(1, N))
    return out[:B]


def _rope_kernel(pos_ref, invf_ref, sign_ref, cos_ref, sin_ref):
    ang = pos_ref[...].astype(F32) * invf_ref[...]
    cos_ref[...] = jnp.cos(ang)
    sin_ref[...] = jnp.sin(ang) * sign_ref[...]


def _rope_tables(positions):
    T = positions.size
    half = ATTN_HEAD_DIM // 2
    inv_freq = ROPE_THETA ** (-jnp.arange(half, dtype=F32) / half)
    lane = jnp.arange(LANES)
    invf = inv_freq[lane % half].reshape(1, LANES)
    sign = jnp.where((lane % ATTN_HEAD_DIM) < half, -1.0, 1.0).astype(F32).reshape(1, LANES)
    return pl.pallas_call(
        _rope_kernel,
        grid=(T // ROPE_TM,),
        in_specs=[pl.BlockSpec((ROPE_TM, 1), lambda i: (i, 0)),
                  pl.BlockSpec((1, LANES), lambda i: (0, 0)),
                  pl.BlockSpec((1, LANES), lambda i: (0, 0))],
        out_specs=[pl.BlockSpec((ROPE_TM, LANES), lambda i: (i, 0)),
                   pl.BlockSpec((ROPE_TM, LANES), lambda i: (i, 0))],
        out_shape=[jax.ShapeDtypeStruct((T, LANES), F32)] * 2,
        compiler_params=_cparams(("parallel",)),
        name="rope_tables",
    )(positions.reshape(T, 1), invf, sign)


def _inproj_kernel(x_ref, sc_ref, sh_ref, w_ref, cos_ref, sin_ref, o_ref, o4_ref, o16_ref, stage_ref):
    j = pl.program_id(1)
    u = (x_ref[...] * (1.0 + sc_ref[...]) + sh_ref[...]).astype(BF16)
    acc = jnp.dot(u, w_ref[...], preferred_element_type=F32)

    @pl.when(j < 2)
    def _():
        qscale = jnp.where(j == 0, ATTN_HEAD_DIM ** -0.5, 1.0).astype(F32)
        cos = cos_ref[...] * qscale
        sin = sin_ref[...] * qscale
        lane = lax.broadcasted_iota(I32, cos.shape, 1)
        first = (lane % ATTN_HEAD_DIM) < (ATTN_HEAD_DIM // 2)
        for cb in range(INPROJ_TN // LANES):
            a = acc[:, cb * LANES:(cb + 1) * LANES]
            partner = jnp.where(first, pltpu.roll(a, LANES - ATTN_HEAD_DIM // 2, 1),
                                pltpu.roll(a, ATTN_HEAD_DIM // 2, 1))
            stage_ref[cb] = a * cos + partner * sin

    @pl.when(j == 2)
    def _():
        for cb in range(INPROJ_TN // LANES):
            stage_ref[cb] = acc[:, cb * LANES:(cb + 1) * LANES]

    @pl.when(j < 3)
    def _():
        for cb in range(INPROJ_TN // LANES):
            o_ref[:, cb * LANES:(cb + 1) * LANES] = stage_ref[cb].astype(BF16)
            for d, od_ref in ((DILATIONS[1], o4_ref), (DILATIONS[2], o16_ref)):
                rows = INPROJ_TM // d
                for r in range(d):
                    c0 = r * INPROJ_TN + cb * LANES
                    od_ref[:, c0:c0 + LANES] = stage_ref[cb, pl.ds(r, rows, stride=d), :].astype(BF16)

    @pl.when(j >= 3)
    def _():
        o_ref[...] = acc.astype(BF16)


def _in_proj(x2, scale1, shift1, w_in_bf, cos_t, sin_t, B, S):
    T, D = x2.shape
    P = w_in_bf.shape[1]
    per_b = S // INPROJ_TM
    assert INPROJ_TN == ATTN_WIDTH
    d4, d16 = DILATIONS[1], DILATIONS[2]
    strided_spec = lambda d: pl.BlockSpec((None, INPROJ_TM // d, d * INPROJ_TN),
                                          lambda i, j: (i // per_b, i % per_b, jnp.minimum(j, 2)))
    strided_shape = lambda d: jax.ShapeDtypeStruct((B, S // d, 3 * d * ATTN_WIDTH), BF16)
    return pl.pallas_call(
        _inproj_kernel,
        grid=(T // INPROJ_TM, P // INPROJ_TN),
        in_specs=[pl.BlockSpec((INPROJ_TM, D), lambda i, j: (i, 0)),
                  pl.BlockSpec((None, 1, D), lambda i, j: (i // per_b, 0, 0)),
                  pl.BlockSpec((None, 1, D), lambda i, j: (i // per_b, 0, 0)),
                  pl.BlockSpec((D, INPROJ_TN), lambda i, j: (0, j)),
                  pl.BlockSpec((INPROJ_TM, LANES), lambda i, j: (i, 0)),
                  pl.BlockSpec((INPROJ_TM, LANES), lambda i, j: (i, 0))],
        out_specs=[pl.BlockSpec((INPROJ_TM, INPROJ_TN), lambda i, j: (i, j)), strided_spec(d4), strided_spec(d16)],
        out_shape=[jax.ShapeDtypeStruct((T, P), BF16), strided_shape(d4), strided_shape(d16)],
        scratch_shapes=[pltpu.VMEM((INPROJ_TN // LANES, INPROJ_TM, LANES), F32)],
        compiler_params=_cparams(("parallel", "arbitrary")),
        name="in_proj",
    )(x2, scale1, shift1, w_in_bf, cos_t, sin_t)


def _attn_kernel(q_ref, k_ref, v_ref, o_ref, lse_ref, *, L, dilation):
    nq = L // ATTN_TQ
    res = pl.program_id(2)
    lane = lax.broadcasted_iota(I32, (1, LANES), 1)
    head0 = lane < ATTN_HEAD_DIM
    rel = (lax.broadcasted_iota(I32, (ATTN_TQ, ATTN_WK), 1)
           - lax.broadcasted_iota(I32, (ATTN_TQ, ATTN_WK), 0))

    def body(i, carry):
        q0 = pl.multiple_of(i * ATTN_TQ, ATTN_TQ)
        ks = pl.multiple_of(jnp.clip(i * ATTN_TQ - ATTN_HALF, 0, L - ATTN_WK), ATTN_HALF)
        q = q_ref[pl.ds(q0, ATTN_TQ), :]
        k = k_ref[pl.ds(ks, ATTN_WK), :]
        v = v_ref[pl.ds(ks, ATTN_WK), :]
        mask = jnp.abs(rel + (ks - q0)) <= ATTN_HALF
        outs, lses = [], []
        for hmask in (head0, jnp.logical_not(head0)):
            qh = jnp.where(hmask, q, jnp.zeros_like(q))
            s = lax.dot_general(qh, k, (((1,), (1,)), ((), ())), preferred_element_type=F32)
            s = jnp.where(mask, s, NEG_INF)
            m = jnp.max(s, axis=-1, keepdims=True)
            p = jnp.exp(s - m)
            l = jnp.sum(p, axis=-1, keepdims=True)
            outs.append(jnp.dot(p.astype(BF16), v, preferred_element_type=F32) / l)
            lses.append(m + jnp.log(l))
        if dilation == 1:
            rows = pl.ds(q0, ATTN_TQ)
        else:
            rows = pl.ds(q0 * dilation + res, ATTN_TQ, stride=dilation)
        o_ref[rows, :] = jnp.where(head0, outs[0], outs[1]).astype(o_ref.dtype)
        lse_ref[rows, :] = jnp.where(head0, lses[0], lses[1])
        return carry

    lax.fori_loop(0, nq, body, 0, unroll=min(ATTN_UNROLL, nq))


def _attn_branch(qkv, dilation, B, S):
    L = S // dilation
    acb = ATTN_WIDTH // LANES
    in_spec = lambda part: pl.BlockSpec((None, L, LANES),
                                        lambda b, h, r: (b, 0, (part * dilation + r) * acb + h))
    out_spec = pl.BlockSpec((None, S, LANES), lambda b, h, r: (b, 0, h))
    o_dtype = BF16 if dilation == 1 else F32
    o, lse = pl.pallas_call(
        functools.partial(_attn_kernel, L=L, dilation=dilation),
        grid=(B, acb, dilation),
        in_specs=[in_spec(0), in_spec(1), in_spec(2)],
        out_specs=[out_spec, out_spec],
        out_shape=[jax.ShapeDtypeStruct((B, S, ATTN_WIDTH), o_dtype),
                   jax.ShapeDtypeStruct((B, S, ATTN_WIDTH), F32)],
        compiler_params=_cparams(("parallel", "parallel", "arbitrary")),
        name=f"attn_d{dilation}",
    )(qkv, qkv, qkv)
    return o.reshape(B * S, ATTN_WIDTH), lse.reshape(B * S, ATTN_WIDTH)


def _hgrn_chunk(q, kk, lf, v_bf, tri, reverse):
    C = HGRN_CHUNK
    SUB = HGRN_SUB
    nsub = C // SUB
    b = jnp.dot(tri, lf, preferred_element_type=F32, precision=HIGHEST) * LOG2E
    col = lax.broadcasted_iota(I32, (SUB, C), 1)
    row = lax.broadcasted_iota(I32, (SUB, C), 0)
    score_rows = []
    for i in range(nsub):
        r0 = i * SUB
        bi = b[r0:r0 + SUB]
        qi = q[r0:r0 + SUB]
        ki = kk[r0:r0 + SUB]
        diag = jnp.zeros((SUB, C), F32)
        for s in range(SUB):
            e = jnp.exp2(bi - bi[s:s + 1])
            colv = jnp.sum(qi * e * ki[s:s + 1], axis=-1, keepdims=True)
            diag = jnp.where(col == r0 + s, colv, diag)
        if reverse:
            keep = (col - r0) >= row
        else:
            keep = (col - r0) <= row
        diag = jnp.where(jnp.logical_and(keep, jnp.logical_and(col >= r0, col < r0 + SUB)), diag, 0.0)
        if reverse:
            has_off = i < nsub - 1
            bref = b[r0 + SUB:r0 + SUB + 1] if has_off else None
            off_mask = col >= r0 + SUB
        else:
            has_off = i > 0
            bref = b[r0 - 1:r0] if has_off else None
            off_mask = col < r0
        if has_off:
            qs = (qi * jnp.exp2(bi - bref)).astype(BF16)
            ks = (kk * jnp.exp2(bref - b)).astype(BF16)
            off = lax.dot_general(qs, ks, (((1,), (1,)), ((), ())), preferred_element_type=F32)
            score_rows.append(jnp.where(off_mask, off, diag))
        else:
            score_rows.append(diag)
    scores = jnp.concatenate(score_rows, axis=0)
    b_edge = b[0:1] if reverse else b[C - 1:C]
    o = jnp.dot(scores.astype(BF16), v_bf, preferred_element_type=F32)
    qd = (q * jnp.exp2(b)).astype(BF16)
    kd = (kk * jnp.exp2(b_edge - b)).astype(BF16)
    return o, qd, kd, jnp.exp2(b_edge)


def _hgrn_kernel(q_ref, zf_ref, zb_ref, v_ref, g_ref, lbf_ref, lbb_ref, ng_ref, o_ref,
                 acc_ref, qd_ref, kd_ref, dec_ref, *, S):
    C = HGRN_CHUNK
    n_chunks = S // C
    r = lax.broadcasted_iota(I32, (C, C), 0)
    c = lax.broadcasted_iota(I32, (C, C), 1)
    tri_f = (c <= r).astype(F32)
    tri_b = (c >= r).astype(F32)
    lbf = lbf_ref[...]
    lbb = lbb_ref[...]

    def gates(z, lb):
        sg = jax.nn.sigmoid(z)
        return jnp.log(lb + (1.0 - lb) * sg), (1.0 - lb) * (1.0 - sg)

    def intra_body(n, carry):
        r0 = pl.multiple_of(n * C, C)
        q = q_ref[pl.ds(r0, C), :].astype(F32)
        v_bf = v_ref[pl.ds(r0, C), :]
        total = None
        for d, (z_ref, lb, tri) in enumerate(((zf_ref, lbf, tri_f), (zb_ref, lbb, tri_b))):
            lf, kk = gates(z_ref[pl.ds(r0, C), :].astype(F32), lb)
            o, qd, kd, dec = _hgrn_chunk(q, kk, lf, v_bf, tri, d == 1)
            qd_ref[d, pl.ds(r0, C), :] = qd
            kd_ref[d, pl.ds(r0, C), :] = kd
            dec_ref[d, pl.ds(n, 1), :] = dec
            total = o if total is None else total + o
        acc_ref[pl.ds(r0, C), :] = total
        return carry

    lax.fori_loop(0, n_chunks, intra_body, 0, unroll=HGRN_INTRA_UNROLL)

    def state_body(n, carry):
        states = list(carry)
        for d, blk in enumerate((n, n_chunks - 1 - n)):
            r0 = pl.multiple_of(blk * C, C)
            st = states[d]
            o = lax.dot_general(qd_ref[d, pl.ds(r0, C), :], st.astype(BF16), (((1,), (1,)), ((), ())),
                                preferred_element_type=F32)
            acc_ref[pl.ds(r0, C), :] += o
            upd = lax.dot_general(v_ref[pl.ds(r0, C), :], kd_ref[d, pl.ds(r0, C), :], (((0,), (0,)), ((), ())),
                                  preferred_element_type=F32)
            states[d] = st * dec_ref[d, pl.ds(blk, 1), :] + upd
        return tuple(states)

    z0 = jnp.zeros((HGRN_HEAD_DIM, HGRN_HEAD_DIM), F32)
    lax.fori_loop(0, n_chunks, state_body, (z0, z0), unroll=HGRN_STATE_UNROLL)

    ng = ng_ref[...]

    def norm_body(n, carry):
        r0 = pl.multiple_of(n * 512, 512)
        t = acc_ref[pl.ds(r0, 512), :]
        g = g_ref[pl.ds(r0, 512), :].astype(F32)
        y = t * lax.rsqrt(jnp.mean(t * t, axis=-1, keepdims=True) + RMS_EPS) * ng
        o_ref[pl.ds(r0, 512), :] = (y * (g * jax.nn.sigmoid(g))).astype(BF16)
        return carry

    lax.fori_loop(0, S // 512, norm_body, 0)


def _hgrn(proj, lb_fwd, lb_bwd, norm_g, B, S):
    P = proj.shape[1]
    pv = proj.reshape(B, S, P)
    base = 3 * ATTN_WIDTH // LANES
    nh = HGRN_HEADS
    in_spec = lambda k: pl.BlockSpec((None, S, LANES), lambda b, h: (b, 0, base + k * nh + h))
    vec_spec = pl.BlockSpec((None, 1, LANES), lambda b, h: (h, 0, 0))
    out = pl.pallas_call(
        functools.partial(_hgrn_kernel, S=S),
        grid=(B, nh),
        in_specs=[in_spec(0), in_spec(1), in_spec(2), in_spec(3), in_spec(4), vec_spec, vec_spec, vec_spec],
        out_specs=pl.BlockSpec((None, S, LANES), lambda b, h: (b, 0, h)),
        out_shape=jax.ShapeDtypeStruct((B, S, HGRN_WIDTH), BF16),
        scratch_shapes=[pltpu.VMEM((S, LANES), F32),
                        pltpu.VMEM((2, S, LANES), BF16),
                        pltpu.VMEM((2, S, LANES), BF16),
                        pltpu.VMEM((2, S // HGRN_CHUNK, LANES), F32)],
        compiler_params=_cparams(("parallel", "parallel")),
        name="hgrn2",
    )(pv, pv, pv, pv, pv, lb_fwd.reshape(nh, 1, LANES), lb_bwd.reshape(nh, 1, LANES),
      norm_g.reshape(nh, 1, LANES))
    return out.reshape(B * S, HGRN_WIDTH)


def _layer_norm_rows(y, g, b):
    mu = jnp.mean(y, axis=-1, keepdims=True)
    d = y - mu
    var = jnp.mean(d * d, axis=-1, keepdims=True)
    return d * lax.rsqrt(var + LN_EPS) * g + b


def _mix_kernel(o1_ref, o2_ref, o3_ref, l1_ref, l2_ref, l3_ref, rec_ref, x_ref, grp_ref, ag_ref, w_ref,
                gate_ref, lng_ref, lnb_ref, sc_ref, sh_ref, x1_ref, u2_ref, urows_ref):
    l1, l2, l3 = l1_ref[...], l2_ref[...], l3_ref[...]
    m = jnp.maximum(jnp.maximum(l1, l2), l3)
    e1, e2, e3 = jnp.exp(l1 - m), jnp.exp(l2 - m), jnp.exp(l3 - m)
    attn = (e1 * o1_ref[...].astype(F32) + e2 * o2_ref[...].astype(F32)
            + e3 * o3_ref[...].astype(F32)) / (e1 + e2 + e3)
    ms = jnp.dot((attn * attn).astype(BF16), grp_ref[...], preferred_element_type=F32)
    normed = attn * lax.rsqrt(ms + RMS_EPS) * ag_ref[...]
    mixed = jnp.concatenate([normed.astype(BF16), rec_ref[...]], axis=-1)
    mix = jnp.dot(mixed, w_ref[...], preferred_element_type=F32)
    x1 = _layer_norm_rows(ALPHA * x_ref[...] + gate_ref[...] * mix, lng_ref[...], lnb_ref[...])
    x1_ref[...] = x1
    u2 = x1 * (1.0 + sc_ref[...]) + sh_ref[...]
    u2_ref[...] = u2
    for cb in range(ROW_SUB):
        urows_ref[pl.ds(cb, MIX_TM, stride=ROW_SUB), :] = u2[:, cb * LANES:(cb + 1) * LANES]


def _mix(o_branches, lse_branches, rec, x2, attn_norm_g, w_out_bf, gate1, ln_g, ln_b, scale2, shift2, S):
    T, D = x2.shape
    per_b = S // MIX_TM
    head = jnp.arange(ATTN_WIDTH) // ATTN_HEAD_DIM
    grp = jnp.where(head[:, None] == head[None, :], 1.0 / ATTN_HEAD_DIM, 0.0).astype(BF16)
    row = lambda w: pl.BlockSpec((MIX_TM, w), lambda i: (i, 0))
    const = lambda shape: pl.BlockSpec(shape, lambda i: (0,) * len(shape))
    per_batch = pl.BlockSpec((None, 1, D), lambda i: (i // per_b, 0, 0))
    return pl.pallas_call(
        _mix_kernel,
        grid=(T // MIX_TM,),
        in_specs=[row(ATTN_WIDTH)] * 6 + [row(HGRN_WIDTH), row(D), const((ATTN_WIDTH, ATTN_WIDTH)),
                  const((1, ATTN_WIDTH)), const((D, D)), per_batch, const((1, D)), const((1, D)),
                  per_batch, per_batch],
        out_specs=[row(D), row(D), pl.BlockSpec((MIX_TM * ROW_SUB, LANES), lambda i: (i, 0))],
        out_shape=[jax.ShapeDtypeStruct((T, D), F32)] * 2 + [jax.ShapeDtypeStruct((T * ROW_SUB, LANES), F32)],
        compiler_params=_cparams(("parallel",)),
        name="mix_out_ln1",
    )(*o_branches, *lse_branches, rec, x2, grp, attn_norm_g.reshape(1, -1), w_out_bf, gate1,
      ln_g.reshape(1, D), ln_b.reshape(1, D), scale2, shift2)


def _route_kernel(u_ref, w_ref, bias_ref, idx_ref, gw_ref, rank_ref, cnt_ref, run_ref):
    i = pl.program_id(0)

    @pl.when(i == 0)
    def _():
        run_ref[...] = jnp.zeros_like(run_ref)

    tm = ROUTE_TM
    hm = tm // ROUTE_SPLIT
    logits = jnp.dot(u_ref[...], w_ref[...], preferred_element_type=F32, precision=HIGHEST)
    scores_all = jax.nn.sigmoid(logits)
    sel_all = scores_all + bias_ref[...]
    lane = lax.broadcasted_iota(I32, (hm, N_EXPERTS), 1)
    lane_f = lane.astype(F32)
    group = lane // GROUP_SIZE
    lane_o = lax.broadcasted_iota(I32, (hm, LANES), 1)
    neg = jnp.float32(-jnp.inf)

    def first_argmax(vals):
        mx = jnp.max(vals, axis=-1, keepdims=True)
        idx = jnp.min(jnp.where(vals == mx, lane_f, float(N_EXPERTS)), axis=-1, keepdims=True)
        return mx, idx

    def select(scores, sel):
        gscore = []
        for g in range(N_EXPERT_GROUPS):
            vals = jnp.where(group == g, sel, neg)
            m1, i1 = first_argmax(vals)
            m2 = jnp.max(jnp.where(lane_f == i1, neg, vals), axis=-1, keepdims=True)
            gscore.append(m1 + m2)
        keep_f = jnp.zeros((hm, N_EXPERTS), F32)
        for g in range(N_EXPERT_GROUPS):
            beaten = jnp.zeros((hm, 1), F32)
            for h in range(N_EXPERT_GROUPS):
                if h == g:
                    continue
                ahead = (gscore[h] >= gscore[g]) if h < g else (gscore[h] > gscore[g])
                beaten = beaten + jnp.where(ahead, 1.0, 0.0)
            gkeep = jnp.where(beaten < TOPK_GROUPS, 1.0, 0.0)
            keep_f = jnp.where(group == g, gkeep, keep_f)
        vals = jnp.where(keep_f > 0.5, sel, neg)

        idx_out = jnp.zeros((hm, LANES), F32)
        gw_out = jnp.zeros((hm, LANES), F32)
        chosen = jnp.zeros((hm, N_EXPERTS), F32)
        picks = []
        wsum = jnp.zeros((hm, 1), F32)
        for k in range(TOP_K):
            _, ik = first_argmax(vals)
            hit = lane_f == ik
            sk = jnp.sum(jnp.where(hit, scores, 0.0), axis=-1, keepdims=True)
            vals = jnp.where(hit, neg, vals)
            chosen = jnp.where(hit, 1.0, chosen)
            picks.append((ik, sk))
            wsum = wsum + sk
            idx_out = jnp.where(lane_o == k, ik, idx_out)
        for k, (ik, sk) in enumerate(picks):
            gw_out = jnp.where(lane_o == k, sk / wsum * ROUTED_SCALE, gw_out)
        return idx_out, gw_out, chosen, [ik for ik, _ in picks]

    parts = [select(scores_all[h * hm:(h + 1) * hm], sel_all[h * hm:(h + 1) * hm]) for h in range(ROUTE_SPLIT)]
    chosen = jnp.concatenate([p[2] for p in parts], axis=0)

    r = lax.broadcasted_iota(I32, (tm, tm), 0)
    c = lax.broadcasted_iota(I32, (tm, tm), 1)
    strict_lower = jnp.where(c < r, 1.0, 0.0).astype(BF16)
    before = jnp.dot(strict_lower, chosen.astype(BF16), preferred_element_type=F32) + run_ref[...]
    for h, (idx_out, gw_out, _, iks) in enumerate(parts):
        rows = slice(h * hm, (h + 1) * hm)
        rank_out = jnp.zeros((hm, LANES), F32)
        for k, ik in enumerate(iks):
            rk = jnp.sum(jnp.where(lane_f == ik, before[rows], 0.0), axis=-1, keepdims=True)
            rank_out = jnp.where(lane_o == k, rk, rank_out)
        idx_ref[rows, :] = idx_out.astype(I32)
        gw_ref[rows, :] = gw_out
        rank_ref[rows, :] = rank_out.astype(I32)
    run_ref[...] = run_ref[...] + jnp.sum(chosen, axis=0, keepdims=True)
    cnt_ref[...] = run_ref[...]


def _route(u2, w_router, router_bias):
    T, D = u2.shape
    row = lambda w: pl.BlockSpec((ROUTE_TM, w), lambda i: (i, 0))
    idx, gw, rank, cnt = pl.pallas_call(
        _route_kernel,
        grid=(T // ROUTE_TM,),
        in_specs=[row(D), pl.BlockSpec((D, N_EXPERTS), lambda i: (0, 0)),
                  pl.BlockSpec((1, N_EXPERTS), lambda i: (0, 0))],
        out_specs=[row(LANES), row(LANES), row(LANES), pl.BlockSpec((1, N_EXPERTS), lambda i: (0, 0))],
        out_shape=[jax.ShapeDtypeStruct((T, LANES), I32), jax.ShapeDtypeStruct((T, LANES), F32),
                   jax.ShapeDtypeStruct((T, LANES), I32), jax.ShapeDtypeStruct((1, N_EXPERTS), F32)],
        scratch_shapes=[pltpu.VMEM((1, N_EXPERTS), F32)],
        compiler_params=_cparams(("arbitrary",)),
        name="router",
    )(u2, w_router, router_bias.reshape(1, N_EXPERTS))
    return idx, gw, rank, cnt[0].astype(I32)


def _slot_kernel(idx_ref, rank_ref, start_ref, dest_ref):
    tm = ROUTE_TM
    lane_e = lax.broadcasted_iota(I32, (tm, N_EXPERTS), 1)
    lane_o = lax.broadcasted_iota(I32, (tm, LANES), 1)
    idx = idx_ref[...]
    start = start_ref[...]
    base = jnp.zeros((tm, LANES), F32)
    for k in range(TOP_K):
        sk = jnp.sum(jnp.where(lane_e == idx[:, k:k + 1], start, 0.0), axis=-1, keepdims=True)
        base = jnp.where(lane_o == k, sk, base)
    dest_ref[...] = base.astype(I32) + rank_ref[...]


def _slots(idx, rank, start):
    T = idx.shape[0]
    row = pl.BlockSpec((ROUTE_TM, LANES), lambda i: (i, 0))
    dest = pl.pallas_call(
        _slot_kernel,
        grid=(T // ROUTE_TM,),
        in_specs=[row, row, pl.BlockSpec((1, N_EXPERTS), lambda i: (0, 0))],
        out_specs=row,
        out_shape=jax.ShapeDtypeStruct((T, LANES), I32),
        compiler_params=_cparams(("parallel",)),
        name="slots",
    )(idx, rank, start.astype(F32).reshape(1, N_EXPERTS))
    return dest[:, :TOP_K]


MOE_META_SLOTS = 8
MOE_BUFS = 3


def _moe_kernel(bexp_ref, wnext_ref, nused_ref, meta_hbm, u_hbm, wg_hbm, wu_hbm, wd_hbm, y_hbm,
                meta_smem, xb0, xb1, xb2, yb0, yb1, yb2, wg_st, wu_st, wd_st, wg_bf, wu_bf, wd_bf,
                sem_meta, sem_g, sem_s, sem_w):
    i = pl.program_id(0)
    n_blocks = pl.num_programs(0)
    nused = nused_ref[0]
    last = nused - 1
    xbufs = (xb0, xb1, xb2)
    ybufs = (yb0, yb1, yb2)

    def slot(blk):
        return blk & (MOE_META_SLOTS - 1)

    def meta_copy(blk, sl):
        row = pl.ds(pl.multiple_of(blk * (2 * MOE_BLOCK), 2 * MOE_BLOCK), 2 * MOE_BLOCK)
        return pltpu.make_async_copy(meta_hbm.at[row], meta_smem.at[sl], sem_meta.at[sl])

    def weight_copies(e):
        return (pltpu.make_async_copy(wg_hbm.at[e], wg_st, sem_w.at[0]),
                pltpu.make_async_copy(wu_hbm.at[e], wu_st, sem_w.at[1]),
                pltpu.make_async_copy(wd_hbm.at[e], wd_st, sem_w.at[2]))

    def gather_rows(sl, xdst, sem):
        for j in range(MOE_BLOCK):
            row0 = pl.multiple_of(meta_smem[sl, j] * ROW_SUB, ROW_SUB)
            pltpu.make_async_copy(u_hbm.at[pl.ds(row0, ROW_SUB)], xdst.at[pl.ds(j * ROW_PITCH, ROW_SUB)], sem).start()

    def scatter_rows(sl, ysrc, sem):
        for j in range(MOE_BLOCK):
            dst = meta_smem[sl, MOE_BLOCK + j]
            row0 = pl.multiple_of(dst * ROW_SUB, ROW_SUB)
            pltpu.make_async_copy(ysrc.at[pl.ds(j * ROW_PITCH, ROW_SUB)], y_hbm.at[pl.ds(row0, ROW_SUB)],
                                  sem).start(priority=1)

    def wait_gather(sem):
        n = MOE_BLOCK * ROW_SUB
        pltpu.make_async_copy(u_hbm.at[pl.ds(0, n)], xb0.at[pl.ds(0, n)], sem).wait()

    def wait_scatter(sem):
        n = MOE_BLOCK * ROW_SUB
        pltpu.make_async_copy(yb0.at[pl.ds(0, n)], yb1.at[pl.ds(0, n)], sem).wait()

    @pl.when(i >= nused)
    def _():
        @pl.when(i == nused)
        def _():
            yb0[...] = jnp.zeros_like(yb0)

        @pl.when(i > nused)
        def _():
            wait_scatter(sem_s.at[0])

        n = MOE_BLOCK * ROW_SUB
        pltpu.make_async_copy(yb0.at[pl.ds(0, n)], y_hbm.at[pl.ds(pl.multiple_of(i * n, n), n)], sem_s.at[0]).start()

        @pl.when(i == n_blocks - 1)
        def _():
            wait_scatter(sem_s.at[0])

    @pl.when(i < nused)
    def _():
        @pl.when(i == 0)
        def _():
            for blk, sl in ((0, 0), (n_blocks, slot(-1)), (jnp.minimum(1, last), 1)):
                cp = meta_copy(blk, sl)
                cp.start()
                cp.wait()
            meta_copy(jnp.minimum(2, last), 2).start()
            yb2[...] = jnp.zeros_like(yb2)
            gather_rows(0, xb0, sem_g.at[0])
            gather_rows(1, xb1, sem_g.at[1])
            for cp in weight_copies(bexp_ref[0]):
                cp.start(priority=1)

        meta_copy(jnp.minimum(i + 2, last), slot(i + 2)).wait()
        meta_copy(jnp.minimum(i + 3, last), slot(i + 3)).start()

        @pl.when((i == 0) | (bexp_ref[i] != bexp_ref[jnp.maximum(i - 1, 0)]))
        def _():
            for cp in weight_copies(bexp_ref[i]):
                cp.wait()
            wg_bf[...] = wg_st[...].astype(BF16)
            wu_bf[...] = wu_st[...].astype(BF16)
            wd_bf[...] = wd_st[...].astype(BF16)
            nxt_e = wnext_ref[i]

            @pl.when(nxt_e >= 0)
            def _():
                for cp in weight_copies(nxt_e):
                    cp.start(priority=1)

    def step(r):
        r_prev = (r + MOE_BUFS - 1) % MOE_BUFS
        xcur, ycur = xbufs[r], ybufs[r]
        wait_gather(sem_g.at[r])
        gather_rows(slot(i + 2), xbufs[r_prev], sem_g.at[r_prev])
        scatter_rows(slot(i - 1), ybufs[r_prev], sem_s.at[r_prev])
        x = jnp.concatenate([xcur[pl.ds(cb, MOE_BLOCK, stride=ROW_PITCH), :].astype(BF16)
                             for cb in range(ROW_SUB)], axis=1)
        hg = jnp.dot(x, wg_bf[...], preferred_element_type=F32)
        hu = jnp.dot(x, wu_bf[...], preferred_element_type=F32)
        act = (hg * jax.nn.sigmoid(hg) * hu).astype(BF16)

        @pl.when(i >= 2)
        def _():
            wait_scatter(sem_s.at[r])

        y = jnp.dot(act, wd_bf[...], preferred_element_type=F32)
        for cb in range(ROW_SUB):
            ycur[pl.ds(cb, MOE_BLOCK, stride=ROW_PITCH), :] = y[:, cb * LANES:(cb + 1) * LANES]

        @pl.when(i == last)
        def _():
            scatter_rows(slot(i), ycur, sem_s.at[r])

    for r in range(MOE_BUFS):
        @pl.when((i < nused) & (i % MOE_BUFS == r))
        def _(r=r):
            step(r)

    @pl.when(i == last)
    def _():
        wait_scatter(sem_s.at[i % MOE_BUFS])
        wait_scatter(sem_s.at[(i + 2) % MOE_BUFS])

        @pl.when(i >= 1)
        def _():
            wait_scatter(sem_s.at[(i + 1) % MOE_BUFS])

        wait_gather(sem_g.at[(i + 1) % MOE_BUFS])
        wait_gather(sem_g.at[(i + 2) % MOE_BUFS])
        meta_copy(jnp.minimum(i + 3, last), slot(i + 3)).wait()


def _moe(u_rows, bexp, wnext, nused, meta, w_gate, w_up, w_down, n_pad):
    D = ROW_SUB * LANES
    n_blocks = n_pad // MOE_BLOCK
    E = EXPERT_DIM
    grid_spec = pltpu.PrefetchScalarGridSpec(
        num_scalar_prefetch=3,
        grid=(n_blocks,),
        in_specs=[pl.BlockSpec(memory_space=pl.ANY)] * 5,
        out_specs=pl.BlockSpec(memory_space=pl.ANY),
        scratch_shapes=[pltpu.SMEM((MOE_META_SLOTS, 2 * MOE_BLOCK), I32)]
                       + [pltpu.VMEM((MOE_BLOCK * ROW_PITCH, LANES), F32)] * MOE_BUFS
                       + [pltpu.VMEM((MOE_BLOCK * ROW_PITCH, LANES), F32)] * MOE_BUFS + [
                        pltpu.VMEM((D, E), F32),
                        pltpu.VMEM((D, E), F32),
                        pltpu.VMEM((E, D), F32),
                        pltpu.VMEM((D, E), BF16),
                        pltpu.VMEM((D, E), BF16),
                        pltpu.VMEM((E, D), BF16),
                        pltpu.SemaphoreType.DMA((MOE_META_SLOTS,)),
                        pltpu.SemaphoreType.DMA((MOE_BUFS,)),
                        pltpu.SemaphoreType.DMA((MOE_BUFS,)),
                        pltpu.SemaphoreType.DMA((3,))])
    return pl.pallas_call(
        _moe_kernel,
        grid_spec=grid_spec,
        out_shape=jax.ShapeDtypeStruct(((n_pad + MOE_BLOCK) * ROW_SUB, LANES), F32),
        compiler_params=_cparams(("arbitrary",)),
        name="moe_experts",
    )(bexp, wnext, nused, meta, u_rows, w_gate, w_up, w_down)


def _dispatch_plan(idx, rank, counts, T):
    n_assign = T * TOP_K
    n_blocks = -(-(n_assign + N_EXPERTS * (MOE_BLOCK - 1)) // MOE_BLOCK)
    n_pad = n_blocks * MOE_BLOCK
    padded = (counts + MOE_BLOCK - 1) // MOE_BLOCK * MOE_BLOCK
    padded_end = jnp.cumsum(padded)
    start = padded_end - padded
    dest = _slots(idx, rank, start)
    src = jnp.full((n_pad,), -1, I32).at[dest.reshape(-1)].set(
        jnp.arange(n_assign, dtype=I32), unique_indices=True, mode='promise_in_bounds')
    is_pad = src < 0
    pad_rank = jnp.cumsum(is_pad.astype(I32)) - 1
    tok = jnp.where(is_pad, 0, src // TOP_K)
    dst = jnp.where(is_pad, n_assign + pad_rank, src)
    tok = jnp.concatenate([tok, jnp.zeros((MOE_BLOCK,), I32)]).reshape(n_blocks + 1, MOE_BLOCK)
    dst = jnp.concatenate([dst, n_pad + jnp.arange(MOE_BLOCK, dtype=I32)]).reshape(n_blocks + 1, MOE_BLOCK)
    meta = jnp.concatenate([tok, dst], axis=1).reshape(-1)
    block_row0 = jnp.arange(n_blocks, dtype=I32) * MOE_BLOCK
    bexp = jnp.minimum(jnp.sum((padded_end[None, :] <= block_row0[:, None]).astype(I32), axis=1),
                       N_EXPERTS - 1).astype(I32)
    nused = (padded_end[-1] // MOE_BLOCK).astype(I32).reshape(1)
    eid = jnp.arange(N_EXPERTS, dtype=I32)
    later = (eid[None, :] > eid[:, None]) & (counts[None, :] > 0)
    next_e = jnp.min(jnp.where(later, eid[None, :], N_EXPERTS), axis=1)
    onehot = (bexp[:, None] == eid[None, :]).astype(I32)
    wnext = jnp.sum(onehot * next_e[None, :], axis=1)
    wnext = jnp.where(wnext < N_EXPERTS, wnext, -1).astype(I32)
    return bexp, wnext, nused, meta.astype(I32), n_pad


def _final_kernel(u_ref, y_ref, gw_ref, x1_ref, wg_ref, wu_ref, wd_ref, gate_ref, lng_ref, lnb_ref, o_ref):
    tm = FINAL_TM
    nrow = tm * TOP_K
    u = u_ref[...].astype(BF16)
    hg = jnp.dot(u, wg_ref[...], preferred_element_type=F32)
    hu = jnp.dot(u, wu_ref[...], preferred_element_type=F32)
    act = (hg * jax.nn.sigmoid(hg) * hu).astype(BF16)
    ffn = jnp.dot(act, wd_ref[...], preferred_element_type=F32)
    expand = jnp.where((lax.broadcasted_iota(I32, (nrow, tm), 0) // TOP_K) == lax.broadcasted_iota(I32, (nrow, tm), 1),
                       1.0, 0.0)
    gw_rows = jnp.dot(expand, gw_ref[...], preferred_element_type=F32, precision=HIGHEST)
    pick = lax.broadcasted_iota(I32, (nrow, LANES), 1) == (lax.broadcasted_iota(I32, (nrow, LANES), 0) % TOP_K)
    wcol = jnp.sum(jnp.where(pick, gw_rows, 0.0), axis=-1, keepdims=True)
    yrows = jnp.concatenate([y_ref[pl.ds(cb, nrow, stride=ROW_SUB), :] for cb in range(ROW_SUB)], axis=1)
    yw = (yrows * wcol).astype(BF16)
    fold = jnp.where((lax.broadcasted_iota(I32, (tm, nrow), 1) // TOP_K) == lax.broadcasted_iota(I32, (tm, nrow), 0),
                     1.0, 0.0).astype(BF16)
    ffn = ffn + jnp.dot(fold, yw, preferred_element_type=F32)
    o_ref[...] = _layer_norm_rows(ALPHA * x1_ref[...] + gate_ref[...] * ffn, lng_ref[...], lnb_ref[...])


def _final(u2, y_flat, gw, x1, ws_gate_bf, ws_up_bf, ws_down_bf, gate2, ln_g, ln_b, S):
    T, D = u2.shape
    E = ws_gate_bf.shape[1]
    per_b = S // FINAL_TM
    row = lambda w: pl.BlockSpec((FINAL_TM, w), lambda i: (i, 0))
    const = lambda shape: pl.BlockSpec(shape, lambda i: (0,) * len(shape))
    return pl.pallas_call(
        _final_kernel,
        grid=(T // FINAL_TM,),
        in_specs=[row(D), pl.BlockSpec((FINAL_TM * TOP_K * ROW_SUB, LANES), lambda i: (i, 0)), row(LANES), row(D),
                  const((D, E)), const((D, E)), const((E, D)),
                  pl.BlockSpec((None, 1, D), lambda i: (i // per_b, 0, 0)), const((1, D)), const((1, D))],
        out_specs=row(D),
        out_shape=jax.ShapeDtypeStruct((T, D), F32),
        compiler_params=_cparams(("parallel",)),
        name="shared_combine_ln2",
    )(u2, y_flat, gw, x1, ws_gate_bf, ws_up_bf, ws_down_bf, gate2, ln_g.reshape(1, D), ln_b.reshape(1, D))


def kernel(x, c, positions, w_ada, b_ada, w_in, lb_logits, attn_norm_g, hgrn_norm_g, w_out, ln1_g, ln1_b,
           w_router, router_bias, expert_w_gate, expert_w_up, expert_w_down, shared_w_gate, shared_w_up,
           shared_w_down, ln2_g, ln2_b):
    B, S, D = x.shape
    T = B * S
    layer = 0
    lower_bounds = jnp.cumsum(jax.nn.softmax(lb_logits.astype(F32), axis=1), axis=1)

    mod = _ada_mod(c, w_ada[layer], b_ada[layer])
    shift1, scale1, gate1, shift2, scale2, gate2 = [m.reshape(B, 1, D) for m in jnp.split(mod, 6, axis=-1)]
    cos_t, sin_t = _rope_tables(positions)
    x2 = x.reshape(T, D)

    proj, qkv4, qkv16 = _in_proj(x2, scale1, shift1, w_in[layer].astype(BF16), cos_t, sin_t, B, S)
    branches = [_attn_branch(qkv, d, B, S)
                for qkv, d in zip((proj.reshape(B, S, -1), qkv4, qkv16), DILATIONS)]
    rec = _hgrn(proj, lower_bounds[0, layer], lower_bounds[1, layer], hgrn_norm_g[layer], B, S)
    x1, u2, u_rows = _mix([o for o, _ in branches], [l for _, l in branches], rec, x2, attn_norm_g[layer],
                  w_out[layer].astype(BF16), gate1, ln1_g[layer], ln1_b[layer], scale2, shift2, S)

    idx, gw, rank, counts = _route(u2, w_router[layer], router_bias[layer])
    bexp, wnext, nused, meta, n_pad = _dispatch_plan(idx, rank, counts, T)
    y_flat = _moe(u_rows, bexp, wnext, nused, meta, expert_w_gate[layer], expert_w_up[layer], expert_w_down[layer], n_pad)
    out = _final(u2, y_flat, gw, x1, shared_w_gate[layer].astype(BF16), shared_w_up[layer].astype(BF16),
                 shared_w_down[layer].astype(BF16), gate2, ln2_g[layer], ln2_b[layer], S)
    return out.reshape(B, S, D)
```

```python
import functools

import jax
import jax.numpy as jnp
from jax import lax
from jax.experimental import pallas as pl
from jax.experimental.pallas import tpu as pltpu

F32 = jnp.float32
BF16 = jnp.bfloat16
I32 = jnp.int32
HIGHEST = lax.Precision.HIGHEST

D_MODEL = 2048
ATTN_HEADS = 16
ATTN_HEAD_DIM = 64
ATTN_WIDTH = ATTN_HEADS * ATTN_HEAD_DIM
HGRN_HEADS = 8
HGRN_HEAD_DIM = 128
HGRN_WIDTH = HGRN_HEADS * HGRN_HEAD_DIM
IN_PROJ_WIDTH = 3 * ATTN_WIDTH + 5 * HGRN_WIDTH
DILATIONS = (1, 4, 16)
ATTN_HALF = 64
ROPE_THETA = 10000.0
N_EXPERTS = 256
N_EXPERT_GROUPS = 8
GROUP_SIZE = N_EXPERTS // N_EXPERT_GROUPS
TOPK_GROUPS = 4
TOP_K = 8
EXPERT_DIM = 512
ROUTED_SCALE = 2.5
DEPTH = 1
ALPHA = (2 * DEPTH) ** 0.25
LN_EPS = 1e-5
RMS_EPS = 1e-6
NEG_INF = -1e30
LOG2E = 1.4426950408889634

LANES = 128
VMEM_LIMIT = 56 * 1024 * 1024

ADA_TN = 1024
ROPE_TM = 2048
INPROJ_TM = 512
INPROJ_TN = 1024
ATTN_TQ = 128
ATTN_WK = 256
ATTN_UNROLL = 8
HGRN_CHUNK = 64
HGRN_SUB = 16
HGRN_STATE_UNROLL = 4
HGRN_INTRA_UNROLL = 4
MIX_TM = 256
ROUTE_TM = 256
ROUTE_SPLIT = 2
MOE_BLOCK = 128
ROW_SUB = D_MODEL // LANES
ROW_PITCH = 24
FINAL_TM = 128


def _cparams(sem):
    return pltpu.CompilerParams(dimension_semantics=sem, vmem_limit_bytes=VMEM_LIMIT)


def _ada_kernel(c_ref, w_ref, b_ref, o_ref):
    c = c_ref[...]
    sc = c * jax.nn.sigmoid(c)
    o_ref[...] = jnp.dot(sc, w_ref[...], preferred_element_type=F32, precision=HIGHEST) + b_ref[...]


def _ada_mod(c, w_ada, b_ada):
    B, D = c.shape
    N = w_ada.shape[1]
    c8 = jnp.zeros((8, D), F32).at[:B].set(c)
    out = pl.pallas_call(
        _ada_kernel,
        grid=(N // ADA_TN,),
        in_specs=[pl.BlockSpec((8, D), lambda j: (0, 0)),
                  pl.BlockSpec((D, ADA_TN), lambda j: (0, j)),
                  pl.BlockSpec((1, ADA_TN), lambda j: (0, j))],
        out_specs=pl.BlockSpec((8, ADA_TN), lambda j: (0, j)),
        out_shape=jax.ShapeDtypeStruct((8, N), F32),
        compiler_params=_cparams(("parallel",)),
        name="ada_mod",
    )(c8, w_ada, b_ada.reshape(1, N))
    return out[:B]


def _rope_kernel(pos_ref, invf_ref, sign_ref, cos_ref, sin_ref):
    ang = pos_ref[...].astype(F32) * invf_ref[...]
    cos_ref[...] = jnp.cos(ang)
    sin_ref[...] = jnp.sin(ang) * sign_ref[...]


def _rope_tables(positions):
    T = positions.size
    half = ATTN_HEAD_DIM // 2
    inv_freq = ROPE_THETA ** (-jnp.arange(half, dtype=F32) / half)
    lane = jnp.arange(LANES)
    invf = inv_freq[lane % half].reshape(1, LANES)
    sign = jnp.where((lane % ATTN_HEAD_DIM) < half, -1.0, 1.0).astype(F32).reshape(1, LANES)
    return pl.pallas_call(
        _rope_kernel,
        grid=(T // ROPE_TM,),
        in_specs=[pl.BlockSpec((ROPE_TM, 1), lambda i: (i, 0)),
                  pl.BlockSpec((1, LANES), lambda i: (0, 0)),
                  pl.BlockSpec((1, LANES), lambda i: (0, 0))],
        out_specs=[pl.BlockSpec((ROPE_TM, LANES), lambda i: (i, 0)),
                   pl.BlockSpec((ROPE_TM, LANES), lambda i: (i, 0))],
        out_shape=[jax.ShapeDtypeStruct((T, LANES), F32)] * 2,
        compiler_params=_cparams(("parallel",)),
        name="rope_tables",
    )(positions.reshape(T, 1), invf, sign)


def _inproj_kernel(x_ref, sc_ref, sh_ref, w_ref, cos_ref, sin_ref, o_ref, o4_ref, o16_ref, stage_ref, u_ref):
    j = pl.program_id(1)

    @pl.when(j == 0)
    def _():
        u_ref[...] = (x_ref[...] * (1.0 + sc_ref[...]) + sh_ref[...]).astype(BF16)

    acc = jnp.dot(u_ref[...], w_ref[...], preferred_element_type=F32)

    @pl.when(j < 2)
    def _():
        qscale = jnp.where(j == 0, ATTN_HEAD_DIM ** -0.5, 1.0).astype(F32)
        cos = cos_ref[...] * qscale
        sin = sin_ref[...] * qscale
        lane = lax.broadcasted_iota(I32, cos.shape, 1)
        first = (lane % ATTN_HEAD_DIM) < (ATTN_HEAD_DIM // 2)
        for cb in range(INPROJ_TN // LANES):
            a = acc[:, cb * LANES:(cb + 1) * LANES]
            partner = jnp.where(first, pltpu.roll(a, LANES - ATTN_HEAD_DIM // 2, 1),
                                pltpu.roll(a, ATTN_HEAD_DIM // 2, 1))
            stage_ref[cb] = a * cos + partner * sin

    @pl.when(j == 2)
    def _():
        for cb in range(INPROJ_TN // LANES):
            stage_ref[cb] = acc[:, cb * LANES:(cb + 1) * LANES]

    @pl.when(j < 3)
    def _():
        for cb in range(INPROJ_TN // LANES):
            o_ref[:, cb * LANES:(cb + 1) * LANES] = stage_ref[cb].astype(BF16)
            for d, od_ref in ((DILATIONS[1], o4_ref), (DILATIONS[2], o16_ref)):
                rows = INPROJ_TM // d
                for r in range(d):
                    c0 = r * INPROJ_TN + cb * LANES
                    od_ref[:, c0:c0 + LANES] = stage_ref.at[cb][pl.ds(r, rows, stride=d), :].astype(BF16)

    @pl.when(j >= 3)
    def _():
        o_ref[...] = acc.astype(BF16)


def _in_proj(x2, scale1, shift1, w_in_bf, cos_t, sin_t, B, S):
    T, D = x2.shape
    P = w_in_bf.shape[1]
    per_b = S // INPROJ_TM
    assert INPROJ_TN == ATTN_WIDTH
    d4, d16 = DILATIONS[1], DILATIONS[2]
    strided_spec = lambda d: pl.BlockSpec((None, INPROJ_TM // d, d * INPROJ_TN),
                                          lambda i, j: (i // per_b, i % per_b, jnp.minimum(j, 2)))
    strided_shape = lambda d: jax.ShapeDtypeStruct((B, S // d, 3 * d * ATTN_WIDTH), BF16)
    return pl.pallas_call(
        _inproj_kernel,
        grid=(T // INPROJ_TM, P // INPROJ_TN),
        in_specs=[pl.BlockSpec((INPROJ_TM, D), lambda i, j: (i, 0)),
                  pl.BlockSpec((None, 1, D), lambda i, j: (i // per_b, 0, 0)),
                  pl.BlockSpec((None, 1, D), lambda i, j: (i // per_b, 0, 0)),
                  pl.BlockSpec((D, INPROJ_TN), lambda i, j: (0, j)),
                  pl.BlockSpec((INPROJ_TM, LANES), lambda i, j: (i, 0)),
                  pl.BlockSpec((INPROJ_TM, LANES), lambda i, j: (i, 0))],
        out_specs=[pl.BlockSpec((INPROJ_TM, INPROJ_TN), lambda i, j: (i, j)), strided_spec(d4), strided_spec(d16)],
        out_shape=[jax.ShapeDtypeStruct((T, P), BF16), strided_shape(d4), strided_shape(d16)],
        scratch_shapes=[pltpu.VMEM((INPROJ_TN // LANES, INPROJ_TM, LANES), F32),
                        pltpu.VMEM((INPROJ_TM, D), BF16)],
        compiler_params=_cparams(("parallel", "arbitrary")),
        name="in_proj",
    )(x2, scale1, shift1, w_in_bf, cos_t, sin_t)


def _attn_kernel(q_ref, k_ref, v_ref, o_ref, lse_ref, *, L, dilation):
    nq = L // ATTN_TQ
    res = pl.program_id(2)
    lane = lax.broadcasted_iota(I32, (1, LANES), 1)
    head0 = lane < ATTN_HEAD_DIM
    rel = (lax.broadcasted_iota(I32, (ATTN_TQ, ATTN_WK), 1)
           - lax.broadcasted_iota(I32, (ATTN_TQ, ATTN_WK), 0))

    def body(i, carry):
        q0 = pl.multiple_of(i * ATTN_TQ, ATTN_TQ)
        ks = pl.multiple_of(jnp.clip(i * ATTN_TQ - ATTN_HALF, 0, L - ATTN_WK), ATTN_HALF)
        q = q_ref[pl.ds(q0, ATTN_TQ), :]
        k = k_ref[pl.ds(ks, ATTN_WK), :]
        v = v_ref[pl.ds(ks, ATTN_WK), :]
        mask = jnp.abs(rel + (ks - q0)) <= ATTN_HALF
        outs, lses = [], []
        for hmask in (head0, jnp.logical_not(head0)):
            qh = jnp.where(hmask, q, jnp.zeros_like(q))
            s = lax.dot_general(qh, k, (((1,), (1,)), ((), ())), preferred_element_type=F32)
            s = jnp.where(mask, s, NEG_INF)
            m = jnp.max(s, axis=-1, keepdims=True)
            p = jnp.exp(s - m)
            l = jnp.sum(p, axis=-1, keepdims=True)
            outs.append(jnp.dot(p.astype(BF16), v, preferred_element_type=F32) / l)
            lses.append(m + jnp.log(l))
        if dilation == 1:
            rows = pl.ds(q0, ATTN_TQ)
        else:
            rows = pl.ds(q0 * dilation + res, ATTN_TQ, stride=dilation)
        o_ref[rows, :] = jnp.where(head0, outs[0], outs[1]).astype(o_ref.dtype)
        lse_ref[rows, :] = jnp.where(head0, lses[0], lses[1])
        return carry

    lax.fori_loop(0, nq, body, 0, unroll=min(ATTN_UNROLL, nq))


def _attn_branch(qkv, dilation, B, S):
    L = S // dilation
    acb = ATTN_WIDTH // LANES
    in_spec = lambda part: pl.BlockSpec((None, L, LANES),
                                        lambda b, h, r: (b, 0, (part * dilation + r) * acb + h))
    out_spec = pl.BlockSpec((None, S, LANES), lambda b, h, r: (b, 0, h))
    o_dtype = BF16 if dilation == 1 else F32
    o, lse = pl.pallas_call(
        functools.partial(_attn_kernel, L=L, dilation=dilation),
        grid=(B, acb, dilation),
        in_specs=[in_spec(0), in_spec(1), in_spec(2)],
        out_specs=[out_spec, out_spec],
        out_shape=[jax.ShapeDtypeStruct((B, S, ATTN_WIDTH), o_dtype),
                   jax.ShapeDtypeStruct((B, S, ATTN_WIDTH), F32)],
        compiler_params=_cparams(("parallel", "parallel", "arbitrary")),
        name=f"attn_d{dilation}",
    )(qkv, qkv, qkv)
    return o.reshape(B * S, ATTN_WIDTH), lse.reshape(B * S, ATTN_WIDTH)


def _hgrn_chunk(q, kk, lf, v_bf, tri, reverse):
    C = HGRN_CHUNK
    SUB = HGRN_SUB
    nsub = C // SUB
    b = jnp.dot(tri, lf, preferred_element_type=F32, precision=HIGHEST) * LOG2E
    c = b - jnp.log2(kk)
    col = lax.broadcasted_iota(I32, (SUB, C), 1)
    row = lax.broadcasted_iota(I32, (SUB, C), 0)
    score_rows = []
    for i in range(nsub):
        r0 = i * SUB
        bi = b[r0:r0 + SUB]
        qi = q[r0:r0 + SUB]
        ci = c[r0:r0 + SUB]
        diag = jnp.zeros((SUB, C), F32)
        for s in range(SUB):
            colv = jnp.sum(qi * jnp.exp2(bi - ci[s:s + 1]), axis=-1, keepdims=True)
            diag = jnp.where(col == r0 + s, colv, diag)
        if reverse:
            keep = (col - r0) >= row
        else:
            keep = (col - r0) <= row
        diag = jnp.where(jnp.logical_and(keep, jnp.logical_and(col >= r0, col < r0 + SUB)), diag, 0.0)
        if reverse:
            has_off = i < nsub - 1
            bref = b[r0 + SUB:r0 + SUB + 1] if has_off else None
            off_mask = col >= r0 + SUB
        else:
            has_off = i > 0
            bref = b[r0 - 1:r0] if has_off else None
            off_mask = col < r0
        if has_off:
            qs = (qi * jnp.exp2(bi - bref)).astype(BF16)
            ks = jnp.exp2(bref - c).astype(BF16)
            off = lax.dot_general(qs, ks, (((1,), (1,)), ((), ())), preferred_element_type=F32)
            score_rows.append(jnp.where(off_mask, off, diag))
        else:
            score_rows.append(diag)
    scores = jnp.concatenate(score_rows, axis=0)
    b_edge = b[0:1] if reverse else b[C - 1:C]
    o = jnp.dot(scores.astype(BF16), v_bf, preferred_element_type=F32)
    qd = (q * jnp.exp2(b)).astype(BF16)
    kd = jnp.exp2(b_edge - c).astype(BF16)
    return o, qd, kd, jnp.exp2(b_edge)


def _hgrn_kernel(q_ref, zf_ref, zb_ref, v_ref, g_ref, lbf_ref, lbb_ref, ng_ref, o_ref,
                 acc_ref, qd_ref, kd_ref, dec_ref, *, S):
    C = HGRN_CHUNK
    n_chunks = S // C
    r = lax.broadcasted_iota(I32, (C, C), 0)
    c = lax.broadcasted_iota(I32, (C, C), 1)
    tri_f = (c <= r).astype(F32)
    tri_b = (c >= r).astype(F32)
    lbf = lbf_ref[...]
    lbb = lbb_ref[...]

    def gates(z, lb):
        sg = jax.nn.sigmoid(z)
        return jnp.log(lb + (1.0 - lb) * sg), (1.0 - lb) * (1.0 - sg)

    def intra_body(n, carry):
        r0 = pl.multiple_of(n * C, C)
        q = q_ref[pl.ds(r0, C), :].astype(F32)
        v_bf = v_ref[pl.ds(r0, C), :]
        total = None
        for d, (z_ref, lb, tri) in enumerate(((zf_ref, lbf, tri_f), (zb_ref, lbb, tri_b))):
            lf, kk = gates(z_ref[pl.ds(r0, C), :].astype(F32), lb)
            o, qd, kd, dec = _hgrn_chunk(q, kk, lf, v_bf, tri, d == 1)
            qd_ref[d, pl.ds(r0, C), :] = qd
            kd_ref[d, pl.ds(r0, C), :] = kd
            dec_ref[d, pl.ds(n, 1), :] = dec
            total = o if total is None else total + o
        acc_ref[pl.ds(r0, C), :] = total
        return carry

    lax.fori_loop(0, n_chunks, intra_body, 0, unroll=HGRN_INTRA_UNROLL)

    def state_body(n, carry):
        states = list(carry)
        for d, blk in enumerate((n, n_chunks - 1 - n)):
            r0 = pl.multiple_of(blk * C, C)
            st = states[d]
            o = lax.dot_general(qd_ref[d, pl.ds(r0, C), :], st.astype(BF16), (((1,), (1,)), ((), ())),
                                preferred_element_type=F32)
            acc_ref[pl.ds(r0, C), :] += o
            upd = lax.dot_general(v_ref[pl.ds(r0, C), :], kd_ref[d, pl.ds(r0, C), :], (((0,), (0,)), ((), ())),
                                  preferred_element_type=F32)
            states[d] = st * dec_ref[d, pl.ds(blk, 1), :] + upd
        return tuple(states)

    z0 = jnp.zeros((HGRN_HEAD_DIM, HGRN_HEAD_DIM), F32)
    lax.fori_loop(0, n_chunks, state_body, (z0, z0), unroll=HGRN_STATE_UNROLL)

    ng = ng_ref[...]

    def norm_body(n, carry):
        r0 = pl.multiple_of(n * 512, 512)
        t = acc_ref[pl.ds(r0, 512), :]
        g = g_ref[pl.ds(r0, 512), :].astype(F32)
        y = t * lax.rsqrt(jnp.mean(t * t, axis=-1, keepdims=True) + RMS_EPS) * ng
        o_ref[pl.ds(r0, 512), :] = (y * (g * jax.nn.sigmoid(g))).astype(BF16)
        return carry

    lax.fori_loop(0, S // 512, norm_body, 0)


def _hgrn(proj, lb_fwd, lb_bwd, norm_g, B, S):
    P = proj.shape[1]
    pv = proj.reshape(B, S, P)
    base = 3 * ATTN_WIDTH // LANES
    nh = HGRN_HEADS
    in_spec = lambda k: pl.BlockSpec((None, S, LANES), lambda b, h: (b, 0, base + k * nh + h))
    vec_spec = pl.BlockSpec((None, 1, LANES), lambda b, h: (h, 0, 0))
    out = pl.pallas_call(
        functools.partial(_hgrn_kernel, S=S),
        grid=(B, nh),
        in_specs=[in_spec(0), in_spec(1), in_spec(2), in_spec(3), in_spec(4), vec_spec, vec_spec, vec_spec],
        out_specs=pl.BlockSpec((None, S, LANES), lambda b, h: (b, 0, h)),
        out_shape=jax.ShapeDtypeStruct((B, S, HGRN_WIDTH), BF16),
        scratch_shapes=[pltpu.VMEM((S, LANES), F32),
                        pltpu.VMEM((2, S, LANES), BF16),
                        pltpu.VMEM((2, S, LANES), BF16),
                        pltpu.VMEM((2, S // HGRN_CHUNK, LANES), F32)],
        compiler_params=_cparams(("parallel", "parallel")),
        name="hgrn2",
    )(pv, pv, pv, pv, pv, lb_fwd.reshape(nh, 1, LANES), lb_bwd.reshape(nh, 1, LANES),
      norm_g.reshape(nh, 1, LANES))
    return out.reshape(B * S, HGRN_WIDTH)


def _layer_norm_rows(y, g, b):
    mu = jnp.mean(y, axis=-1, keepdims=True)
    d = y - mu
    var = jnp.mean(d * d, axis=-1, keepdims=True)
    return d * lax.rsqrt(var + LN_EPS) * g + b


def _mix_kernel(o1_ref, o2_ref, o3_ref, l1_ref, l2_ref, l3_ref, rec_ref, x_ref, grp_ref, ag_ref, w_ref,
                gate_ref, lng_ref, lnb_ref, sc_ref, sh_ref, x1_ref, u2_ref, urows_ref):
    l1, l2, l3 = l1_ref[...], l2_ref[...], l3_ref[...]
    m = jnp.maximum(jnp.maximum(l1, l2), l3)
    e1, e2, e3 = jnp.exp(l1 - m), jnp.exp(l2 - m), jnp.exp(l3 - m)
    attn = (e1 * o1_ref[...].astype(F32) + e2 * o2_ref[...].astype(F32)
            + e3 * o3_ref[...].astype(F32)) / (e1 + e2 + e3)
    ms = jnp.dot((attn * attn).astype(BF16), grp_ref[...], preferred_element_type=F32)
    normed = attn * lax.rsqrt(ms + RMS_EPS) * ag_ref[...]
    mixed = jnp.concatenate([normed.astype(BF16), rec_ref[...]], axis=-1)
    mix = jnp.dot(mixed, w_ref[...], preferred_element_type=F32)
    x1 = _layer_norm_rows(ALPHA * x_ref[...] + gate_ref[...] * mix, lng_ref[...], lnb_ref[...])
    x1_ref[...] = x1
    u2 = x1 * (1.0 + sc_ref[...]) + sh_ref[...]
    u2_ref[...] = u2
    for cb in range(ROW_SUB):
        urows_ref[pl.ds(cb, MIX_TM, stride=ROW_SUB), :] = u2[:, cb * LANES:(cb + 1) * LANES]


def _mix(o_branches, lse_branches, rec, x2, attn_norm_g, w_out_bf, gate1, ln_g, ln_b, scale2, shift2, S):
    T, D = x2.shape
    per_b = S // MIX_TM
    head = jnp.arange(ATTN_WIDTH) // ATTN_HEAD_DIM
    grp = jnp.where(head[:, None] == head[None, :], 1.0 / ATTN_HEAD_DIM, 0.0).astype(BF16)
    row = lambda w: pl.BlockSpec((MIX_TM, w), lambda i: (i, 0))
    const = lambda shape: pl.BlockSpec(shape, lambda i: (0,) * len(shape))
    per_batch = pl.BlockSpec((None, 1, D), lambda i: (i // per_b, 0, 0))
    return pl.pallas_call(
        _mix_kernel,
        grid=(T // MIX_TM,),
        in_specs=[row(ATTN_WIDTH)] * 6 + [row(HGRN_WIDTH), row(D), const((ATTN_WIDTH, ATTN_WIDTH)),
                  const((1, ATTN_WIDTH)), const((D, D)), per_batch, const((1, D)), const((1, D)),
                  per_batch, per_batch],
        out_specs=[row(D), row(D), pl.BlockSpec((MIX_TM * ROW_SUB, LANES), lambda i: (i, 0))],
        out_shape=[jax.ShapeDtypeStruct((T, D), F32)] * 2 + [jax.ShapeDtypeStruct((T * ROW_SUB, LANES), F32)],
        compiler_params=_cparams(("parallel",)),
        name="mix_out_ln1",
    )(*o_branches, *lse_branches, rec, x2, grp, attn_norm_g.reshape(1, -1), w_out_bf, gate1,
      ln_g.reshape(1, D), ln_b.reshape(1, D), scale2, shift2)


def _route_kernel(u_ref, w_ref, bias_ref, idx_ref, gw_ref, rank_ref, cnt_ref, run_ref):
    i = pl.program_id(0)

    @pl.when(i == 0)
    def _():
        run_ref[...] = jnp.zeros_like(run_ref)

    tm = ROUTE_TM
    hm = tm // ROUTE_SPLIT
    logits = jnp.dot(u_ref[...], w_ref[...], preferred_element_type=F32, precision=HIGHEST)
    scores_all = jax.nn.sigmoid(logits)
    sel_all = scores_all + bias_ref[...]
    lane = lax.broadcasted_iota(I32, (hm, N_EXPERTS), 1)
    lane_f = lane.astype(F32)
    group = lane // GROUP_SIZE
    lane_o = lax.broadcasted_iota(I32, (hm, LANES), 1)
    neg = jnp.float32(-jnp.inf)

    def first_argmax(vals):
        mx = jnp.max(vals, axis=-1, keepdims=True)
        idx = jnp.min(jnp.where(vals == mx, lane_f, float(N_EXPERTS)), axis=-1, keepdims=True)
        return mx, idx

    def select(scores, sel):
        gscore = []
        for g in range(N_EXPERT_GROUPS):
            vals = jnp.where(group == g, sel, neg)
            m1, i1 = first_argmax(vals)
            m2 = jnp.max(jnp.where(lane_f == i1, neg, vals), axis=-1, keepdims=True)
            gscore.append(m1 + m2)
        keep_f = jnp.zeros((hm, N_EXPERTS), F32)
        for g in range(N_EXPERT_GROUPS):
            beaten = jnp.zeros((hm, 1), F32)
            for h in range(N_EXPERT_GROUPS):
                if h == g:
                    continue
                ahead = (gscore[h] >= gscore[g]) if h < g else (gscore[h] > gscore[g])
                beaten = beaten + jnp.where(ahead, 1.0, 0.0)
            gkeep = jnp.where(beaten < TOPK_GROUPS, 1.0, 0.0)
            keep_f = jnp.where(group == g, gkeep, keep_f)
        vals = jnp.where(keep_f > 0.5, sel, neg)

        idx_out = jnp.zeros((hm, LANES), F32)
        gw_out = jnp.zeros((hm, LANES), F32)
        chosen = jnp.zeros((hm, N_EXPERTS), F32)
        picks = []
        wsum = jnp.zeros((hm, 1), F32)
        for k in range(TOP_K):
            _, ik = first_argmax(vals)
            hit = lane_f == ik
            sk = jnp.sum(jnp.where(hit, scores, 0.0), axis=-1, keepdims=True)
            vals = jnp.where(hit, neg, vals)
            chosen = jnp.where(hit, 1.0, chosen)
            picks.append((ik, sk))
            wsum = wsum + sk
            idx_out = jnp.where(lane_o == k, ik, idx_out)
        for k, (ik, sk) in enumerate(picks):
            gw_out = jnp.where(lane_o == k, sk / wsum * ROUTED_SCALE, gw_out)
        return idx_out, gw_out, chosen, [ik for ik, _ in picks]

    parts = [select(scores_all[h * hm:(h + 1) * hm], sel_all[h * hm:(h + 1) * hm]) for h in range(ROUTE_SPLIT)]
    chosen = jnp.concatenate([p[2] for p in parts], axis=0)

    r = lax.broadcasted_iota(I32, (tm, tm), 0)
    c = lax.broadcasted_iota(I32, (tm, tm), 1)
    strict_lower = jnp.where(c < r, 1.0, 0.0).astype(BF16)
    before = jnp.dot(strict_lower, chosen.astype(BF16), preferred_element_type=F32) + run_ref[...]
    for h, (idx_out, gw_out, _, iks) in enumerate(parts):
        rows = slice(h * hm, (h + 1) * hm)
        rank_out = jnp.zeros((hm, LANES), F32)
        for k, ik in enumerate(iks):
            rk = jnp.sum(jnp.where(lane_f == ik, before[rows], 0.0), axis=-1, keepdims=True)
            rank_out = jnp.where(lane_o == k, rk, rank_out)
        idx_ref[rows, :] = idx_out.astype(I32)
        gw_ref[rows, :] = gw_out
        rank_ref[rows, :] = rank_out.astype(I32)
    run_ref[...] = run_ref[...] + jnp.sum(chosen, axis=0, keepdims=True)
    cnt_ref[...] = run_ref[...]


def _route(u2, w_router, router_bias):
    T, D = u2.shape
    row = lambda w: pl.BlockSpec((ROUTE_TM, w), lambda i: (i, 0))
    idx, gw, rank, cnt = pl.pallas_call(
        _route_kernel,
        grid=(T // ROUTE_TM,),
        in_specs=[row(D), pl.BlockSpec((D, N_EXPERTS), lambda i: (0, 0)),
                  pl.BlockSpec((1, N_EXPERTS), lambda i: (0, 0))],
        out_specs=[row(LANES), row(LANES), row(LANES), pl.BlockSpec((1, N_EXPERTS), lambda i: (0, 0))],
        out_shape=[jax.ShapeDtypeStruct((T, LANES), I32), jax.ShapeDtypeStruct((T, LANES), F32),
                   jax.ShapeDtypeStruct((T, LANES), I32), jax.ShapeDtypeStruct((1, N_EXPERTS), F32)],
        scratch_shapes=[pltpu.VMEM((1, N_EXPERTS), F32)],
        compiler_params=_cparams(("arbitrary",)),
        name="router",
    )(u2, w_router, router_bias.reshape(1, N_EXPERTS))
    return idx, gw, rank, cnt[0].astype(I32)


def _slot_kernel(idx_ref, rank_ref, start_ref, dest_ref):
    tm = ROUTE_TM
    lane_e = lax.broadcasted_iota(I32, (tm, N_EXPERTS), 1)
    lane_o = lax.broadcasted_iota(I32, (tm, LANES), 1)
    idx = idx_ref[...]
    start = start_ref[...]
    base = jnp.zeros((tm, LANES), F32)
    for k in range(TOP_K):
        sk = jnp.sum(jnp.where(lane_e == idx[:, k:k + 1], start, 0.0), axis=-1, keepdims=True)
        base = jnp.where(lane_o == k, sk, base)
    dest_ref[...] = base.astype(I32) + rank_ref[...]


def _slots(idx, rank, start):
    T = idx.shape[0]
    row = pl.BlockSpec((ROUTE_TM, LANES), lambda i: (i, 0))
    dest = pl.pallas_call(
        _slot_kernel,
        grid=(T // ROUTE_TM,),
        in_specs=[row, row, pl.BlockSpec((1, N_EXPERTS), lambda i: (0, 0))],
        out_specs=row,
        out_shape=jax.ShapeDtypeStruct((T, LANES), I32),
        compiler_params=_cparams(("parallel",)),
        name="slots",
    )(idx, rank, start.astype(F32).reshape(1, N_EXPERTS))
    return dest[:, :TOP_K]


MOE_META_SLOTS = 8
MOE_BUFS = 3


def _moe_kernel(bexp_ref, wnext_ref, nused_ref, meta_hbm, u_hbm, wg_hbm, wu_hbm, wd_hbm, y_hbm,
                meta_smem, xb0, xb1, xb2, yb0, yb1, yb2, wg_st, wu_st, wd_st, wg_bf, wu_bf, wd_bf,
                sem_meta, sem_g, sem_s, sem_w):
    i = pl.program_id(0)
    n_blocks = pl.num_programs(0)
    nused = nused_ref[0]
    last = nused - 1
    xbufs = (xb0, xb1, xb2)
    ybufs = (yb0, yb1, yb2)

    def slot(blk):
        return blk & (MOE_META_SLOTS - 1)

    def meta_copy(blk, sl):
        row = pl.ds(pl.multiple_of(blk * (2 * MOE_BLOCK), 2 * MOE_BLOCK), 2 * MOE_BLOCK)
        return pltpu.make_async_copy(meta_hbm.at[row], meta_smem.at[sl], sem_meta.at[sl])

    def weight_copies(e):
        return (pltpu.make_async_copy(wg_hbm.at[e], wg_st, sem_w.at[0]),
                pltpu.make_async_copy(wu_hbm.at[e], wu_st, sem_w.at[1]),
                pltpu.make_async_copy(wd_hbm.at[e], wd_st, sem_w.at[2]))

    def gather_rows(sl, xdst, sem):
        for j in range(MOE_BLOCK):
            row0 = pl.multiple_of(meta_smem[sl, j] * ROW_SUB, ROW_SUB)
            pltpu.make_async_copy(u_hbm.at[pl.ds(row0, ROW_SUB)], xdst.at[pl.ds(j * ROW_PITCH, ROW_SUB)], sem).start()

    def scatter_rows(sl, ysrc, sem):
        for j in range(MOE_BLOCK):
            dst = meta_smem[sl, MOE_BLOCK + j]
            row0 = pl.multiple_of(dst * ROW_SUB, ROW_SUB)
            pltpu.make_async_copy(ysrc.at[pl.ds(j * ROW_PITCH, ROW_SUB)], y_hbm.at[pl.ds(row0, ROW_SUB)],
                                  sem).start(priority=1)

    def wait_gather(sem):
        n = MOE_BLOCK * ROW_SUB
        pltpu.make_async_copy(u_hbm.at[pl.ds(0, n)], xb0.at[pl.ds(0, n)], sem).wait()

    def wait_scatter(sem):
        n = MOE_BLOCK * ROW_SUB
        pltpu.make_async_copy(yb0.at[pl.ds(0, n)], yb1.at[pl.ds(0, n)], sem).wait()

    @pl.when(i >= nused)
    def _():
        @pl.when(i == nused)
        def _():
            yb0[...] = jnp.zeros_like(yb0)

        @pl.when(i > nused)
        def _():
            wait_scatter(sem_s.at[0])

        n = MOE_BLOCK * ROW_SUB
        pltpu.make_async_copy(yb0.at[pl.ds(0, n)], y_hbm.at[pl.ds(pl.multiple_of(i * n, n), n)], sem_s.at[0]).start()

        @pl.when(i == n_blocks - 1)
        def _():
            wait_scatter(sem_s.at[0])

    @pl.when(i < nused)
    def _():
        @pl.when(i == 0)
        def _():
            for blk, sl in ((0, 0), (n_blocks, slot(-1)), (jnp.minimum(1, last), 1)):
                cp = meta_copy(blk, sl)
                cp.start()
                cp.wait()
            meta_copy(jnp.minimum(2, last), 2).start()
            yb2[...] = jnp.zeros_like(yb2)
            gather_rows(0, xb0, sem_g.at[0])
            gather_rows(1, xb1, sem_g.at[1])
            for cp in weight_copies(bexp_ref[0]):
                cp.start(priority=1)

        meta_copy(jnp.minimum(i + 2, last), slot(i + 2)).wait()
        meta_copy(jnp.minimum(i + 3, last), slot(i + 3)).start()

        @pl.when((i == 0) | (bexp_ref[i] != bexp_ref[jnp.maximum(i - 1, 0)]))
        def _():
            for cp in weight_copies(bexp_ref[i]):
                cp.wait()
            wg_bf[...] = wg_st[...].astype(BF16)
            wu_bf[...] = wu_st[...].astype(BF16)
            wd_bf[...] = wd_st[...].astype(BF16)
            nxt_e = wnext_ref[i]

            @pl.when(nxt_e >= 0)
            def _():
                for cp in weight_copies(nxt_e):
                    cp.start(priority=1)

    def step(r):
        r_prev = (r + MOE_BUFS - 1) % MOE_BUFS
        xcur, ycur = xbufs[r], ybufs[r]
        wait_gather(sem_g.at[r])
        gather_rows(slot(i + 2), xbufs[r_prev], sem_g.at[r_prev])
        scatter_rows(slot(i - 1), ybufs[r_prev], sem_s.at[r_prev])
        x = jnp.concatenate([xcur[pl.ds(cb, MOE_BLOCK, stride=ROW_PITCH), :].astype(BF16)
                             for cb in range(ROW_SUB)], axis=1)
        hg = jnp.dot(x, wg_bf[...], preferred_element_type=F32)
        hu = jnp.dot(x, wu_bf[...], preferred_element_type=F32)
        act = (hg * jax.nn.sigmoid(hg) * hu).astype(BF16)

        @pl.when(i >= 2)
        def _():
            wait_scatter(sem_s.at[r])

        y = jnp.dot(act, wd_bf[...], preferred_element_type=F32)
        for cb in range(ROW_SUB):
            ycur[pl.ds(cb, MOE_BLOCK, stride=ROW_PITCH), :] = y[:, cb * LANES:(cb + 1) * LANES]

        @pl.when(i == last)
        def _():
            scatter_rows(slot(i), ycur, sem_s.at[r])

    for r in range(MOE_BUFS):
        @pl.when((i < nused) & (i % MOE_BUFS == r))
        def _(r=r):
            step(r)

    @pl.when(i == last)
    def _():
        wait_scatter(sem_s.at[i % MOE_BUFS])
        wait_scatter(sem_s.at[(i + 2) % MOE_BUFS])

        @pl.when(i >= 1)
        def _():
            wait_scatter(sem_s.at[(i + 1) % MOE_BUFS])

        wait_gather(sem_g.at[(i + 1) % MOE_BUFS])
        wait_gather(sem_g.at[(i + 2) % MOE_BUFS])
        meta_copy(jnp.minimum(i + 3, last), slot(i + 3)).wait()


def _moe(u_rows, bexp, wnext, nused, meta, w_gate, w_up, w_down, n_pad):
    D = ROW_SUB * LANES
    n_blocks = n_pad // MOE_BLOCK
    E = EXPERT_DIM
    grid_spec = pltpu.PrefetchScalarGridSpec(
        num_scalar_prefetch=3,
        grid=(n_blocks,),
        in_specs=[pl.BlockSpec(memory_space=pl.ANY)] * 5,
        out_specs=pl.BlockSpec(memory_space=pl.ANY),
        scratch_shapes=[pltpu.SMEM((MOE_META_SLOTS, 2 * MOE_BLOCK), I32)]
                       + [pltpu.VMEM((MOE_BLOCK * ROW_PITCH, LANES), F32)] * MOE_BUFS
                       + [pltpu.VMEM((MOE_BLOCK * ROW_PITCH, LANES), F32)] * MOE_BUFS + [
                        pltpu.VMEM((D, E), F32),
                        pltpu.VMEM((D, E), F32),
                        pltpu.VMEM((E, D), F32),
                        pltpu.VMEM((D, E), BF16),
                        pltpu.VMEM((D, E), BF16),
                        pltpu.VMEM((E, D), BF16),
                        pltpu.SemaphoreType.DMA((MOE_META_SLOTS,)),
                        pltpu.SemaphoreType.DMA((MOE_BUFS,)),
                        pltpu.SemaphoreType.DMA((MOE_BUFS,)),
                        pltpu.SemaphoreType.DMA((3,))])
    return pl.pallas_call(
        _moe_kernel,
        grid_spec=grid_spec,
        out_shape=jax.ShapeDtypeStruct(((n_pad + MOE_BLOCK) * ROW_SUB, LANES), F32),
        compiler_params=_cparams(("arbitrary",)),
        name="moe_experts",
    )(bexp, wnext, nused, meta, u_rows, w_gate, w_up, w_down)


def _dispatch_plan(idx, rank, counts, T):
    n_assign = T * TOP_K
    n_blocks = -(-(n_assign + N_EXPERTS * (MOE_BLOCK - 1)) // MOE_BLOCK)
    n_pad = n_blocks * MOE_BLOCK
    padded = (counts + MOE_BLOCK - 1) // MOE_BLOCK * MOE_BLOCK
    padded_end = jnp.cumsum(padded)
    start = padded_end - padded
    dest = _slots(idx, rank, start)
    src = jnp.full((n_pad,), -1, I32).at[dest.reshape(-1)].set(
        jnp.arange(n_assign, dtype=I32), unique_indices=True, mode='promise_in_bounds')
    is_pad = src < 0
    pad_rank = jnp.cumsum(is_pad.astype(I32)) - 1
    tok = jnp.where(is_pad, 0, src // TOP_K)
    dst = jnp.where(is_pad, n_assign + pad_rank, src)
    tok = jnp.concatenate([tok, jnp.zeros((MOE_BLOCK,), I32)]).reshape(n_blocks + 1, MOE_BLOCK)
    dst = jnp.concatenate([dst, n_pad + jnp.arange(MOE_BLOCK, dtype=I32)]).reshape(n_blocks + 1, MOE_BLOCK)
    meta = jnp.concatenate([tok, dst], axis=1).reshape(-1)
    block_row0 = jnp.arange(n_blocks, dtype=I32) * MOE_BLOCK
    bexp = jnp.minimum(jnp.sum((padded_end[None, :] <= block_row0[:, None]).astype(I32), axis=1),
                       N_EXPERTS - 1).astype(I32)
    nused = (padded_end[-1] // MOE_BLOCK).astype(I32).reshape(1)
    eid = jnp.arange(N_EXPERTS, dtype=I32)
    later = (eid[None, :] > eid[:, None]) & (counts[None, :] > 0)
    next_e = jnp.min(jnp.where(later, eid[None, :], N_EXPERTS), axis=1)
    onehot = (bexp[:, None] == eid[None, :]).astype(I32)
    wnext = jnp.sum(onehot * next_e[None, :], axis=1)
    wnext = jnp.where(wnext < N_EXPERTS, wnext, -1).astype(I32)
    return bexp, wnext, nused, meta.astype(I32), n_pad


def _final_kernel(u_ref, y_ref, gw_ref, x1_ref, wg_ref, wu_ref, wd_ref, gate_ref, lng_ref, lnb_ref, o_ref):
    tm = FINAL_TM
    nrow = tm * TOP_K
    u = u_ref[...].astype(BF16)
    hg = jnp.dot(u, wg_ref[...], preferred_element_type=F32)
    hu = jnp.dot(u, wu_ref[...], preferred_element_type=F32)
    act = (hg * jax.nn.sigmoid(hg) * hu).astype(BF16)
    ffn = jnp.dot(act, wd_ref[...], preferred_element_type=F32)
    expand = jnp.where((lax.broadcasted_iota(I32, (nrow, tm), 0) // TOP_K) == lax.broadcasted_iota(I32, (nrow, tm), 1),
                       1.0, 0.0)
    gw_rows = jnp.dot(expand, gw_ref[...], preferred_element_type=F32, precision=HIGHEST)
    pick = lax.broadcasted_iota(I32, (nrow, LANES), 1) == (lax.broadcasted_iota(I32, (nrow, LANES), 0) % TOP_K)
    wcol = jnp.sum(jnp.where(pick, gw_rows, 0.0), axis=-1, keepdims=True)
    yrows = jnp.concatenate([y_ref[pl.ds(cb, nrow, stride=ROW_SUB), :] for cb in range(ROW_SUB)], axis=1)
    yw = (yrows * wcol).astype(BF16)
    fold = jnp.where((lax.broadcasted_iota(I32, (tm, nrow), 1) // TOP_K) == lax.broadcasted_iota(I32, (tm, nrow), 0),
                     1.0, 0.0).astype(BF16)
    ffn = ffn + jnp.dot(fold, yw, preferred_element_type=F32)
    o_ref[...] = _layer_norm_rows(ALPHA * x1_ref[...] + gate_ref[...] * ffn, lng_ref[...], lnb_ref[...])


def _final(u2, y_flat, gw, x1, ws_gate_bf, ws_up_bf, ws_down_bf, gate2, ln_g, ln_b, S):
    T, D = u2.shape
    E = ws_gate_bf.shape[1]
    per_b = S // FINAL_TM
    row = lambda w: pl.BlockSpec((FINAL_TM, w), lambda i: (i, 0))
    const = lambda shape: pl.BlockSpec(shape, lambda i: (0,) * len(shape))
    return pl.pallas_call(
        _final_kernel,
        grid=(T // FINAL_TM,),
        in_specs=[row(D), pl.BlockSpec((FINAL_TM * TOP_K * ROW_SUB, LANES), lambda i: (i, 0)), row(LANES), row(D),
                  const((D, E)), const((D, E)), const((E, D)),
                  pl.BlockSpec((None, 1, D), lambda i: (i // per_b, 0, 0)), const((1, D)), const((1, D))],
        out_specs=row(D),
        out_shape=jax.ShapeDtypeStruct((T, D), F32),
        compiler_params=_cparams(("parallel",)),
        name="shared_combine_ln2",
    )(u2, y_flat, gw, x1, ws_gate_bf, ws_up_bf, ws_down_bf, gate2, ln_g.reshape(1, D), ln_b.reshape(1, D))


def kernel(x, c, positions, w_ada, b_ada, w_in, lb_logits, attn_norm_g, hgrn_norm_g, w_out, ln1_g, ln1_b,
           w_router, router_bias, expert_w_gate, expert_w_up, expert_w_down, shared_w_gate, shared_w_up,
           shared_w_down, ln2_g, ln2_b):
    B, S, D = x.shape
    T = B * S
    layer = 0
    lower_bounds = jnp.cumsum(jax.nn.softmax(lb_logits.astype(F32), axis=1), axis=1)

    mod = _ada_mod(c, w_ada[layer], b_ada[layer])
    shift1, scale1, gate1, shift2, scale2, gate2 = [m.reshape(B, 1, D) for m in jnp.split(mod, 6, axis=-1)]
    cos_t, sin_t = _rope_tables(positions)
    x2 = x.reshape(T, D)

    proj, qkv4, qkv16 = _in_proj(x2, scale1, shift1, w_in[layer].astype(BF16), cos_t, sin_t, B, S)
    branches = [_attn_branch(qkv, d, B, S)
                for qkv, d in zip((proj.reshape(B, S, -1), qkv4, qkv16), DILATIONS)]
    rec = _hgrn(proj, lower_bounds[0, layer], lower_bounds[1, layer], hgrn_norm_g[layer], B, S)
    x1, u2, u_rows = _mix([o for o, _ in branches], [l for _, l in branches], rec, x2, attn_norm_g[layer],
                  w_out[layer].astype(BF16), gate1, ln1_g[layer], ln1_b[layer], scale2, shift2, S)

    idx, gw, rank, counts = _route(u2, w_router[layer], router_bias[layer])
    bexp, wnext, nused, meta, n_pad = _dispatch_plan(idx, rank, counts, T)
    y_flat = _moe(u_rows, bexp, wnext, nused, meta, expert_w_gate[layer], expert_w_up[layer], expert_w_down[layer], n_pad)
    out = _final(u2, y_flat, gw, x1, shared_w_gate[layer].astype(BF16), shared_w_up[layer].astype(BF16),
                 shared_w_down[layer].astype(BF16), gate2, ln2_g[layer], ln2_b[layer], S)
    return out.reshape(B, S, D)
```

```python
import functools

import jax
import jax.numpy as jnp
from jax import lax
from jax.experimental import pallas as pl
from jax.experimental.pallas import tpu as pltpu

F32 = jnp.float32
BF16 = jnp.bfloat16
I32 = jnp.int32
HIGHEST = lax.Precision.HIGHEST

D_MODEL = 2048
ATTN_HEADS = 16
ATTN_HEAD_DIM = 64
ATTN_WIDTH = ATTN_HEADS * ATTN_HEAD_DIM
HGRN_HEADS = 8
HGRN_HEAD_DIM = 128
HGRN_WIDTH = HGRN_HEADS * HGRN_HEAD_DIM
IN_PROJ_WIDTH = 3 * ATTN_WIDTH + 5 * HGRN_WIDTH
DILATIONS = (1, 4, 16)
ATTN_HALF = 64
ROPE_THETA = 10000.0
N_EXPERTS = 256
N_EXPERT_GROUPS = 8
GROUP_SIZE = N_EXPERTS // N_EXPERT_GROUPS
TOPK_GROUPS = 4
TOP_K = 8
EXPERT_DIM = 512
ROUTED_SCALE = 2.5
DEPTH = 1
ALPHA = (2 * DEPTH) ** 0.25
LN_EPS = 1e-5
RMS_EPS = 1e-6
NEG_INF = -1e30
LOG2E = 1.4426950408889634

LANES = 128
SUBLANES = 8
VMEM_LIMIT = 56 * 1024 * 1024

ADA_TN = 1024
ROPE_TM = 2048
INPROJ_TM = 1024
INPROJ_TN = 1024
ATTN_TQ = 128
ATTN_WK = 256
ATTN_UNROLL = 8
HGRN_CHUNK = 64
HGRN_SUB = 16
HGRN_NORM_ROWS = 512
HGRN_STATE_UNROLL = 4
HGRN_INTRA_UNROLL = 4
MIX_TM = 256
ROUTE_TM = 256
ROUTE_SPLIT = 2
MOE_BLOCK = 128
ROW_SUB = D_MODEL // LANES
ROW_PITCH = 24
FINAL_TM = 128


def _cparams(sem):
    return pltpu.CompilerParams(dimension_semantics=sem, vmem_limit_bytes=VMEM_LIMIT)


def _ada_kernel(c_ref, w_ref, b_ref, o_ref):
    c = c_ref[...]
    sc = c * jax.nn.sigmoid(c)
    o_ref[...] = jnp.dot(sc, w_ref[...], preferred_element_type=F32, precision=HIGHEST) + b_ref[...]


def _ada_mod(c, w_ada, b_ada):
    B, D = c.shape
    N = w_ada.shape[1]
    c8 = jnp.zeros((SUBLANES, D), F32).at[:B].set(c)
    out = pl.pallas_call(
        _ada_kernel,
        grid=(N // ADA_TN,),
        in_specs=[pl.BlockSpec((SUBLANES, D), lambda j: (0, 0)),
                  pl.BlockSpec((D, ADA_TN), lambda j: (0, j)),
                  pl.BlockSpec((1, ADA_TN), lambda j: (0, j))],
        out_specs=pl.BlockSpec((SUBLANES, ADA_TN), lambda j: (0, j)),
        out_shape=jax.ShapeDtypeStruct((SUBLANES, N), F32),
        compiler_params=_cparams(("parallel",)),
        name="ada_mod",
    )(c8, w_ada, b_ada.reshape(1, N))
    return out[:B]


def _rope_kernel(pos_ref, invf_ref, sign_ref, cos_ref, sin_ref):
    ang = pos_ref[...].astype(F32) * invf_ref[...]
    cos_ref[...] = jnp.cos(ang)
    sin_ref[...] = jnp.sin(ang) * sign_ref[...]


def _rope_tables(positions):
    T = positions.size
    half = ATTN_HEAD_DIM // 2
    inv_freq = ROPE_THETA ** (-jnp.arange(half, dtype=F32) / half)
    lane = jnp.arange(LANES)
    invf = inv_freq[lane % half].reshape(1, LANES)
    sign = jnp.where((lane % ATTN_HEAD_DIM) < half, -1.0, 1.0).astype(F32).reshape(1, LANES)
    return pl.pallas_call(
        _rope_kernel,
        grid=(T // ROPE_TM,),
        in_specs=[pl.BlockSpec((ROPE_TM, 1), lambda i: (i, 0)),
                  pl.BlockSpec((1, LANES), lambda i: (0, 0)),
                  pl.BlockSpec((1, LANES), lambda i: (0, 0))],
        out_specs=[pl.BlockSpec((ROPE_TM, LANES), lambda i: (i, 0)),
                   pl.BlockSpec((ROPE_TM, LANES), lambda i: (i, 0))],
        out_shape=[jax.ShapeDtypeStruct((T, LANES), F32)] * 2,
        compiler_params=_cparams(("parallel",)),
        name="rope_tables",
    )(positions.reshape(T, 1), invf, sign)


def _inproj_kernel(x_ref, sc_ref, sh_ref, w_ref, cos_ref, sin_ref, o_ref, o4_ref, o16_ref, stage_ref):
    j = pl.program_id(1)
    u = (x_ref[...] * (1.0 + sc_ref[...]) + sh_ref[...]).astype(BF16)
    acc = jnp.dot(u, w_ref[...], preferred_element_type=F32)

    @pl.when(j < 2)
    def _():
        qscale = jnp.where(j == 0, ATTN_HEAD_DIM ** -0.5, 1.0).astype(F32)
        cos = cos_ref[...] * qscale
        sin = sin_ref[...] * qscale
        lane = lax.broadcasted_iota(I32, cos.shape, 1)
        first = (lane % ATTN_HEAD_DIM) < (ATTN_HEAD_DIM // 2)
        for cb in range(INPROJ_TN // LANES):
            a = acc[:, cb * LANES:(cb + 1) * LANES]
            partner = jnp.where(first, pltpu.roll(a, LANES - ATTN_HEAD_DIM // 2, 1),
                                pltpu.roll(a, ATTN_HEAD_DIM // 2, 1))
            stage_ref[cb] = a * cos + partner * sin

    @pl.when(j == 2)
    def _():
        for cb in range(INPROJ_TN // LANES):
            stage_ref[cb] = acc[:, cb * LANES:(cb + 1) * LANES]

    @pl.when(j < 3)
    def _():
        for cb in range(INPROJ_TN // LANES):
            o_ref[:, cb * LANES:(cb + 1) * LANES] = stage_ref[cb].astype(BF16)
            for d, od_ref in ((DILATIONS[1], o4_ref), (DILATIONS[2], o16_ref)):
                rows = INPROJ_TM // d
                for r in range(d):
                    c0 = r * INPROJ_TN + cb * LANES
                    od_ref[:, c0:c0 + LANES] = stage_ref.at[cb][pl.ds(r, rows, stride=d), :].astype(BF16)

    @pl.when(j >= 3)
    def _():
        o_ref[...] = acc.astype(BF16)


def _in_proj(x2, scale1, shift1, w_in_bf, cos_t, sin_t, B, S):
    T, D = x2.shape
    P = w_in_bf.shape[1]
    per_b = S // INPROJ_TM
    assert INPROJ_TN == ATTN_WIDTH
    d4, d16 = DILATIONS[1], DILATIONS[2]
    strided_spec = lambda d: pl.BlockSpec((None, INPROJ_TM // d, d * INPROJ_TN),
                                          lambda i, j: (i // per_b, i % per_b, jnp.minimum(j, 2)))
    strided_shape = lambda d: jax.ShapeDtypeStruct((B, S // d, 3 * d * ATTN_WIDTH), BF16)
    return pl.pallas_call(
        _inproj_kernel,
        grid=(T // INPROJ_TM, P // INPROJ_TN),
        in_specs=[pl.BlockSpec((INPROJ_TM, D), lambda i, j: (i, 0)),
                  pl.BlockSpec((None, 1, D), lambda i, j: (i // per_b, 0, 0)),
                  pl.BlockSpec((None, 1, D), lambda i, j: (i // per_b, 0, 0)),
                  pl.BlockSpec((D, INPROJ_TN), lambda i, j: (0, j)),
                  pl.BlockSpec((INPROJ_TM, LANES), lambda i, j: (i, 0)),
                  pl.BlockSpec((INPROJ_TM, LANES), lambda i, j: (i, 0))],
        out_specs=[pl.BlockSpec((INPROJ_TM, INPROJ_TN), lambda i, j: (i, j)), strided_spec(d4), strided_spec(d16)],
        out_shape=[jax.ShapeDtypeStruct((T, P), BF16), strided_shape(d4), strided_shape(d16)],
        scratch_shapes=[pltpu.VMEM((INPROJ_TN // LANES, INPROJ_TM, LANES), F32)],
        compiler_params=_cparams(("parallel", "arbitrary")),
        name="in_proj",
    )(x2, scale1, shift1, w_in_bf, cos_t, sin_t)


def _attn_kernel(q_ref, k_ref, v_ref, o_ref, lse_ref, *, L, dilation):
    nq = L // ATTN_TQ
    res = pl.program_id(2)
    lane = lax.broadcasted_iota(I32, (1, LANES), 1)
    head0 = lane < ATTN_HEAD_DIM
    rel = (lax.broadcasted_iota(I32, (ATTN_TQ, ATTN_WK), 1)
           - lax.broadcasted_iota(I32, (ATTN_TQ, ATTN_WK), 0))

    def body(i, carry):
        q0 = pl.multiple_of(i * ATTN_TQ, ATTN_TQ)
        ks = pl.multiple_of(jnp.clip(i * ATTN_TQ - ATTN_HALF, 0, L - ATTN_WK), ATTN_HALF)
        q = q_ref[pl.ds(q0, ATTN_TQ), :]
        k = k_ref[pl.ds(ks, ATTN_WK), :]
        v = v_ref[pl.ds(ks, ATTN_WK), :]
        mask = jnp.abs(rel + (ks - q0)) <= ATTN_HALF
        outs, lses = [], []
        for hmask in (head0, jnp.logical_not(head0)):
            qh = jnp.where(hmask, q, jnp.zeros_like(q))
            s = lax.dot_general(qh, k, (((1,), (1,)), ((), ())), preferred_element_type=F32)
            s = jnp.where(mask, s, NEG_INF)
            m = jnp.max(s, axis=-1, keepdims=True)
            p = jnp.exp(s - m)
            l = jnp.sum(p, axis=-1, keepdims=True)
            outs.append(jnp.dot(p.astype(BF16), v, preferred_element_type=F32) / l)
            lses.append(m + jnp.log(l))
        if dilation == 1:
            rows = pl.ds(q0, ATTN_TQ)
        else:
            rows = pl.ds(q0 * dilation + res, ATTN_TQ, stride=dilation)
        o_ref[rows, :] = jnp.where(head0, outs[0], outs[1]).astype(o_ref.dtype)
        lse_ref[rows, :] = jnp.where(head0, lses[0], lses[1])
        return carry

    lax.fori_loop(0, nq, body, 0, unroll=min(ATTN_UNROLL, nq))


def _attn_branch(qkv, dilation, B, S):
    L = S // dilation
    acb = ATTN_WIDTH // LANES
    in_spec = lambda part: pl.BlockSpec((None, L, LANES),
                                        lambda b, h, r: (b, 0, (part * dilation + r) * acb + h))
    out_spec = pl.BlockSpec((None, S, LANES), lambda b, h, r: (b, 0, h))
    o_dtype = BF16 if dilation == 1 else F32
    o, lse = pl.pallas_call(
        functools.partial(_attn_kernel, L=L, dilation=dilation),
        grid=(B, acb, dilation),
        in_specs=[in_spec(0), in_spec(1), in_spec(2)],
        out_specs=[out_spec, out_spec],
        out_shape=[jax.ShapeDtypeStruct((B, S, ATTN_WIDTH), o_dtype),
                   jax.ShapeDtypeStruct((B, S, ATTN_WIDTH), F32)],
        compiler_params=_cparams(("parallel", "parallel", "arbitrary")),
        name=f"attn_d{dilation}",
    )(qkv, qkv, qkv)
    return o.reshape(B * S, ATTN_WIDTH), lse.reshape(B * S, ATTN_WIDTH)


def _hgrn_chunk(q, kk, lf, v_bf, tri, reverse):
    C = HGRN_CHUNK
    SUB = HGRN_SUB
    nsub = C // SUB
    b = jnp.dot(tri, lf, preferred_element_type=F32, precision=HIGHEST) * LOG2E
    c = b - jnp.log2(kk)
    col = lax.broadcasted_iota(I32, (SUB, C), 1)
    row = lax.broadcasted_iota(I32, (SUB, C), 0)
    score_rows = []
    for i in range(nsub):
        r0 = i * SUB
        bi = b[r0:r0 + SUB]
        qi = q[r0:r0 + SUB]
        ci = c[r0:r0 + SUB]
        diag = jnp.zeros((SUB, C), F32)
        for s in range(SUB):
            colv = jnp.sum(qi * jnp.exp2(bi - ci[s:s + 1]), axis=-1, keepdims=True)
            diag = jnp.where(col == r0 + s, colv, diag)
        if reverse:
            keep = (col - r0) >= row
        else:
            keep = (col - r0) <= row
        diag = jnp.where(jnp.logical_and(keep, jnp.logical_and(col >= r0, col < r0 + SUB)), diag, 0.0)
        if reverse:
            has_off = i < nsub - 1
            bref = b[r0 + SUB:r0 + SUB + 1] if has_off else None
            off_mask = col >= r0 + SUB
        else:
            has_off = i > 0
            bref = b[r0 - 1:r0] if has_off else None
            off_mask = col < r0
        if has_off:
            qs = (qi * jnp.exp2(bi - bref)).astype(BF16)
            ks = jnp.exp2(bref - c).astype(BF16)
            off = lax.dot_general(qs, ks, (((1,), (1,)), ((), ())), preferred_element_type=F32)
            score_rows.append(jnp.where(off_mask, off, diag))
        else:
            score_rows.append(diag)
    scores = jnp.concatenate(score_rows, axis=0)
    b_edge = b[0:1] if reverse else b[C - 1:C]
    o = jnp.dot(scores.astype(BF16), v_bf, preferred_element_type=F32)
    qd = (q * jnp.exp2(b)).astype(BF16)
    kd = jnp.exp2(b_edge - c).astype(BF16)
    return o, qd, kd, jnp.exp2(b_edge)


def _hgrn_kernel(q_ref, zf_ref, zb_ref, v_ref, g_ref, lbf_ref, lbb_ref, ng_ref, o_ref,
                 acc_ref, qd_ref, kd_ref, dec_ref, *, S):
    C = HGRN_CHUNK
    n_chunks = S // C
    r = lax.broadcasted_iota(I32, (C, C), 0)
    c = lax.broadcasted_iota(I32, (C, C), 1)
    tri_f = (c <= r).astype(F32)
    tri_b = (c >= r).astype(F32)
    lbf = lbf_ref[...]
    lbb = lbb_ref[...]

    def gates(z, lb):
        sg = jax.nn.sigmoid(z)
        return jnp.log(lb + (1.0 - lb) * sg), (1.0 - lb) * (1.0 - sg)

    def intra_body(n, carry):
        r0 = pl.multiple_of(n * C, C)
        q = q_ref[pl.ds(r0, C), :].astype(F32)
        v_bf = v_ref[pl.ds(r0, C), :]
        total = None
        for d, (z_ref, lb, tri) in enumerate(((zf_ref, lbf, tri_f), (zb_ref, lbb, tri_b))):
            lf, kk = gates(z_ref[pl.ds(r0, C), :].astype(F32), lb)
            o, qd, kd, dec = _hgrn_chunk(q, kk, lf, v_bf, tri, d == 1)
            qd_ref[d, pl.ds(r0, C), :] = qd
            kd_ref[d, pl.ds(r0, C), :] = kd
            dec_ref[d, pl.ds(n, 1), :] = dec
            total = o if total is None else total + o
        acc_ref[pl.ds(r0, C), :] = total
        return carry

    lax.fori_loop(0, n_chunks, intra_body, 0, unroll=HGRN_INTRA_UNROLL)

    def state_body(n, carry):
        states = list(carry)
        for d, blk in enumerate((n, n_chunks - 1 - n)):
            r0 = pl.multiple_of(blk * C, C)
            st = states[d]
            o = lax.dot_general(qd_ref[d, pl.ds(r0, C), :], st.astype(BF16), (((1,), (1,)), ((), ())),
                                preferred_element_type=F32)
            acc_ref[pl.ds(r0, C), :] += o
            upd = lax.dot_general(v_ref[pl.ds(r0, C), :], kd_ref[d, pl.ds(r0, C), :], (((0,), (0,)), ((), ())),
                                  preferred_element_type=F32)
            states[d] = st * dec_ref[d, pl.ds(blk, 1), :] + upd
        return tuple(states)

    z0 = jnp.zeros((HGRN_HEAD_DIM, HGRN_HEAD_DIM), F32)
    lax.fori_loop(0, n_chunks, state_body, (z0, z0), unroll=HGRN_STATE_UNROLL)

    ng = ng_ref[...]

    def norm_body(n, carry):
        r0 = pl.multiple_of(n * HGRN_NORM_ROWS, HGRN_NORM_ROWS)
        t = acc_ref[pl.ds(r0, HGRN_NORM_ROWS), :]
        g = g_ref[pl.ds(r0, HGRN_NORM_ROWS), :].astype(F32)
        y = t * lax.rsqrt(jnp.mean(t * t, axis=-1, keepdims=True) + RMS_EPS) * ng
        o_ref[pl.ds(r0, HGRN_NORM_ROWS), :] = (y * (g * jax.nn.sigmoid(g))).astype(BF16)
        return carry

    lax.fori_loop(0, S // HGRN_NORM_ROWS, norm_body, 0)


def _hgrn(proj, lb_fwd, lb_bwd, norm_g, B, S):
    P = proj.shape[1]
    pv = proj.reshape(B, S, P)
    base = 3 * ATTN_WIDTH // LANES
    nh = HGRN_HEADS
    in_spec = lambda k: pl.BlockSpec((None, S, LANES), lambda b, h: (b, 0, base + k * nh + h))
    vec_spec = pl.BlockSpec((None, 1, LANES), lambda b, h: (h, 0, 0))
    out = pl.pallas_call(
        functools.partial(_hgrn_kernel, S=S),
        grid=(B, nh),
        in_specs=[in_spec(0), in_spec(1), in_spec(2), in_spec(3), in_spec(4), vec_spec, vec_spec, vec_spec],
        out_specs=pl.BlockSpec((None, S, LANES), lambda b, h: (b, 0, h)),
        out_shape=jax.ShapeDtypeStruct((B, S, HGRN_WIDTH), BF16),
        scratch_shapes=[pltpu.VMEM((S, LANES), F32),
                        pltpu.VMEM((2, S, LANES), BF16),
                        pltpu.VMEM((2, S, LANES), BF16),
                        pltpu.VMEM((2, S // HGRN_CHUNK, LANES), F32)],
        compiler_params=_cparams(("parallel", "parallel")),
        name="hgrn2",
    )(pv, pv, pv, pv, pv, lb_fwd.reshape(nh, 1, LANES), lb_bwd.reshape(nh, 1, LANES),
      norm_g.reshape(nh, 1, LANES))
    return out.reshape(B * S, HGRN_WIDTH)


def _layer_norm_rows(y, g, b):
    mu = jnp.mean(y, axis=-1, keepdims=True)
    d = y - mu
    var = jnp.mean(d * d, axis=-1, keepdims=True)
    return d * lax.rsqrt(var + LN_EPS) * g + b


def _mix_kernel(o1_ref, o2_ref, o3_ref, l1_ref, l2_ref, l3_ref, rec_ref, x_ref, grp_ref, ag_ref, w_ref,
                gate_ref, lng_ref, lnb_ref, sc_ref, sh_ref, x1_ref, u2_ref, urows_ref):
    l1, l2, l3 = l1_ref[...], l2_ref[...], l3_ref[...]
    m = jnp.maximum(jnp.maximum(l1, l2), l3)
    e1, e2, e3 = jnp.exp(l1 - m), jnp.exp(l2 - m), jnp.exp(l3 - m)
    attn = (e1 * o1_ref[...].astype(F32) + e2 * o2_ref[...].astype(F32)
            + e3 * o3_ref[...].astype(F32)) / (e1 + e2 + e3)
    ms = jnp.dot((attn * attn).astype(BF16), grp_ref[...], preferred_element_type=F32)
    normed = attn * lax.rsqrt(ms + RMS_EPS) * ag_ref[...]
    mixed = jnp.concatenate([normed.astype(BF16), rec_ref[...]], axis=-1)
    mix = jnp.dot(mixed, w_ref[...], preferred_element_type=F32)
    x1 = _layer_norm_rows(ALPHA * x_ref[...] + gate_ref[...] * mix, lng_ref[...], lnb_ref[...])
    x1_ref[...] = x1
    u2 = x1 * (1.0 + sc_ref[...]) + sh_ref[...]
    u2_ref[...] = u2
    for cb in range(ROW_SUB):
        urows_ref[pl.ds(cb, MIX_TM, stride=ROW_SUB), :] = u2[:, cb * LANES:(cb + 1) * LANES]


def _mix(o_branches, lse_branches, rec, x2, attn_norm_g, w_out_bf, gate1, ln_g, ln_b, scale2, shift2, S):
    T, D = x2.shape
    per_b = S // MIX_TM
    head = jnp.arange(ATTN_WIDTH) // ATTN_HEAD_DIM
    grp = jnp.where(head[:, None] == head[None, :], 1.0 / ATTN_HEAD_DIM, 0.0).astype(BF16)
    row = lambda w: pl.BlockSpec((MIX_TM, w), lambda i: (i, 0))
    const = lambda shape: pl.BlockSpec(shape, lambda i: (0,) * len(shape))
    per_batch = pl.BlockSpec((None, 1, D), lambda i: (i // per_b, 0, 0))
    return pl.pallas_call(
        _mix_kernel,
        grid=(T // MIX_TM,),
        in_specs=[row(ATTN_WIDTH)] * 6 + [row(HGRN_WIDTH), row(D), const((ATTN_WIDTH, ATTN_WIDTH)),
                  const((1, ATTN_WIDTH)), const((D, D)), per_batch, const((1, D)), const((1, D)),
                  per_batch, per_batch],
        out_specs=[row(D), row(D), pl.BlockSpec((MIX_TM * ROW_SUB, LANES), lambda i: (i, 0))],
        out_shape=[jax.ShapeDtypeStruct((T, D), F32)] * 2 + [jax.ShapeDtypeStruct((T * ROW_SUB, LANES), F32)],
        compiler_params=_cparams(("parallel",)),
        name="mix_out_ln1",
    )(*o_branches, *lse_branches, rec, x2, grp, attn_norm_g.reshape(1, -1), w_out_bf, gate1,
      ln_g.reshape(1, D), ln_b.reshape(1, D), scale2, shift2)


def _route_kernel(u_ref, w_ref, bias_ref, idx_ref, gw_ref, rank_ref, cnt_ref, run_ref):
    i = pl.program_id(0)

    @pl.when(i == 0)
    def _():
        run_ref[...] = jnp.zeros_like(run_ref)

    tm = ROUTE_TM
    hm = tm // ROUTE_SPLIT
    logits = jnp.dot(u_ref[...], w_ref[...], preferred_element_type=F32, precision=HIGHEST)
    scores_all = jax.nn.sigmoid(logits)
    sel_all = scores_all + bias_ref[...]
    lane = lax.broadcasted_iota(I32, (hm, N_EXPERTS), 1)
    lane_f = lane.astype(F32)
    group = lane // GROUP_SIZE
    lane_o = lax.broadcasted_iota(I32, (hm, LANES), 1)
    neg = jnp.float32(-jnp.inf)

    def first_argmax(vals):
        mx = jnp.max(vals, axis=-1, keepdims=True)
        idx = jnp.min(jnp.where(vals == mx, lane_f, float(N_EXPERTS)), axis=-1, keepdims=True)
        return mx, idx

    def select(scores, sel):
        gscore = []
        for g in range(N_EXPERT_GROUPS):
            vals = jnp.where(group == g, sel, neg)
            m1, i1 = first_argmax(vals)
            m2 = jnp.max(jnp.where(lane_f == i1, neg, vals), axis=-1, keepdims=True)
            gscore.append(m1 + m2)
        keep_f = jnp.zeros((hm, N_EXPERTS), F32)
        for g in range(N_EXPERT_GROUPS):
            beaten = jnp.zeros((hm, 1), F32)
            for h in range(N_EXPERT_GROUPS):
                if h == g:
                    continue
                ahead = (gscore[h] >= gscore[g]) if h < g else (gscore[h] > gscore[g])
                beaten = beaten + jnp.where(ahead, 1.0, 0.0)
            gkeep = jnp.where(beaten < TOPK_GROUPS, 1.0, 0.0)
            keep_f = jnp.where(group == g, gkeep, keep_f)
        vals = jnp.where(keep_f > 0.5, sel, neg)

        idx_out = jnp.zeros((hm, LANES), F32)
        gw_out = jnp.zeros((hm, LANES), F32)
        chosen = jnp.zeros((hm, N_EXPERTS), F32)
        picks = []
        wsum = jnp.zeros((hm, 1), F32)
        for k in range(TOP_K):
            _, ik = first_argmax(vals)
            hit = lane_f == ik
            sk = jnp.sum(jnp.where(hit, scores, 0.0), axis=-1, keepdims=True)
            vals = jnp.where(hit, neg, vals)
            chosen = jnp.where(hit, 1.0, chosen)
            picks.append((ik, sk))
            wsum = wsum + sk
            idx_out = jnp.where(lane_o == k, ik, idx_out)
        for k, (ik, sk) in enumerate(picks):
            gw_out = jnp.where(lane_o == k, sk / wsum * ROUTED_SCALE, gw_out)
        return idx_out, gw_out, chosen, [ik for ik, _ in picks]

    parts = [select(scores_all[h * hm:(h + 1) * hm], sel_all[h * hm:(h + 1) * hm]) for h in range(ROUTE_SPLIT)]
    chosen = jnp.concatenate([p[2] for p in parts], axis=0)

    r = lax.broadcasted_iota(I32, (tm, tm), 0)
    c = lax.broadcasted_iota(I32, (tm, tm), 1)
    strict_lower = jnp.where(c < r, 1.0, 0.0).astype(BF16)
    before = jnp.dot(strict_lower, chosen.astype(BF16), preferred_element_type=F32) + run_ref[...]
    for h, (idx_out, gw_out, _, iks) in enumerate(parts):
        rows = slice(h * hm, (h + 1) * hm)
        rank_out = jnp.zeros((hm, LANES), F32)
        for k, ik in enumerate(iks):
            rk = jnp.sum(jnp.where(lane_f == ik, before[rows], 0.0), axis=-1, keepdims=True)
            rank_out = jnp.where(lane_o == k, rk, rank_out)
        idx_ref[rows, :] = idx_out.astype(I32)
        gw_ref[rows, :] = gw_out
        rank_ref[rows, :] = rank_out.astype(I32)
    run_ref[...] = run_ref[...] + jnp.sum(chosen, axis=0, keepdims=True)
    cnt_ref[...] = run_ref[...]


def _route(u2, w_router, router_bias):
    T, D = u2.shape
    row = lambda w: pl.BlockSpec((ROUTE_TM, w), lambda i: (i, 0))
    idx, gw, rank, cnt = pl.pallas_call(
        _route_kernel,
        grid=(T // ROUTE_TM,),
        in_specs=[row(D), pl.BlockSpec((D, N_EXPERTS), lambda i: (0, 0)),
                  pl.BlockSpec((1, N_EXPERTS), lambda i: (0, 0))],
        out_specs=[row(LANES), row(LANES), row(LANES), pl.BlockSpec((1, N_EXPERTS), lambda i: (0, 0))],
        out_shape=[jax.ShapeDtypeStruct((T, LANES), I32), jax.ShapeDtypeStruct((T, LANES), F32),
                   jax.ShapeDtypeStruct((T, LANES), I32), jax.ShapeDtypeStruct((1, N_EXPERTS), F32)],
        scratch_shapes=[pltpu.VMEM((1, N_EXPERTS), F32)],
        compiler_params=_cparams(("arbitrary",)),
        name="router",
    )(u2, w_router, router_bias.reshape(1, N_EXPERTS))
    return idx, gw, rank, cnt[0].astype(I32)


def _slot_kernel(idx_ref, rank_ref, start_ref, dest_ref):
    tm = ROUTE_TM
    lane_e = lax.broadcasted_iota(I32, (tm, N_EXPERTS), 1)
    lane_o = lax.broadcasted_iota(I32, (tm, LANES), 1)
    idx = idx_ref[...]
    start = start_ref[...]
    base = jnp.zeros((tm, LANES), F32)
    for k in range(TOP_K):
        sk = jnp.sum(jnp.where(lane_e == idx[:, k:k + 1], start, 0.0), axis=-1, keepdims=True)
        base = jnp.where(lane_o == k, sk, base)
    dest_ref[...] = base.astype(I32) + rank_ref[...]


def _slots(idx, rank, start):
    T = idx.shape[0]
    row = pl.BlockSpec((ROUTE_TM, LANES), lambda i: (i, 0))
    dest = pl.pallas_call(
        _slot_kernel,
        grid=(T // ROUTE_TM,),
        in_specs=[row, row, pl.BlockSpec((1, N_EXPERTS), lambda i: (0, 0))],
        out_specs=row,
        out_shape=jax.ShapeDtypeStruct((T, LANES), I32),
        compiler_params=_cparams(("parallel",)),
        name="slots",
    )(idx, rank, start.astype(F32).reshape(1, N_EXPERTS))
    return dest[:, :TOP_K]


MOE_META_SLOTS = 8
MOE_BUFS = 3


def _moe_kernel(bexp_ref, wnext_ref, nused_ref, meta_hbm, u_hbm, wg_hbm, wu_hbm, wd_hbm, y_hbm,
                meta_smem, xb0, xb1, xb2, yb0, yb1, yb2, wg_st, wu_st, wd_st, wg_bf, wu_bf, wd_bf,
                sem_meta, sem_g, sem_s, sem_w):
    i = pl.program_id(0)
    n_blocks = pl.num_programs(0)
    nused = nused_ref[0]
    last = nused - 1
    xbufs = (xb0, xb1, xb2)
    ybufs = (yb0, yb1, yb2)

    def slot(blk):
        return blk & (MOE_META_SLOTS - 1)

    def meta_copy(blk, sl):
        row = pl.ds(pl.multiple_of(blk * (2 * MOE_BLOCK), 2 * MOE_BLOCK), 2 * MOE_BLOCK)
        return pltpu.make_async_copy(meta_hbm.at[row], meta_smem.at[sl], sem_meta.at[sl])

    def weight_copies(e):
        return (pltpu.make_async_copy(wg_hbm.at[e], wg_st, sem_w.at[0]),
                pltpu.make_async_copy(wu_hbm.at[e], wu_st, sem_w.at[1]),
                pltpu.make_async_copy(wd_hbm.at[e], wd_st, sem_w.at[2]))

    def gather_rows(sl, xdst, sem):
        for j in range(MOE_BLOCK):
            row0 = pl.multiple_of(meta_smem[sl, j] * ROW_SUB, ROW_SUB)
            pltpu.make_async_copy(u_hbm.at[pl.ds(row0, ROW_SUB)], xdst.at[pl.ds(j * ROW_PITCH, ROW_SUB)], sem).start()

    def scatter_rows(sl, ysrc, sem):
        for j in range(MOE_BLOCK):
            dst = meta_smem[sl, MOE_BLOCK + j]
            row0 = pl.multiple_of(dst * ROW_SUB, ROW_SUB)
            pltpu.make_async_copy(ysrc.at[pl.ds(j * ROW_PITCH, ROW_SUB)], y_hbm.at[pl.ds(row0, ROW_SUB)],
                                  sem).start(priority=1)

    def wait_gather(sem):
        n = MOE_BLOCK * ROW_SUB
        pltpu.make_async_copy(u_hbm.at[pl.ds(0, n)], xb0.at[pl.ds(0, n)], sem).wait()

    def wait_scatter(sem):
        n = MOE_BLOCK * ROW_SUB
        pltpu.make_async_copy(yb0.at[pl.ds(0, n)], yb1.at[pl.ds(0, n)], sem).wait()

    @pl.when(i >= nused)
    def _():
        @pl.when(i == nused)
        def _():
            yb0[...] = jnp.zeros_like(yb0)

        @pl.when(i > nused)
        def _():
            wait_scatter(sem_s.at[0])

        n = MOE_BLOCK * ROW_SUB
        pltpu.make_async_copy(yb0.at[pl.ds(0, n)], y_hbm.at[pl.ds(pl.multiple_of(i * n, n), n)], sem_s.at[0]).start()

        @pl.when(i == n_blocks - 1)
        def _():
            wait_scatter(sem_s.at[0])

    @pl.when(i < nused)
    def _():
        @pl.when(i == 0)
        def _():
            for blk, sl in ((0, 0), (n_blocks, slot(-1)), (jnp.minimum(1, last), 1)):
                cp = meta_copy(blk, sl)
                cp.start()
                cp.wait()
            meta_copy(jnp.minimum(2, last), 2).start()
            yb2[...] = jnp.zeros_like(yb2)
            gather_rows(0, xb0, sem_g.at[0])
            gather_rows(1, xb1, sem_g.at[1])
            for cp in weight_copies(bexp_ref[0]):
                cp.start(priority=1)

        meta_copy(jnp.minimum(i + 2, last), slot(i + 2)).wait()
        meta_copy(jnp.minimum(i + 3, last), slot(i + 3)).start()

        @pl.when((i == 0) | (bexp_ref[i] != bexp_ref[jnp.maximum(i - 1, 0)]))
        def _():
            for cp in weight_copies(bexp_ref[i]):
                cp.wait()
            wg_bf[...] = wg_st[...].astype(BF16)
            wu_bf[...] = wu_st[...].astype(BF16)
            wd_bf[...] = wd_st[...].astype(BF16)
            nxt_e = wnext_ref[i]

            @pl.when(nxt_e >= 0)
            def _():
                for cp in weight_copies(nxt_e):
                    cp.start(priority=1)

    def step(r):
        r_prev = (r + MOE_BUFS - 1) % MOE_BUFS
        xcur, ycur = xbufs[r], ybufs[r]
        wait_gather(sem_g.at[r])
        gather_rows(slot(i + 2), xbufs[r_prev], sem_g.at[r_prev])
        scatter_rows(slot(i - 1), ybufs[r_prev], sem_s.at[r_prev])
        x = jnp.concatenate([xcur[pl.ds(cb, MOE_BLOCK, stride=ROW_PITCH), :].astype(BF16)
                             for cb in range(ROW_SUB)], axis=1)
        hg = jnp.dot(x, wg_bf[...], preferred_element_type=F32)
        hu = jnp.dot(x, wu_bf[...], preferred_element_type=F32)
        act = (hg * jax.nn.sigmoid(hg) * hu).astype(BF16)

        @pl.when(i >= 2)
        def _():
            wait_scatter(sem_s.at[r])

        y = jnp.dot(act, wd_bf[...], preferred_element_type=F32)
        for cb in range(ROW_SUB):
            ycur[pl.ds(cb, MOE_BLOCK, stride=ROW_PITCH), :] = y[:, cb * LANES:(cb + 1) * LANES]

        @pl.when(i == last)
        def _():
            scatter_rows(slot(i), ycur, sem_s.at[r])

    for r in range(MOE_BUFS):
        @pl.when((i < nused) & (i % MOE_BUFS == r))
        def _(r=r):
            step(r)

    @pl.when(i == last)
    def _():
        wait_scatter(sem_s.at[i % MOE_BUFS])
        wait_scatter(sem_s.at[(i + 2) % MOE_BUFS])

        @pl.when(i >= 1)
        def _():
            wait_scatter(sem_s.at[(i + 1) % MOE_BUFS])

        wait_gather(sem_g.at[(i + 1) % MOE_BUFS])
        wait_gather(sem_g.at[(i + 2) % MOE_BUFS])
        meta_copy(jnp.minimum(i + 3, last), slot(i + 3)).wait()


def _moe(u_rows, bexp, wnext, nused, meta, w_gate, w_up, w_down, n_pad):
    D = ROW_SUB * LANES
    n_blocks = n_pad // MOE_BLOCK
    E = EXPERT_DIM
    grid_spec = pltpu.PrefetchScalarGridSpec(
        num_scalar_prefetch=3,
        grid=(n_blocks,),
        in_specs=[pl.BlockSpec(memory_space=pl.ANY)] * 5,
        out_specs=pl.BlockSpec(memory_space=pl.ANY),
        scratch_shapes=[pltpu.SMEM((MOE_META_SLOTS, 2 * MOE_BLOCK), I32)]
                       + [pltpu.VMEM((MOE_BLOCK * ROW_PITCH, LANES), F32)] * MOE_BUFS
                       + [pltpu.VMEM((MOE_BLOCK * ROW_PITCH, LANES), F32)] * MOE_BUFS + [
                        pltpu.VMEM((D, E), F32),
                        pltpu.VMEM((D, E), F32),
                        pltpu.VMEM((E, D), F32),
                        pltpu.VMEM((D, E), BF16),
                        pltpu.VMEM((D, E), BF16),
                        pltpu.VMEM((E, D), BF16),
                        pltpu.SemaphoreType.DMA((MOE_META_SLOTS,)),
                        pltpu.SemaphoreType.DMA((MOE_BUFS,)),
                        pltpu.SemaphoreType.DMA((MOE_BUFS,)),
                        pltpu.SemaphoreType.DMA((3,))])
    return pl.pallas_call(
        _moe_kernel,
        grid_spec=grid_spec,
        out_shape=jax.ShapeDtypeStruct(((n_pad + MOE_BLOCK) * ROW_SUB, LANES), F32),
        compiler_params=_cparams(("arbitrary",)),
        name="moe_experts",
    )(bexp, wnext, nused, meta, u_rows, w_gate, w_up, w_down)


def _dispatch_plan(idx, rank, counts, T):
    n_assign = T * TOP_K
    n_blocks = -(-(n_assign + N_EXPERTS * (MOE_BLOCK - 1)) // MOE_BLOCK)
    n_pad = n_blocks * MOE_BLOCK
    padded = (counts + MOE_BLOCK - 1) // MOE_BLOCK * MOE_BLOCK
    padded_end = jnp.cumsum(padded)
    start = padded_end - padded
    dest = _slots(idx, rank, start)
    src = jnp.full((n_pad,), -1, I32).at[dest.reshape(-1)].set(
        jnp.arange(n_assign, dtype=I32), unique_indices=True, mode='promise_in_bounds')
    is_pad = src < 0
    pad_rank = jnp.cumsum(is_pad.astype(I32)) - 1
    tok = jnp.where(is_pad, 0, src // TOP_K)
    dst = jnp.where(is_pad, n_assign + pad_rank, src)
    tok = jnp.concatenate([tok, jnp.zeros((MOE_BLOCK,), I32)]).reshape(n_blocks + 1, MOE_BLOCK)
    dst = jnp.concatenate([dst, n_pad + jnp.arange(MOE_BLOCK, dtype=I32)]).reshape(n_blocks + 1, MOE_BLOCK)
    meta = jnp.concatenate([tok, dst], axis=1).reshape(-1)
    block_row0 = jnp.arange(n_blocks, dtype=I32) * MOE_BLOCK
    bexp = jnp.minimum(jnp.sum((padded_end[None, :] <= block_row0[:, None]).astype(I32), axis=1),
                       N_EXPERTS - 1).astype(I32)
    nused = (padded_end[-1] // MOE_BLOCK).astype(I32).reshape(1)
    eid = jnp.arange(N_EXPERTS, dtype=I32)
    later = (eid[None, :] > eid[:, None]) & (counts[None, :] > 0)
    next_e = jnp.min(jnp.where(later, eid[None, :], N_EXPERTS), axis=1)
    onehot = (bexp[:, None] == eid[None, :]).astype(I32)
    wnext = jnp.sum(onehot * next_e[None, :], axis=1)
    wnext = jnp.where(wnext < N_EXPERTS, wnext, -1).astype(I32)
    return bexp, wnext, nused, meta.astype(I32), n_pad


def _final_kernel(u_ref, y_ref, gw_ref, x1_ref, wg_ref, wu_ref, wd_ref, gate_ref, lng_ref, lnb_ref, o_ref):
    tm = FINAL_TM
    nrow = tm * TOP_K
    u = u_ref[...].astype(BF16)
    hg = jnp.dot(u, wg_ref[...], preferred_element_type=F32)
    hu = jnp.dot(u, wu_ref[...], preferred_element_type=F32)
    act = (hg * jax.nn.sigmoid(hg) * hu).astype(BF16)
    ffn = jnp.dot(act, wd_ref[...], preferred_element_type=F32)
    expand = jnp.where((lax.broadcasted_iota(I32, (nrow, tm), 0) // TOP_K) == lax.broadcasted_iota(I32, (nrow, tm), 1),
                       1.0, 0.0)
    gw_rows = jnp.dot(expand, gw_ref[...], preferred_element_type=F32, precision=HIGHEST)
    pick = lax.broadcasted_iota(I32, (nrow, LANES), 1) == (lax.broadcasted_iota(I32, (nrow, LANES), 0) % TOP_K)
    wcol = jnp.sum(jnp.where(pick, gw_rows, 0.0), axis=-1, keepdims=True)
    yrows = jnp.concatenate([y_ref[pl.ds(cb, nrow, stride=ROW_SUB), :] for cb in range(ROW_SUB)], axis=1)
    yw = (yrows * wcol).astype(BF16)
    fold = jnp.where((lax.broadcasted_iota(I32, (tm, nrow), 1) // TOP_K) == lax.broadcasted_iota(I32, (tm, nrow), 0),
                     1.0, 0.0).astype(BF16)
    ffn = ffn + jnp.dot(fold, yw, preferred_element_type=F32)
    o_ref[...] = _layer_norm_rows(ALPHA * x1_ref[...] + gate_ref[...] * ffn, lng_ref[...], lnb_ref[...])


def _final(u2, y_flat, gw, x1, ws_gate_bf, ws_up_bf, ws_down_bf, gate2, ln_g, ln_b, S):
    T, D = u2.shape
    E = ws_gate_bf.shape[1]
    per_b = S // FINAL_TM
    row = lambda w: pl.BlockSpec((FINAL_TM, w), lambda i: (i, 0))
    const = lambda shape: pl.BlockSpec(shape, lambda i: (0,) * len(shape))
    return pl.pallas_call(
        _final_kernel,
        grid=(T // FINAL_TM,),
        in_specs=[row(D), pl.BlockSpec((FINAL_TM * TOP_K * ROW_SUB, LANES), lambda i: (i, 0)), row(LANES), row(D),
                  const((D, E)), const((D, E)), const((E, D)),
                  pl.BlockSpec((None, 1, D), lambda i: (i // per_b, 0, 0)), const((1, D)), const((1, D))],
        out_specs=row(D),
        out_shape=jax.ShapeDtypeStruct((T, D), F32),
        compiler_params=_cparams(("parallel",)),
        name="shared_combine_ln2",
    )(u2, y_flat, gw, x1, ws_gate_bf, ws_up_bf, ws_down_bf, gate2, ln_g.reshape(1, D), ln_b.reshape(1, D))


def kernel(x, c, positions, w_ada, b_ada, w_in, lb_logits, attn_norm_g, hgrn_norm_g, w_out, ln1_g, ln1_b,
           w_router, router_bias, expert_w_gate, expert_w_up, expert_w_down, shared_w_gate, shared_w_up,
           shared_w_down, ln2_g, ln2_b):
    B, S, D = x.shape
    T = B * S
    layer = 0
    lower_bounds = jnp.cumsum(jax.nn.softmax(lb_logits.astype(F32), axis=1), axis=1)

    mod = _ada_mod(c, w_ada[layer], b_ada[layer])
    shift1, scale1, gate1, shift2, scale2, gate2 = [m.reshape(B, 1, D) for m in jnp.split(mod, 6, axis=-1)]
    cos_t, sin_t = _rope_tables(positions)
    x2 = x.reshape(T, D)

    proj, qkv4, qkv16 = _in_proj(x2, scale1, shift1, w_in[layer].astype(BF16), cos_t, sin_t, B, S)
    branches = [_attn_branch(qkv, d, B, S)
                for qkv, d in zip((proj.reshape(B, S, -1), qkv4, qkv16), DILATIONS)]
    rec = _hgrn(proj, lower_bounds[0, layer], lower_bounds[1, layer], hgrn_norm_g[layer], B, S)
    x1, u2, u_rows = _mix([o for o, _ in branches], [l for _, l in branches], rec, x2, attn_norm_g[layer],
                  w_out[layer].astype(BF16), gate1, ln1_g[layer], ln1_b[layer], scale2, shift2, S)

    idx, gw, rank, counts = _route(u2, w_router[layer], router_bias[layer])
    bexp, wnext, nused, meta, n_pad = _dispatch_plan(idx, rank, counts, T)
    y_flat = _moe(u_rows, bexp, wnext, nused, meta, expert_w_gate[layer], expert_w_up[layer], expert_w_down[layer], n_pad)
    out = _final(u2, y_flat, gw, x1, shared_w_gate[layer].astype(BF16), shared_w_up[layer].astype(BF16),
                 shared_w_down[layer].astype(BF16), gate2, ln2_g[layer], ln2_b[layer], S)
    return out.reshape(B, S, D)
```

```python
import functools

import jax
import jax.numpy as jnp
from jax import lax
from jax.experimental import pallas as pl
from jax.experimental.pallas import tpu as pltpu

F32 = jnp.float32
BF16 = jnp.bfloat16
I32 = jnp.int32
HIGHEST = lax.Precision.HIGHEST

D_MODEL = 2048
ATTN_HEADS = 16
ATTN_HEAD_DIM = 64
ATTN_WIDTH = ATTN_HEADS * ATTN_HEAD_DIM
HGRN_HEADS = 8
HGRN_HEAD_DIM = 128
HGRN_WIDTH = HGRN_HEADS * HGRN_HEAD_DIM
IN_PROJ_WIDTH = 3 * ATTN_WIDTH + 5 * HGRN_WIDTH
DILATIONS = (1, 4, 16)
ATTN_HALF = 64
ROPE_THETA = 10000.0
N_EXPERTS = 256
N_EXPERT_GROUPS = 8
GROUP_SIZE = N_EXPERTS // N_EXPERT_GROUPS
TOPK_GROUPS = 4
TOP_K = 8
EXPERT_DIM = 512
ROUTED_SCALE = 2.5
DEPTH = 1
ALPHA = (2 * DEPTH) ** 0.25
LN_EPS = 1e-5
RMS_EPS = 1e-6
NEG_INF = -1e30
LOG2E = 1.4426950408889634

LANES = 128
SUBLANES = 8
VMEM_LIMIT = 56 * 1024 * 1024

ADA_TN = 1024
ROPE_TM = 2048
INPROJ_TM = 1024
INPROJ_TN = 1024
ATTN_TQ = 128
ATTN_WK = 256
ATTN_UNROLL = 8
HGRN_CHUNK = 64
HGRN_SUB = 16
HGRN_NORM_ROWS = 512
HGRN_STATE_UNROLL = 4
HGRN_INTRA_UNROLL = 4
MIX_TM = 256
ROUTE_TM = 256
ROUTE_SPLIT = 2
MOE_BLOCK = 128
ROW_SUB = D_MODEL // LANES
ROW_PITCH = 24
FINAL_TM = 128


def _cparams(sem):
    return pltpu.CompilerParams(dimension_semantics=sem, vmem_limit_bytes=VMEM_LIMIT)


def _ada_kernel(c_ref, w_ref, b_ref, o_ref):
    c = c_ref[...]
    sc = c * jax.nn.sigmoid(c)
    o_ref[...] = jnp.dot(sc, w_ref[...], preferred_element_type=F32, precision=HIGHEST) + b_ref[...]


def _ada_mod(c, w_ada, b_ada):
    B, D = c.shape
    N = w_ada.shape[1]
    c8 = jnp.zeros((SUBLANES, D), F32).at[:B].set(c)
    out = pl.pallas_call(
        _ada_kernel,
        grid=(N // ADA_TN,),
        in_specs=[pl.BlockSpec((SUBLANES, D), lambda j: (0, 0)),
                  pl.BlockSpec((D, ADA_TN), lambda j: (0, j)),
                  pl.BlockSpec((1, ADA_TN), lambda j: (0, j))],
        out_specs=pl.BlockSpec((SUBLANES, ADA_TN), lambda j: (0, j)),
        out_shape=jax.ShapeDtypeStruct((SUBLANES, N), F32),
        compiler_params=_cparams(("parallel",)),
        name="ada_mod",
    )(c8, w_ada, b_ada.reshape(1, N))
    return out[:B]


def _rope_kernel(pos_ref, invf_ref, sign_ref, cos_ref, sin_ref):
    ang = pos_ref[...].astype(F32) * invf_ref[...]
    cos_ref[...] = jnp.cos(ang)
    sin_ref[...] = jnp.sin(ang) * sign_ref[...]


def _rope_tables(positions):
    T = positions.size
    half = ATTN_HEAD_DIM // 2
    inv_freq = ROPE_THETA ** (-jnp.arange(half, dtype=F32) / half)
    lane = jnp.arange(LANES)
    invf = inv_freq[lane % half].reshape(1, LANES)
    sign = jnp.where((lane % ATTN_HEAD_DIM) < half, -1.0, 1.0).astype(F32).reshape(1, LANES)
    return pl.pallas_call(
        _rope_kernel,
        grid=(T // ROPE_TM,),
        in_specs=[pl.BlockSpec((ROPE_TM, 1), lambda i: (i, 0)),
                  pl.BlockSpec((1, LANES), lambda i: (0, 0)),
                  pl.BlockSpec((1, LANES), lambda i: (0, 0))],
        out_specs=[pl.BlockSpec((ROPE_TM, LANES), lambda i: (i, 0)),
                   pl.BlockSpec((ROPE_TM, LANES), lambda i: (i, 0))],
        out_shape=[jax.ShapeDtypeStruct((T, LANES), F32)] * 2,
        compiler_params=_cparams(("parallel",)),
        name="rope_tables",
    )(positions.reshape(T, 1), invf, sign)


def _inproj_kernel(x_ref, sc_ref, sh_ref, w_ref, cos_ref, sin_ref, o_ref, o4_ref, o16_ref, stage_ref):
    j = pl.program_id(1)
    u = (x_ref[...] * (1.0 + sc_ref[...]) + sh_ref[...]).astype(BF16)
    acc = jnp.dot(u, w_ref[...], preferred_element_type=F32)

    @pl.when(j < 2)
    def _():
        qscale = jnp.where(j == 0, ATTN_HEAD_DIM ** -0.5, 1.0).astype(F32)
        cos = cos_ref[...] * qscale
        sin = sin_ref[...] * qscale
        lane = lax.broadcasted_iota(I32, cos.shape, 1)
        first = (lane % ATTN_HEAD_DIM) < (ATTN_HEAD_DIM // 2)
        for cb in range(INPROJ_TN // LANES):
            a = acc[:, cb * LANES:(cb + 1) * LANES]
            partner = jnp.where(first, pltpu.roll(a, LANES - ATTN_HEAD_DIM // 2, 1),
                                pltpu.roll(a, ATTN_HEAD_DIM // 2, 1))
            stage_ref[cb] = a * cos + partner * sin

    @pl.when(j == 2)
    def _():
        for cb in range(INPROJ_TN // LANES):
            stage_ref[cb] = acc[:, cb * LANES:(cb + 1) * LANES]

    @pl.when(j < 3)
    def _():
        for cb in range(INPROJ_TN // LANES):
            o_ref[:, cb * LANES:(cb + 1) * LANES] = stage_ref[cb].astype(BF16)
            for d, od_ref in ((DILATIONS[1], o4_ref), (DILATIONS[2], o16_ref)):
                rows = INPROJ_TM // d
                for r in range(d):
                    c0 = r * INPROJ_TN + cb * LANES
                    od_ref[:, c0:c0 + LANES] = stage_ref.at[cb][pl.ds(r, rows, stride=d), :].astype(BF16)

    @pl.when(j >= 3)
    def _():
        o_ref[...] = acc.astype(BF16)


def _in_proj(x2, scale1, shift1, w_in_bf, cos_t, sin_t, B, S):
    T, D = x2.shape
    P = w_in_bf.shape[1]
    per_b = S // INPROJ_TM
    assert INPROJ_TN == ATTN_WIDTH
    d4, d16 = DILATIONS[1], DILATIONS[2]
    strided_spec = lambda d: pl.BlockSpec((None, INPROJ_TM // d, d * INPROJ_TN),
                                          lambda i, j: (i // per_b, i % per_b, jnp.minimum(j, 2)))
    strided_shape = lambda d: jax.ShapeDtypeStruct((B, S // d, 3 * d * ATTN_WIDTH), BF16)
    return pl.pallas_call(
        _inproj_kernel,
        grid=(T // INPROJ_TM, P // INPROJ_TN),
        in_specs=[pl.BlockSpec((INPROJ_TM, D), lambda i, j: (i, 0)),
                  pl.BlockSpec((None, 1, D), lambda i, j: (i // per_b, 0, 0)),
                  pl.BlockSpec((None, 1, D), lambda i, j: (i // per_b, 0, 0)),
                  pl.BlockSpec((D, INPROJ_TN), lambda i, j: (0, j)),
                  pl.BlockSpec((INPROJ_TM, LANES), lambda i, j: (i, 0)),
                  pl.BlockSpec((INPROJ_TM, LANES), lambda i, j: (i, 0))],
        out_specs=[pl.BlockSpec((INPROJ_TM, INPROJ_TN), lambda i, j: (i, j)), strided_spec(d4), strided_spec(d16)],
        out_shape=[jax.ShapeDtypeStruct((T, P), BF16), strided_shape(d4), strided_shape(d16)],
        scratch_shapes=[pltpu.VMEM((INPROJ_TN // LANES, INPROJ_TM, LANES), F32)],
        compiler_params=_cparams(("parallel", "arbitrary")),
        name="in_proj",
    )(x2, scale1, shift1, w_in_bf, cos_t, sin_t)


def _attn_kernel(q_ref, k_ref, v_ref, o_ref, lse_ref, *, L, dilation):
    nq = L // ATTN_TQ
    res = pl.program_id(2)
    lane = lax.broadcasted_iota(I32, (1, LANES), 1)
    head0 = lane < ATTN_HEAD_DIM
    rel = (lax.broadcasted_iota(I32, (ATTN_TQ, ATTN_WK), 1)
           - lax.broadcasted_iota(I32, (ATTN_TQ, ATTN_WK), 0))

    def body(i, carry):
        q0 = pl.multiple_of(i * ATTN_TQ, ATTN_TQ)
        ks = pl.multiple_of(jnp.clip(i * ATTN_TQ - ATTN_HALF, 0, L - ATTN_WK), ATTN_HALF)
        q = q_ref[pl.ds(q0, ATTN_TQ), :]
        k = k_ref[pl.ds(ks, ATTN_WK), :]
        v = v_ref[pl.ds(ks, ATTN_WK), :]
        mask = jnp.abs(rel + (ks - q0)) <= ATTN_HALF
        outs, lses = [], []
        for hmask in (head0, jnp.logical_not(head0)):
            qh = jnp.where(hmask, q, jnp.zeros_like(q))
            s = lax.dot_general(qh, k, (((1,), (1,)), ((), ())), preferred_element_type=F32)
            s = jnp.where(mask, s, NEG_INF)
            m = jnp.max(s, axis=-1, keepdims=True)
            p = jnp.exp(s - m)
            l = jnp.sum(p, axis=-1, keepdims=True)
            outs.append(jnp.dot(p.astype(BF16), v, preferred_element_type=F32) / l)
            lses.append(m + jnp.log(l))
        if dilation == 1:
            rows = pl.ds(q0, ATTN_TQ)
        else:
            rows = pl.ds(q0 * dilation + res, ATTN_TQ, stride=dilation)
        o_ref[rows, :] = jnp.where(head0, outs[0], outs[1]).astype(o_ref.dtype)
        lse_ref[rows, :] = jnp.where(head0, lses[0], lses[1])
        return carry

    lax.fori_loop(0, nq, body, 0, unroll=min(ATTN_UNROLL, nq))


def _attn_branch(qkv, dilation, B, S):
    L = S // dilation
    acb = ATTN_WIDTH // LANES
    in_spec = lambda part: pl.BlockSpec((None, L, LANES),
                                        lambda b, h, r: (b, 0, (part * dilation + r) * acb + h))
    out_spec = pl.BlockSpec((None, S, LANES), lambda b, h, r: (b, 0, h))
    o_dtype = BF16 if dilation == 1 else F32
    o, lse = pl.pallas_call(
        functools.partial(_attn_kernel, L=L, dilation=dilation),
        grid=(B, acb, dilation),
        in_specs=[in_spec(0), in_spec(1), in_spec(2)],
        out_specs=[out_spec, out_spec],
        out_shape=[jax.ShapeDtypeStruct((B, S, ATTN_WIDTH), o_dtype),
                   jax.ShapeDtypeStruct((B, S, ATTN_WIDTH), F32)],
        compiler_params=_cparams(("parallel", "parallel", "arbitrary")),
        name=f"attn_d{dilation}",
    )(qkv, qkv, qkv)
    return o.reshape(B * S, ATTN_WIDTH), lse.reshape(B * S, ATTN_WIDTH)


def _hgrn_chunk(q, kk, lf, v_bf, tri, reverse):
    C = HGRN_CHUNK
    SUB = HGRN_SUB
    nsub = C // SUB
    b = jnp.dot(tri, lf, preferred_element_type=F32, precision=HIGHEST) * LOG2E
    c = b - jnp.log2(kk)
    col = lax.broadcasted_iota(I32, (SUB, C), 1)
    row = lax.broadcasted_iota(I32, (SUB, C), 0)
    score_rows = []
    for i in range(nsub):
        r0 = i * SUB
        bi = b[r0:r0 + SUB]
        qi = q[r0:r0 + SUB]
        ci = c[r0:r0 + SUB]
        diag = jnp.zeros((SUB, C), F32)
        for s in range(SUB):
            colv = jnp.sum(qi * jnp.exp2(bi - ci[s:s + 1]), axis=-1, keepdims=True)
            diag = jnp.where(col == r0 + s, colv, diag)
        if reverse:
            keep = (col - r0) >= row
        else:
            keep = (col - r0) <= row
        diag = jnp.where(jnp.logical_and(keep, jnp.logical_and(col >= r0, col < r0 + SUB)), diag, 0.0)
        if reverse:
            has_off = i < nsub - 1
            bref = b[r0 + SUB:r0 + SUB + 1] if has_off else None
            off_mask = col >= r0 + SUB
        else:
            has_off = i > 0
            bref = b[r0 - 1:r0] if has_off else None
            off_mask = col < r0
        if has_off:
            qs = (qi * jnp.exp2(bi - bref)).astype(BF16)
            ks = jnp.exp2(bref - c).astype(BF16)
            off = lax.dot_general(qs, ks, (((1,), (1,)), ((), ())), preferred_element_type=F32)
            score_rows.append(jnp.where(off_mask, off, diag))
        else:
            score_rows.append(diag)
    scores = jnp.concatenate(score_rows, axis=0)
    b_edge = b[0:1] if reverse else b[C - 1:C]
    o = jnp.dot(scores.astype(BF16), v_bf, preferred_element_type=F32)
    qd = (q * jnp.exp2(b)).astype(BF16)
    kd = jnp.exp2(b_edge - c).astype(BF16)
    return o, qd, kd, jnp.exp2(b_edge)


def _hgrn_kernel(q_ref, zf_ref, zb_ref, v_ref, g_ref, lbf_ref, lbb_ref, ng_ref, o_ref,
                 acc_ref, qd_ref, kd_ref, dec_ref, *, S):
    C = HGRN_CHUNK
    n_chunks = S // C
    r = lax.broadcasted_iota(I32, (C, C), 0)
    c = lax.broadcasted_iota(I32, (C, C), 1)
    tri_f = (c <= r).astype(F32)
    tri_b = (c >= r).astype(F32)
    lbf = lbf_ref[...]
    lbb = lbb_ref[...]

    def gates(z, lb):
        sg = jax.nn.sigmoid(z)
        return jnp.log(lb + (1.0 - lb) * sg), (1.0 - lb) * (1.0 - sg)

    def intra_body(n, carry):
        r0 = pl.multiple_of(n * C, C)
        q = q_ref[pl.ds(r0, C), :].astype(F32)
        v_bf = v_ref[pl.ds(r0, C), :]
        total = None
        for d, (z_ref, lb, tri) in enumerate(((zf_ref, lbf, tri_f), (zb_ref, lbb, tri_b))):
            lf, kk = gates(z_ref[pl.ds(r0, C), :].astype(F32), lb)
            o, qd, kd, dec = _hgrn_chunk(q, kk, lf, v_bf, tri, d == 1)
            qd_ref[d, pl.ds(r0, C), :] = qd
            kd_ref[d, pl.ds(r0, C), :] = kd
            dec_ref[d, pl.ds(n, 1), :] = dec
            total = o if total is None else total + o
        acc_ref[pl.ds(r0, C), :] = total
        return carry

    lax.fori_loop(0, n_chunks, intra_body, 0, unroll=HGRN_INTRA_UNROLL)

    def state_body(n, carry):
        states = list(carry)
        for d, blk in enumerate((n, n_chunks - 1 - n)):
            r0 = pl.multiple_of(blk * C, C)
            st = states[d]
            o = lax.dot_general(qd_ref[d, pl.ds(r0, C), :], st.astype(BF16), (((1,), (1,)), ((), ())),
                                preferred_element_type=F32)
            acc_ref[pl.ds(r0, C), :] += o
            upd = lax.dot_general(v_ref[pl.ds(r0, C), :], kd_ref[d, pl.ds(r0, C), :], (((0,), (0,)), ((), ())),
                                  preferred_element_type=F32)
            states[d] = st * dec_ref[d, pl.ds(blk, 1), :] + upd
        return tuple(states)

    z0 = jnp.zeros((HGRN_HEAD_DIM, HGRN_HEAD_DIM), F32)
    lax.fori_loop(0, n_chunks, state_body, (z0, z0), unroll=HGRN_STATE_UNROLL)

    ng = ng_ref[...]

    def norm_body(n, carry):
        r0 = pl.multiple_of(n * HGRN_NORM_ROWS, HGRN_NORM_ROWS)
        t = acc_ref[pl.ds(r0, HGRN_NORM_ROWS), :]
        g = g_ref[pl.ds(r0, HGRN_NORM_ROWS), :].astype(F32)
        y = t * lax.rsqrt(jnp.mean(t * t, axis=-1, keepdims=True) + RMS_EPS) * ng
        o_ref[pl.ds(r0, HGRN_NORM_ROWS), :] = (y * (g * jax.nn.sigmoid(g))).astype(BF16)
        return carry

    lax.fori_loop(0, S // HGRN_NORM_ROWS, norm_body, 0)


def _hgrn(proj, lb_fwd, lb_bwd, norm_g, B, S):
    P = proj.shape[1]
    pv = proj.reshape(B, S, P)
    base = 3 * ATTN_WIDTH // LANES
    nh = HGRN_HEADS
    in_spec = lambda k: pl.BlockSpec((None, S, LANES), lambda b, h: (b, 0, base + k * nh + h))
    vec_spec = pl.BlockSpec((None, 1, LANES), lambda b, h: (h, 0, 0))
    out = pl.pallas_call(
        functools.partial(_hgrn_kernel, S=S),
        grid=(B, nh),
        in_specs=[in_spec(0), in_spec(1), in_spec(2), in_spec(3), in_spec(4), vec_spec, vec_spec, vec_spec],
        out_specs=pl.BlockSpec((None, S, LANES), lambda b, h: (b, 0, h)),
        out_shape=jax.ShapeDtypeStruct((B, S, HGRN_WIDTH), BF16),
        scratch_shapes=[pltpu.VMEM((S, LANES), F32),
                        pltpu.VMEM((2, S, LANES), BF16),
                        pltpu.VMEM((2, S, LANES), BF16),
                        pltpu.VMEM((2, S // HGRN_CHUNK, LANES), F32)],
        compiler_params=_cparams(("parallel", "parallel")),
        name="hgrn2",
    )(pv, pv, pv, pv, pv, lb_fwd.reshape(nh, 1, LANES), lb_bwd.reshape(nh, 1, LANES),
      norm_g.reshape(nh, 1, LANES))
    return out.reshape(B * S, HGRN_WIDTH)


def _layer_norm_rows(y, g, b):
    mu = jnp.mean(y, axis=-1, keepdims=True)
    d = y - mu
    var = jnp.mean(d * d, axis=-1, keepdims=True)
    return d * lax.rsqrt(var + LN_EPS) * g + b


def _mix_kernel(o1_ref, o2_ref, o3_ref, l1_ref, l2_ref, l3_ref, rec_ref, x_ref, grp_ref, ag_ref, w_ref,
                gate_ref, lng_ref, lnb_ref, sc_ref, sh_ref, x1_ref, u2_ref, urows_ref):
    l1, l2, l3 = l1_ref[...], l2_ref[...], l3_ref[...]
    m = jnp.maximum(jnp.maximum(l1, l2), l3)
    e1, e2, e3 = jnp.exp(l1 - m), jnp.exp(l2 - m), jnp.exp(l3 - m)
    attn = (e1 * o1_ref[...].astype(F32) + e2 * o2_ref[...].astype(F32)
            + e3 * o3_ref[...].astype(F32)) / (e1 + e2 + e3)
    ms = jnp.dot((attn * attn).astype(BF16), grp_ref[...], preferred_element_type=F32)
    normed = attn * lax.rsqrt(ms + RMS_EPS) * ag_ref[...]
    mixed = jnp.concatenate([normed.astype(BF16), rec_ref[...]], axis=-1)
    mix = jnp.dot(mixed, w_ref[...], preferred_element_type=F32)
    x1 = _layer_norm_rows(ALPHA * x_ref[...] + gate_ref[...] * mix, lng_ref[...], lnb_ref[...])
    x1_ref[...] = x1
    u2 = x1 * (1.0 + sc_ref[...]) + sh_ref[...]
    u2_ref[...] = u2
    for cb in range(ROW_SUB):
        urows_ref[pl.ds(cb, MIX_TM, stride=ROW_SUB), :] = u2[:, cb * LANES:(cb + 1) * LANES]


def _mix(o_branches, lse_branches, rec, x2, attn_norm_g, w_out_bf, gate1, ln_g, ln_b, scale2, shift2, S):
    T, D = x2.shape
    per_b = S // MIX_TM
    head = jnp.arange(ATTN_WIDTH) // ATTN_HEAD_DIM
    grp = jnp.where(head[:, None] == head[None, :], 1.0 / ATTN_HEAD_DIM, 0.0).astype(BF16)
    row = lambda w: pl.BlockSpec((MIX_TM, w), lambda i: (i, 0))
    const = lambda shape: pl.BlockSpec(shape, lambda i: (0,) * len(shape))
    per_batch = pl.BlockSpec((None, 1, D), lambda i: (i // per_b, 0, 0))
    return pl.pallas_call(
        _mix_kernel,
        grid=(T // MIX_TM,),
        in_specs=[row(ATTN_WIDTH)] * 6 + [row(HGRN_WIDTH), row(D), const((ATTN_WIDTH, ATTN_WIDTH)),
                  const((1, ATTN_WIDTH)), const((D, D)), per_batch, const((1, D)), const((1, D)),
                  per_batch, per_batch],
        out_specs=[row(D), row(D), pl.BlockSpec((MIX_TM * ROW_SUB, LANES), lambda i: (i, 0))],
        out_shape=[jax.ShapeDtypeStruct((T, D), F32)] * 2 + [jax.ShapeDtypeStruct((T * ROW_SUB, LANES), F32)],
        compiler_params=_cparams(("parallel",)),
        name="mix_out_ln1",
    )(*o_branches, *lse_branches, rec, x2, grp, attn_norm_g.reshape(1, -1), w_out_bf, gate1,
      ln_g.reshape(1, D), ln_b.reshape(1, D), scale2, shift2)


def _route_kernel(u_ref, w_ref, bias_ref, idx_ref, gw_ref, rank_ref, cnt_ref, run_ref):
    i = pl.program_id(0)

    @pl.when(i == 0)
    def _():
        run_ref[...] = jnp.zeros_like(run_ref)

    tm = ROUTE_TM
    hm = tm // ROUTE_SPLIT
    logits = jnp.dot(u_ref[...], w_ref[...], preferred_element_type=F32, precision=HIGHEST)
    scores_all = jax.nn.sigmoid(logits)
    sel_all = scores_all + bias_ref[...]
    lane = lax.broadcasted_iota(I32, (hm, N_EXPERTS), 1)
    lane_f = lane.astype(F32)
    group = lane // GROUP_SIZE
    lane_o = lax.broadcasted_iota(I32, (hm, LANES), 1)
    neg = jnp.float32(-jnp.inf)

    def first_argmax(vals):
        mx = jnp.max(vals, axis=-1, keepdims=True)
        idx = jnp.min(jnp.where(vals == mx, lane_f, float(N_EXPERTS)), axis=-1, keepdims=True)
        return mx, idx

    def select(scores, sel):
        gscore = []
        for g in range(N_EXPERT_GROUPS):
            vals = jnp.where(group == g, sel, neg)
            m1, i1 = first_argmax(vals)
            m2 = jnp.max(jnp.where(lane_f == i1, neg, vals), axis=-1, keepdims=True)
            gscore.append(m1 + m2)
        keep_f = jnp.zeros((hm, N_EXPERTS), F32)
        for g in range(N_EXPERT_GROUPS):
            beaten = jnp.zeros((hm, 1), F32)
            for h in range(N_EXPERT_GROUPS):
                if h == g:
                    continue
                ahead = (gscore[h] >= gscore[g]) if h < g else (gscore[h] > gscore[g])
                beaten = beaten + jnp.where(ahead, 1.0, 0.0)
            gkeep = jnp.where(beaten < TOPK_GROUPS, 1.0, 0.0)
            keep_f = jnp.where(group == g, gkeep, keep_f)
        vals = jnp.where(keep_f > 0.5, sel, neg)

        idx_out = jnp.zeros((hm, LANES), F32)
        gw_out = jnp.zeros((hm, LANES), F32)
        chosen = jnp.zeros((hm, N_EXPERTS), F32)
        picks = []
        wsum = jnp.zeros((hm, 1), F32)
        for k in range(TOP_K):
            _, ik = first_argmax(vals)
            hit = lane_f == ik
            sk = jnp.sum(jnp.where(hit, scores, 0.0), axis=-1, keepdims=True)
            vals = jnp.where(hit, neg, vals)
            chosen = jnp.where(hit, 1.0, chosen)
            picks.append((ik, sk))
            wsum = wsum + sk
            idx_out = jnp.where(lane_o == k, ik, idx_out)
        for k, (ik, sk) in enumerate(picks):
            gw_out = jnp.where(lane_o == k, sk / wsum * ROUTED_SCALE, gw_out)
        return idx_out, gw_out, chosen, [ik for ik, _ in picks]

    parts = [select(scores_all[h * hm:(h + 1) * hm], sel_all[h * hm:(h + 1) * hm]) for h in range(ROUTE_SPLIT)]
    chosen = jnp.concatenate([p[2] for p in parts], axis=0)

    r = lax.broadcasted_iota(I32, (tm, tm), 0)
    c = lax.broadcasted_iota(I32, (tm, tm), 1)
    strict_lower = jnp.where(c < r, 1.0, 0.0).astype(BF16)
    before = jnp.dot(strict_lower, chosen.astype(BF16), preferred_element_type=F32) + run_ref[...]
    for h, (idx_out, gw_out, _, iks) in enumerate(parts):
        rows = slice(h * hm, (h + 1) * hm)
        rank_out = jnp.zeros((hm, LANES), F32)
        for k, ik in enumerate(iks):
            rk = jnp.sum(jnp.where(lane_f == ik, before[rows], 0.0), axis=-1, keepdims=True)
            rank_out = jnp.where(lane_o == k, rk, rank_out)
        idx_ref[rows, :] = idx_out.astype(I32)
        gw_ref[rows, :] = gw_out
        rank_ref[rows, :] = rank_out.astype(I32)
    run_ref[...] = run_ref[...] + jnp.sum(chosen, axis=0, keepdims=True)
    cnt_ref[...] = run_ref[...]


def _route(u2, w_router, router_bias):
    T, D = u2.shape
    row = lambda w: pl.BlockSpec((ROUTE_TM, w), lambda i: (i, 0))
    idx, gw, rank, cnt = pl.pallas_call(
        _route_kernel,
        grid=(T // ROUTE_TM,),
        in_specs=[row(D), pl.BlockSpec((D, N_EXPERTS), lambda i: (0, 0)),
                  pl.BlockSpec((1, N_EXPERTS), lambda i: (0, 0))],
        out_specs=[row(LANES), row(LANES), row(LANES), pl.BlockSpec((1, N_EXPERTS), lambda i: (0, 0))],
        out_shape=[jax.ShapeDtypeStruct((T, LANES), I32), jax.ShapeDtypeStruct((T, LANES), F32),
                   jax.ShapeDtypeStruct((T, LANES), I32), jax.ShapeDtypeStruct((1, N_EXPERTS), F32)],
        scratch_shapes=[pltpu.VMEM((1, N_EXPERTS), F32)],
        compiler_params=_cparams(("arbitrary",)),
        name="router",
    )(u2, w_router, router_bias.reshape(1, N_EXPERTS))
    return idx, gw, rank, cnt[0].astype(I32)


def _slot_kernel(idx_ref, rank_ref, start_ref, dest_ref):
    tm = ROUTE_TM
    lane_e = lax.broadcasted_iota(I32, (tm, N_EXPERTS), 1)
    lane_o = lax.broadcasted_iota(I32, (tm, LANES), 1)
    idx = idx_ref[...]
    start = start_ref[...]
    base = jnp.zeros((tm, LANES), F32)
    for k in range(TOP_K):
        sk = jnp.sum(jnp.where(lane_e == idx[:, k:k + 1], start, 0.0), axis=-1, keepdims=True)
        base = jnp.where(lane_o == k, sk, base)
    dest_ref[...] = base.astype(I32) + rank_ref[...]


def _slots(idx, rank, start):
    T = idx.shape[0]
    row = pl.BlockSpec((ROUTE_TM, LANES), lambda i: (i, 0))
    dest = pl.pallas_call(
        _slot_kernel,
        grid=(T // ROUTE_TM,),
        in_specs=[row, row, pl.BlockSpec((1, N_EXPERTS), lambda i: (0, 0))],
        out_specs=row,
        out_shape=jax.ShapeDtypeStruct((T, LANES), I32),
        compiler_params=_cparams(("parallel",)),
        name="slots",
    )(idx, rank, start.astype(F32).reshape(1, N_EXPERTS))
    return dest[:, :TOP_K]


MOE_META_SLOTS = 8
MOE_BUFS = 3


def _moe_kernel(bexp_ref, wnext_ref, wslot_ref, nused_ref, meta_hbm, u_hbm, wg_hbm, wu_hbm, wd_hbm, y_hbm,
                meta_smem, xb0, xb1, xb2, yb0, yb1, yb2, wg_st, wu_st, wd_st,
                sem_meta, sem_g, sem_s, sem_w):
    i = pl.program_id(0)
    n_blocks = pl.num_programs(0)
    nused = nused_ref[0]
    last = nused - 1
    xbufs = (xb0, xb1, xb2)
    ybufs = (yb0, yb1, yb2)

    def slot(blk):
        return blk & (MOE_META_SLOTS - 1)

    def meta_copy(blk, sl):
        row = pl.ds(pl.multiple_of(blk * (2 * MOE_BLOCK), 2 * MOE_BLOCK), 2 * MOE_BLOCK)
        return pltpu.make_async_copy(meta_hbm.at[row], meta_smem.at[sl], sem_meta.at[sl])

    def weight_copies(e, ws):
        return (pltpu.make_async_copy(wg_hbm.at[e], wg_st.at[ws], sem_w.at[3 * ws]),
                pltpu.make_async_copy(wu_hbm.at[e], wu_st.at[ws], sem_w.at[3 * ws + 1]),
                pltpu.make_async_copy(wd_hbm.at[e], wd_st.at[ws], sem_w.at[3 * ws + 2]))

    def gather_rows(sl, xdst, sem):
        for j in range(MOE_BLOCK):
            row0 = pl.multiple_of(meta_smem[sl, j] * ROW_SUB, ROW_SUB)
            pltpu.make_async_copy(u_hbm.at[pl.ds(row0, ROW_SUB)], xdst.at[pl.ds(j * ROW_PITCH, ROW_SUB)], sem).start()

    def scatter_rows(sl, ysrc, sem):
        for j in range(MOE_BLOCK):
            dst = meta_smem[sl, MOE_BLOCK + j]
            row0 = pl.multiple_of(dst * ROW_SUB, ROW_SUB)
            pltpu.make_async_copy(ysrc.at[pl.ds(j * ROW_PITCH, ROW_SUB)], y_hbm.at[pl.ds(row0, ROW_SUB)],
                                  sem).start(priority=1)

    def wait_gather(sem):
        n = MOE_BLOCK * ROW_SUB
        pltpu.make_async_copy(u_hbm.at[pl.ds(0, n)], xb0.at[pl.ds(0, n)], sem).wait()

    def wait_scatter(sem):
        n = MOE_BLOCK * ROW_SUB
        pltpu.make_async_copy(yb0.at[pl.ds(0, n)], yb1.at[pl.ds(0, n)], sem).wait()

    @pl.when(i >= nused)
    def _():
        @pl.when(i == nused)
        def _():
            yb0[...] = jnp.zeros_like(yb0)

        @pl.when(i > nused)
        def _():
            wait_scatter(sem_s.at[0])

        n = MOE_BLOCK * ROW_SUB
        pltpu.make_async_copy(yb0.at[pl.ds(0, n)], y_hbm.at[pl.ds(pl.multiple_of(i * n, n), n)], sem_s.at[0]).start()

        @pl.when(i == n_blocks - 1)
        def _():
            wait_scatter(sem_s.at[0])

    @pl.when(i < nused)
    def _():
        @pl.when(i == 0)
        def _():
            for blk, sl in ((0, 0), (n_blocks, slot(-1)), (jnp.minimum(1, last), 1)):
                cp = meta_copy(blk, sl)
                cp.start()
                cp.wait()
            meta_copy(jnp.minimum(2, last), 2).start()
            yb2[...] = jnp.zeros_like(yb2)
            gather_rows(0, xb0, sem_g.at[0])
            gather_rows(1, xb1, sem_g.at[1])
            for cp in weight_copies(bexp_ref[0], wslot_ref[0]):
                cp.start(priority=1)

        meta_copy(jnp.minimum(i + 2, last), slot(i + 2)).wait()
        meta_copy(jnp.minimum(i + 3, last), slot(i + 3)).start()

        @pl.when((i == 0) | (bexp_ref[i] != bexp_ref[jnp.maximum(i - 1, 0)]))
        def _():
            for cp in weight_copies(bexp_ref[i], wslot_ref[i]):
                cp.wait()
            nxt_e = wnext_ref[i]

            @pl.when(nxt_e >= 0)
            def _():
                for cp in weight_copies(nxt_e, 1 - wslot_ref[i]):
                    cp.start(priority=1)

    def step(r):
        r_prev = (r + MOE_BUFS - 1) % MOE_BUFS
        xcur, ycur = xbufs[r], ybufs[r]
        wait_gather(sem_g.at[r])
        gather_rows(slot(i + 2), xbufs[r_prev], sem_g.at[r_prev])
        scatter_rows(slot(i - 1), ybufs[r_prev], sem_s.at[r_prev])
        x = jnp.concatenate([xcur[pl.ds(cb, MOE_BLOCK, stride=ROW_PITCH), :].astype(BF16)
                             for cb in range(ROW_SUB)], axis=1)
        ws = wslot_ref[i]
        hg = jnp.dot(x, wg_st[ws].astype(BF16), preferred_element_type=F32)
        hu = jnp.dot(x, wu_st[ws].astype(BF16), preferred_element_type=F32)
        act = (hg * jax.nn.sigmoid(hg) * hu).astype(BF16)

        @pl.when(i >= 2)
        def _():
            wait_scatter(sem_s.at[r])

        y = jnp.dot(act, wd_st[ws].astype(BF16), preferred_element_type=F32)
        for cb in range(ROW_SUB):
            ycur[pl.ds(cb, MOE_BLOCK, stride=ROW_PITCH), :] = y[:, cb * LANES:(cb + 1) * LANES]

        @pl.when(i == last)
        def _():
            scatter_rows(slot(i), ycur, sem_s.at[r])

    for r in range(MOE_BUFS):
        @pl.when((i < nused) & (i % MOE_BUFS == r))
        def _(r=r):
            step(r)

    @pl.when(i == last)
    def _():
        wait_scatter(sem_s.at[i % MOE_BUFS])
        wait_scatter(sem_s.at[(i + 2) % MOE_BUFS])

        @pl.when(i >= 1)
        def _():
            wait_scatter(sem_s.at[(i + 1) % MOE_BUFS])

        wait_gather(sem_g.at[(i + 1) % MOE_BUFS])
        wait_gather(sem_g.at[(i + 2) % MOE_BUFS])
        meta_copy(jnp.minimum(i + 3, last), slot(i + 3)).wait()


def _moe(u_rows, bexp, wnext, wslot, nused, meta, w_gate, w_up, w_down, n_pad):
    D = ROW_SUB * LANES
    n_blocks = n_pad // MOE_BLOCK
    E = EXPERT_DIM
    grid_spec = pltpu.PrefetchScalarGridSpec(
        num_scalar_prefetch=4,
        grid=(n_blocks,),
        in_specs=[pl.BlockSpec(memory_space=pl.ANY)] * 5,
        out_specs=pl.BlockSpec(memory_space=pl.ANY),
        scratch_shapes=[pltpu.SMEM((MOE_META_SLOTS, 2 * MOE_BLOCK), I32)]
                       + [pltpu.VMEM((MOE_BLOCK * ROW_PITCH, LANES), F32)] * MOE_BUFS
                       + [pltpu.VMEM((MOE_BLOCK * ROW_PITCH, LANES), F32)] * MOE_BUFS + [
                        pltpu.VMEM((2, D, E), F32),
                        pltpu.VMEM((2, D, E), F32),
                        pltpu.VMEM((2, E, D), F32),
                        pltpu.SemaphoreType.DMA((MOE_META_SLOTS,)),
                        pltpu.SemaphoreType.DMA((MOE_BUFS,)),
                        pltpu.SemaphoreType.DMA((MOE_BUFS,)),
                        pltpu.SemaphoreType.DMA((6,))])
    return pl.pallas_call(
        _moe_kernel,
        grid_spec=grid_spec,
        out_shape=jax.ShapeDtypeStruct(((n_pad + MOE_BLOCK) * ROW_SUB, LANES), F32),
        compiler_params=_cparams(("arbitrary",)),
        name="moe_experts",
    )(bexp, wnext, wslot, nused, meta, u_rows, w_gate, w_up, w_down)


def _dispatch_plan(idx, rank, counts, T):
    n_assign = T * TOP_K
    n_blocks = -(-(n_assign + N_EXPERTS * (MOE_BLOCK - 1)) // MOE_BLOCK)
    n_pad = n_blocks * MOE_BLOCK
    padded = (counts + MOE_BLOCK - 1) // MOE_BLOCK * MOE_BLOCK
    padded_end = jnp.cumsum(padded)
    start = padded_end - padded
    dest = _slots(idx, rank, start)
    src = jnp.full((n_pad,), -1, I32).at[dest.reshape(-1)].set(
        jnp.arange(n_assign, dtype=I32), unique_indices=True, mode='promise_in_bounds')
    is_pad = src < 0
    pad_rank = jnp.cumsum(is_pad.astype(I32)) - 1
    tok = jnp.where(is_pad, 0, src // TOP_K)
    dst = jnp.where(is_pad, n_assign + pad_rank, src)
    tok = jnp.concatenate([tok, jnp.zeros((MOE_BLOCK,), I32)]).reshape(n_blocks + 1, MOE_BLOCK)
    dst = jnp.concatenate([dst, n_pad + jnp.arange(MOE_BLOCK, dtype=I32)]).reshape(n_blocks + 1, MOE_BLOCK)
    meta = jnp.concatenate([tok, dst], axis=1).reshape(-1)
    block_row0 = jnp.arange(n_blocks, dtype=I32) * MOE_BLOCK
    bexp = jnp.minimum(jnp.sum((padded_end[None, :] <= block_row0[:, None]).astype(I32), axis=1),
                       N_EXPERTS - 1).astype(I32)
    nused = (padded_end[-1] // MOE_BLOCK).astype(I32).reshape(1)
    eid = jnp.arange(N_EXPERTS, dtype=I32)
    later = (eid[None, :] > eid[:, None]) & (counts[None, :] > 0)
    next_e = jnp.min(jnp.where(later, eid[None, :], N_EXPERTS), axis=1)
    onehot = (bexp[:, None] == eid[None, :]).astype(I32)
    wnext = jnp.sum(onehot * next_e[None, :], axis=1)
    wnext = jnp.where(wnext < N_EXPERTS, wnext, -1).astype(I32)
    nonempty = (counts > 0).astype(I32)
    run_index = jnp.cumsum(nonempty) - nonempty
    wslot = (jnp.sum(onehot * run_index[None, :], axis=1) % 2).astype(I32)
    return bexp, wnext, wslot, nused, meta.astype(I32), n_pad


def _final_kernel(u_ref, y_ref, gw_ref, x1_ref, wg_ref, wu_ref, wd_ref, gate_ref, lng_ref, lnb_ref, o_ref):
    tm = FINAL_TM
    nrow = tm * TOP_K
    u = u_ref[...].astype(BF16)
    hg = jnp.dot(u, wg_ref[...], preferred_element_type=F32)
    hu = jnp.dot(u, wu_ref[...], preferred_element_type=F32)
    act = (hg * jax.nn.sigmoid(hg) * hu).astype(BF16)
    ffn = jnp.dot(act, wd_ref[...], preferred_element_type=F32)
    expand = jnp.where((lax.broadcasted_iota(I32, (nrow, tm), 0) // TOP_K) == lax.broadcasted_iota(I32, (nrow, tm), 1),
                       1.0, 0.0)
    gw_rows = jnp.dot(expand, gw_ref[...], preferred_element_type=F32, precision=HIGHEST)
    pick = lax.broadcasted_iota(I32, (nrow, LANES), 1) == (lax.broadcasted_iota(I32, (nrow, LANES), 0) % TOP_K)
    wcol = jnp.sum(jnp.where(pick, gw_rows, 0.0), axis=-1, keepdims=True)
    yrows = jnp.concatenate([y_ref[pl.ds(cb, nrow, stride=ROW_SUB), :] for cb in range(ROW_SUB)], axis=1)
    yw = (yrows * wcol).astype(BF16)
    fold = jnp.where((lax.broadcasted_iota(I32, (tm, nrow), 1) // TOP_K) == lax.broadcasted_iota(I32, (tm, nrow), 0),
                     1.0, 0.0).astype(BF16)
    ffn = ffn + jnp.dot(fold, yw, preferred_element_type=F32)
    o_ref[...] = _layer_norm_rows(ALPHA * x1_ref[...] + gate_ref[...] * ffn, lng_ref[...], lnb_ref[...])


def _final(u2, y_flat, gw, x1, ws_gate_bf, ws_up_bf, ws_down_bf, gate2, ln_g, ln_b, S):
    T, D = u2.shape
    E = ws_gate_bf.shape[1]
    per_b = S // FINAL_TM
    row = lambda w: pl.BlockSpec((FINAL_TM, w), lambda i: (i, 0))
    const = lambda shape: pl.BlockSpec(shape, lambda i: (0,) * len(shape))
    return pl.pallas_call(
        _final_kernel,
        grid=(T // FINAL_TM,),
        in_specs=[row(D), pl.BlockSpec((FINAL_TM * TOP_K * ROW_SUB, LANES), lambda i: (i, 0)), row(LANES), row(D),
                  const((D, E)), const((D, E)), const((E, D)),
                  pl.BlockSpec((None, 1, D), lambda i: (i // per_b, 0, 0)), const((1, D)), const((1, D))],
        out_specs=row(D),
        out_shape=jax.ShapeDtypeStruct((T, D), F32),
        compiler_params=_cparams(("parallel",)),
        name="shared_combine_ln2",
    )(u2, y_flat, gw, x1, ws_gate_bf, ws_up_bf, ws_down_bf, gate2, ln_g.reshape(1, D), ln_b.reshape(1, D))


def kernel(x, c, positions, w_ada, b_ada, w_in, lb_logits, attn_norm_g, hgrn_norm_g, w_out, ln1_g, ln1_b,
           w_router, router_bias, expert_w_gate, expert_w_up, expert_w_down, shared_w_gate, shared_w_up,
           shared_w_down, ln2_g, ln2_b):
    B, S, D = x.shape
    T = B * S
    layer = 0
    lower_bounds = jnp.cumsum(jax.nn.softmax(lb_logits.astype(F32), axis=1), axis=1)

    mod = _ada_mod(c, w_ada[layer], b_ada[layer])
    shift1, scale1, gate1, shift2, scale2, gate2 = [m.reshape(B, 1, D) for m in jnp.split(mod, 6, axis=-1)]
    cos_t, sin_t = _rope_tables(positions)
    x2 = x.reshape(T, D)

    proj, qkv4, qkv16 = _in_proj(x2, scale1, shift1, w_in[layer].astype(BF16), cos_t, sin_t, B, S)
    branches = [_attn_branch(qkv, d, B, S)
                for qkv, d in zip((proj.reshape(B, S, -1), qkv4, qkv16), DILATIONS)]
    rec = _hgrn(proj, lower_bounds[0, layer], lower_bounds[1, layer], hgrn_norm_g[layer], B, S)
    x1, u2, u_rows = _mix([o for o, _ in branches], [l for _, l in branches], rec, x2, attn_norm_g[layer],
                  w_out[layer].astype(BF16), gate1, ln1_g[layer], ln1_b[layer], scale2, shift2, S)

    idx, gw, rank, counts = _route(u2, w_router[layer], router_bias[layer])
    bexp, wnext, wslot, nused, meta, n_pad = _dispatch_plan(idx, rank, counts, T)
    y_flat = _moe(u_rows, bexp, wnext, wslot, nused, meta, expert_w_gate[layer], expert_w_up[layer], expert_w_down[layer], n_pad)
    out = _final(u2, y_flat, gw, x1, shared_w_gate[layer].astype(BF16), shared_w_up[layer].astype(BF16),
                 shared_w_down[layer].astype(BF16), gate2, ln2_g[layer], ln2_b[layer], S)
    return out.reshape(B, S, D)
```
